```python
import math
import jax, jax.numpy as jnp
from jax import lax
import numpy as np

D_MODEL = 2048
BATCH = 2
SEQ = 4096
DEPTH = 2
DEC_BATCH = 8
DEC_SEQ = 4
PAST_LEN = 16384
PAGE_SIZE = 128

ATT_GROUPS = ((128, 1), (512, 4), (2048, 16))
N_ATT_GROUPS = 3
ATT_HEADS = 4
ATT_HEAD_DIM = 128
ATT_WIDTH = ATT_HEADS * ATT_HEAD_DIM
ATT_SCALE = ATT_HEAD_DIM ** -0.5
ROPE_DIM = ATT_HEAD_DIM // 4
ROPE_THETA = 500000.0
ML_HEADS = 8
ML_QK_DIM = D_MODEL // 16
ML_V_DIM = D_MODEL // 8
ML_QK_WIDTH = ML_HEADS * ML_QK_DIM
ML_V_WIDTH = ML_HEADS * ML_V_DIM
ML_CHUNK = 128
N_GROUPS = 4
EXP_PER_GROUP = 8
N_EXPERTS = N_GROUPS * EXP_PER_GROUP
TOP_K = 2
EXPERT_FF = D_MODEL // 8
EPS = 1e-6
IN_SPLITS = (N_ATT_GROUPS * ATT_WIDTH, N_ATT_GROUPS * ATT_WIDTH, N_ATT_GROUPS * ATT_WIDTH,
             ML_QK_WIDTH, ML_QK_WIDTH, ML_V_WIDTH, ML_V_WIDTH, ML_HEADS, ML_HEADS, D_MODEL, D_MODEL)
N_IN = sum(IN_SPLITS)

kernel_name = "gated_dilated_attn_mlstm_hmoe_step"


def rmsnorm(x, g):
    xf = x.astype(jnp.float32)
    r = lax.rsqrt(jnp.mean(xf * xf, axis=-1, keepdims=True) + EPS)
    return (xf * r).astype(x.dtype) * g


def rope_partial(x, pos):
    half = ROPE_DIM // 2
    inv = jnp.power(jnp.float32(ROPE_THETA), -jnp.arange(half, dtype=jnp.float32) / half)
    ang = pos.astype(jnp.float32)[:, None] * inv[None, :]
    cos = jnp.cos(ang)[None, :, None, :]
    sin = jnp.sin(ang)[None, :, None, :]
    x1 = x[..., :half]
    x2 = x[..., half:ROPE_DIM]
    return jnp.concatenate([x1 * cos - x2 * sin, x2 * cos + x1 * sin, x[..., ROPE_DIM:]], axis=-1)


def window_attn_prompt(q, k, v, window, dilation):
    B, S, H, Dh = q.shape
    nk = window // dilation
    span = nk * dilation
    s_pad = -(-S // span) * span
    nb = s_pad // span

    def to_blocks(a):
        a = jnp.pad(a, ((0, 0), (0, s_pad - S), (0, 0), (0, 0)))
        return a.reshape(B, nb, nk, dilation, H, Dh)

    qb, kb, vb = to_blocks(q), to_blocks(k), to_blocks(v)

    def with_prev(a):
        prev = jnp.concatenate([jnp.zeros_like(a[:, :1]), a[:, :-1]], axis=1)
        return jnp.concatenate([prev, a], axis=2)

    kk, vv = with_prev(kb), with_prev(vb)
    s = jnp.einsum('bnqrhd,bnkrhd->bnrhqk', qb, kk) * ATT_SCALE
    qi = jnp.arange(nk)[:, None]
    kj = jnp.arange(2 * nk)[None, :]
    delta = nk + qi - kj
    band = (delta >= 0) & (delta <= nk)
    blk = jnp.arange(nb)[:, None, None]
    mask = band[None] & ((blk > 0) | (kj[None] >= nk))
    s = jnp.where(mask[None, :, None, None], s, -jnp.inf)
    m = jnp.max(s, axis=-1)
    p = jnp.exp(s - m[..., None])
    l = jnp.sum(p, axis=-1)
    acc = jnp.einsum('bnrhqk,bnkrhd->bnqrhd', p, vv).reshape(B, s_pad, H, Dh)[:, :S]
    m = m.transpose(0, 1, 4, 2, 3).reshape(B, s_pad, H)[:, :S]
    l = l.transpose(0, 1, 4, 2, 3).reshape(B, s_pad, H)[:, :S]
    return m, l, acc


def window_attn_sample(q, k_new, v_new, k_buf, v_buf, window, dilation):
    T = q.shape[1]
    wb = k_buf.shape[1]
    nk = window // dilation
    k_all = jnp.concatenate([k_buf, k_new], axis=1)
    v_all = jnp.concatenate([v_buf, v_new], axis=1)
    idx = wb + jnp.arange(T)[:, None] - dilation * jnp.arange(nk + 1)[None, :]
    valid = idx >= 0
    idc = jnp.maximum(idx, 0)
    kg = k_all[:, idc]
    vg = v_all[:, idc]
    s = jnp.einsum('bthd,btkhd->bthk', q, kg) * ATT_SCALE
    s = jnp.where(valid[None, :, None, :], s, -jnp.inf)
    m = jnp.max(s, axis=-1)
    p = jnp.exp(s - m[..., None])
    l = jnp.sum(p, axis=-1)
    acc = jnp.einsum('bthk,btkhd->bthd', p, vg)
    return m, l, acc, k_all, v_all


def mlstm(q, k, v, ig, lf, C0, n0, m0):
    B, S, H, dk = q.shape
    dv = v.shape[-1]
    L = math.gcd(S, ML_CHUNK)
    nc = S // L

    def chunk4(a):
        return a.reshape(B, nc, L, H, a.shape[-1]).transpose(1, 0, 3, 2, 4)

    def chunk3(a):
        return a.reshape(B, nc, L, H).transpose(1, 0, 3, 2)

    causal = jnp.tril(jnp.ones((L, L), dtype=bool))

    def step(carry, inp):
        C, n, m = carry
        qc, kc, vc, igc, lfc = inp
        b = jnp.cumsum(lfc, axis=-1)
        dmat = b[..., :, None] - b[..., None, :] + igc[..., None, :]
        dmat = jnp.where(causal, dmat, -jnp.inf)
        inter = b + m[..., None]
        m_t = jnp.maximum(inter, jnp.max(dmat, axis=-1))
        w_intra = jnp.exp(dmat - m_t[..., None])
        w_inter = jnp.exp(inter - m_t)
        sw = jnp.einsum('bhtd,bhsd->bhts', qc, kc) * w_intra
        num = w_inter[..., None] * jnp.einsum('bhvd,bhtd->bhtv', C, qc) + jnp.einsum('bhts,bhsv->bhtv', sw, vc)
        den = w_inter * jnp.einsum('bhd,bhtd->bht', n, qc) + jnp.sum(sw, axis=-1)
        h = num / jnp.maximum(jnp.abs(den), jnp.exp(-m_t))[..., None]
        b_last = b[..., -1]
        m_new = m_t[..., -1]
        decay = jnp.exp(b_last + m - m_new)
        wk = jnp.exp(b_last[..., None] - b + igc - m_new[..., None])
        C_new = decay[..., None, None] * C + jnp.einsum('bhs,bhsv,bhsd->bhvd', wk, vc, kc)
        n_new = decay[..., None] * n + jnp.einsum('bhs,bhsd->bhd', wk, kc)
        return (C_new, n_new, m_new), h

    (C, n, m), hs = lax.scan(step, (C0, n0, m0),
                             (chunk4(q), chunk4(k), chunk4(v), chunk3(ig), chunk3(lf)))
    h = hs.transpose(1, 0, 3, 2, 4).reshape(B, S, H, dv)
    return h, C, n, m


def mixer(xn, pos, kv_l, C0, n0, m0, w_in, b_if, ml_g, w_pa, w_pm, w_out):
    B, S, _ = xn.shape
    f32 = jnp.float32
    u = xn @ w_in
    offs = []
    acc_off = 0
    for sz in IN_SPLITS[:-1]:
        acc_off += sz
        offs.append(acc_off)
    aq, ak, av, mq, mk, mv, mo, mi, mf, ga, gb = jnp.split(u, offs, axis=-1)

    nh = N_ATT_GROUPS * ATT_HEADS
    aq = rope_partial(aq.reshape(B, S, nh, ATT_HEAD_DIM).astype(f32), pos)
    ak = rope_partial(ak.reshape(B, S, nh, ATT_HEAD_DIM).astype(f32), pos)
    av = av.reshape(B, S, nh, ATT_HEAD_DIM).astype(f32)
    ms, ls, accs, new_kv = [], [], [], []
    for g, (win, dil) in enumerate(ATT_GROUPS):
        sl = slice(g * ATT_HEADS, (g + 1) * ATT_HEADS)
        qg, kg, vg = aq[:, :, sl], ak[:, :, sl], av[:, :, sl]
        if kv_l is None:
            m, l, acc = window_attn_prompt(qg, kg, vg, win, dil)
            keep = min(win, S)
            new_kv.append(jnp.stack([kg[:, S - keep:], vg[:, S - keep:]], axis=2))
        else:
            buf = kv_l[g].astype(f32)
            m, l, acc, k_all, v_all = window_attn_sample(qg, kg, vg, buf[:, :, 0], buf[:, :, 1], win, dil)
            keep = min(win, k_all.shape[1])
            new_kv.append(jnp.stack([k_all[:, -keep:], v_all[:, -keep:]], axis=2))
        ms.append(m)
        ls.append(l)
        accs.append(acc)
    m_all = jnp.stack(ms)
    l_all = jnp.stack(ls)
    acc_all = jnp.stack(accs)
    wgt = jnp.exp(m_all - jnp.max(m_all, axis=0, keepdims=True))
    att = jnp.sum(wgt[..., None] * acc_all, axis=0) / jnp.sum(wgt * l_all, axis=0)[..., None]
    att = att.reshape(B, S, ATT_WIDTH).astype(xn.dtype)

    q = mq.reshape(B, S, ML_HEADS, ML_QK_DIM).astype(f32)
    k = mk.reshape(B, S, ML_HEADS, ML_QK_DIM).astype(f32) * (ML_QK_DIM ** -0.5)
    v = mv.reshape(B, S, ML_HEADS, ML_V_DIM).astype(f32)
    bif = b_if.astype(f32)
    ig = mi.astype(f32) + bif[:ML_HEADS]
    lf = jax.nn.log_sigmoid(mf.astype(f32) + bif[ML_HEADS:])
    h, C, n, m = mlstm(q, k, v, ig, lf, C0.astype(f32), n0.astype(f32), m0.astype(f32))
    h = h * lax.rsqrt(jnp.mean(h * h, axis=-1, keepdims=True) + EPS) * ml_g.astype(f32)
    h = h.reshape(B, S, ML_V_WIDTH).astype(xn.dtype) * jax.nn.sigmoid(mo)

    merged = jax.nn.sigmoid(ga) * (att @ w_pa) + jax.nn.sigmoid(gb) * (h @ w_pm)
    return merged @ w_out, new_kv, (C, n, m)


def hier_moe(x, w_rg, b_rg, w_re, b_re, w_gate, w_up, w_down):
    shp = x.shape
    xt = x.reshape(-1, shp[-1])
    T = xt.shape[0]
    gp = jax.nn.softmax((xt @ w_rg + b_rg).astype(jnp.float32), axis=-1)
    gv, gi = lax.top_k(gp, 1)
    g_idx = gi[:, 0]
    el = (xt @ w_re + b_re).astype(jnp.float32).reshape(T, N_GROUPS, EXP_PER_GROUP)
    el_sel = el[jnp.arange(T), g_idx]
    tv, ti = lax.top_k(el_sel, TOP_K)
    w2 = jax.nn.softmax(tv, axis=-1) * gv
    eid = g_idx[:, None] * EXP_PER_GROUP + ti
    gate = jnp.sum(jax.nn.one_hot(eid, N_EXPERTS, dtype=jnp.float32) * w2[..., None], axis=1)
    hdn = jax.nn.silu(xt @ w_gate) * (xt @ w_up)
    hdn = (hdn.reshape(T, N_EXPERTS, EXPERT_FF) * gate[:, :, None].astype(hdn.dtype)).reshape(T, N_EXPERTS * EXPERT_FF)
    return (hdn @ w_down).reshape(shp)


def run_trunk(x, pos, kv_caches, C0s, n0s, m0s, attn_norm_g, w_in, b_if, ml_norm_g, w_pa, w_pm, w_out,
              ffn_norm_g, w_rg, b_rg, w_re, b_re, w_gate, w_up, w_down, final_norm_g):
    new_kv = [[] for _ in ATT_GROUPS]
    Cs, ns, ms = [], [], []
    for li in range(DEPTH):
        kv_l = None if kv_caches is None else [c[li] for c in kv_caches]
        mix, kvs, (C, n, m) = mixer(rmsnorm(x, attn_norm_g[li]), pos, kv_l, C0s[li], n0s[li], m0s[li],
                                    w_in[li], b_if[li], ml_norm_g[li], w_pa[li], w_pm[li], w_out[li])
        x = x + mix
        x = x + hier_moe(rmsnorm(x, ffn_norm_g[li]), w_rg[li], b_rg[li], w_re[li], b_re[li],
                         w_gate[li], w_up[li], w_down[li])
        for g in range(N_ATT_GROUPS):
            new_kv[g].append(kvs[g])
        Cs.append(C)
        ns.append(n)
        ms.append(m)
    y = rmsnorm(x, final_norm_g)
    return y, [jnp.stack(a) for a in new_kv], jnp.stack(Cs), jnp.stack(ns), jnp.stack(ms)


def setup_inputs(seed: int = 0) -> dict:
    key = jax.random.key(seed)
    ks = jax.random.split(key, 28)

    def nrm(k, shape, scale):
        return jax.random.normal(k, shape, jnp.float32) * scale

    kv_shape = lambda w: (DEPTH, DEC_BATCH, min(w, PAST_LEN), 2, ATT_HEADS, ATT_HEAD_DIM)
    b_i = nrm(ks[11], (DEPTH, ML_HEADS), 0.1)
    b_f = jnp.linspace(3.0, 6.0, ML_HEADS, dtype=jnp.float32)[None, :] + nrm(ks[12], (DEPTH, ML_HEADS), 0.1)
    return {
        "x_prompt": nrm(ks[0], (BATCH, SEQ, D_MODEL), 1.0),
        "x_sample": nrm(ks[1], (DEC_BATCH, DEC_SEQ, D_MODEL), 1.0),
        "cache_kv_w128": nrm(ks[2], kv_shape(ATT_GROUPS[0][0]), 1.0),
        "cache_kv_w512": nrm(ks[3], kv_shape(ATT_GROUPS[1][0]), 1.0),
        "cache_kv_w2048": nrm(ks[4], kv_shape(ATT_GROUPS[2][0]), 1.0),
        "state_C": nrm(ks[5], (DEPTH, DEC_BATCH, ML_HEADS, ML_V_DIM, ML_QK_DIM), 0.3),
        "state_n": nrm(ks[6], (DEPTH, DEC_BATCH, ML_HEADS, ML_QK_DIM), 0.3),
        "state_m": nrm(ks[7], (DEPTH, DEC_BATCH, ML_HEADS), 1.0),
        "attn_norm_g": 1.0 + nrm(ks[8], (DEPTH, D_MODEL), 0.05),
        "w_in": nrm(ks[9], (DEPTH, D_MODEL, N_IN), D_MODEL ** -0.5),
        "b_if": jnp.concatenate([b_i, b_f], axis=-1),
        "ml_norm_g": 1.0 + nrm(ks[10], (DEPTH, ML_HEADS, ML_V_DIM), 0.05),
        "w_pa": nrm(ks[13], (DEPTH, ATT_WIDTH, D_MODEL), ATT_WIDTH ** -0.5),
        "w_pm": nrm(ks[14], (DEPTH, ML_V_WIDTH, D_MODEL), ML_V_WIDTH ** -0.5),
        "w_out": nrm(ks[15], (DEPTH, D_MODEL, D_MODEL), 0.5 * D_MODEL ** -0.5),
        "ffn_norm_g": 1.0 + nrm(ks[16], (DEPTH, D_MODEL), 0.05),
        "w_rg": nrm(ks[17], (DEPTH, D_MODEL, N_GROUPS), D_MODEL ** -0.5),
        "b_rg": nrm(ks[18], (DEPTH, N_GROUPS), 0.01),
        "w_re": nrm(ks[19], (DEPTH, D_MODEL, N_EXPERTS), D_MODEL ** -0.5),
        "b_re": nrm(ks[20], (DEPTH, N_EXPERTS), 0.01),
        "w_gate": nrm(ks[21], (DEPTH, D_MODEL, N_EXPERTS * EXPERT_FF), D_MODEL ** -0.5),
        "w_up": nrm(ks[22], (DEPTH, D_MODEL, N_EXPERTS * EXPERT_FF), D_MODEL ** -0.5),
        "w_down": nrm(ks[23], (DEPTH, N_EXPERTS * EXPERT_FF, D_MODEL), EXPERT_FF ** -0.5),
        "final_norm_g": 1.0 + nrm(ks[24], (D_MODEL,), 0.05),
    }


def reference(x_prompt, x_sample, cache_kv_w128, cache_kv_w512, cache_kv_w2048, state_C, state_n, state_m,
              attn_norm_g, w_in, b_if, ml_norm_g, w_pa, w_pm, w_out, ffn_norm_g, w_rg, b_rg, w_re, b_re,
              w_gate, w_up, w_down, final_norm_g):
    f32 = jnp.float32
    bp, sp = x_prompt.shape[0], x_prompt.shape[1]
    C0 = jnp.zeros((DEPTH, bp, ML_HEADS, ML_V_DIM, ML_QK_DIM), f32)
    n0 = jnp.zeros((DEPTH, bp, ML_HEADS, ML_QK_DIM), f32)
    m0 = jnp.zeros((DEPTH, bp, ML_HEADS), f32)
    y_prompt, p_kv, p_C, p_n, p_m = run_trunk(
        x_prompt, jnp.arange(sp, dtype=jnp.int32), None, C0, n0, m0,
        attn_norm_g, w_in, b_if, ml_norm_g, w_pa, w_pm, w_out, ffn_norm_g, w_rg, b_rg, w_re, b_re,
        w_gate, w_up, w_down, final_norm_g)
    ts = x_sample.shape[1]
    y_sample, s_kv, s_C, s_n, s_m = run_trunk(
        x_sample, PAST_LEN + jnp.arange(ts, dtype=jnp.int32), [cache_kv_w128, cache_kv_w512, cache_kv_w2048],
        state_C, state_n, state_m,
        attn_norm_g, w_in, b_if, ml_norm_g, w_pa, w_pm, w_out, ffn_norm_g, w_rg, b_rg, w_re, b_re,
        w_gate, w_up, w_down, final_norm_g)
    return (y_prompt, y_sample, p_kv[0], p_kv[1], p_kv[2], p_C, p_n, p_m,
            s_kv[0], s_kv[1], s_kv[2], s_C, s_n, s_m)
```

```python
import functools

import jax
import jax.numpy as jnp
from jax import lax
from jax.experimental import pallas as pl
from jax.experimental.pallas import tpu as pltpu

F32 = jnp.float32
BF16 = jnp.bfloat16

D_MODEL = 2048
DEPTH = 2
PAST_LEN = 16384
ATT_GROUPS = ((128, 1), (512, 4), (2048, 16))
N_ATT_GROUPS = 3
ATT_HEADS = 4
HEAD_DIM = 128
ATT_WIDTH = ATT_HEADS * HEAD_DIM
ATT_SCALE = HEAD_DIM ** -0.5
ROPE_DIM = HEAD_DIM // 4
ROPE_THETA = 500000.0
ML_HEADS = 8
ML_QK = 128
ML_V = 256
ML_K_SCALE = ML_QK ** -0.5
N_GROUPS = 4
EXP_PER_GROUP = 8
N_EXPERTS = N_GROUPS * EXP_PER_GROUP
EXPERT_FF = 256
EPS = 1e-6

O_AQ, O_AK, O_AV = 0, 1536, 3072
O_MQ, O_MK, O_MV, O_MO = 4608, 5632, 6656, 8704
O_MI, O_MF, O_GA, O_GB = 10752, 10760, 10768, 12816
N_MAIN = O_MI
LANES = 128
VMEM_LIMIT_MB = 56


def _cparams(sem):
    return pltpu.CompilerParams(dimension_semantics=sem, vmem_limit_bytes=VMEM_LIMIT_MB * 1024 * 1024)


def _dot(a, b, dims, precise):
    dn = (dims, ((), ()))
    if precise:
        return lax.dot_general(a.astype(F32), b.astype(F32), dn, preferred_element_type=F32,
                               precision=lax.Precision.HIGHEST)
    return lax.dot_general(a.astype(BF16), b.astype(BF16), dn, preferred_element_type=F32)


def _mm(a, b, precise=False):
    return _dot(a, b, ((1,), (0,)), precise)


def _mm_nt(a, b, precise=False):
    return _dot(a, b, ((1,), (1,)), precise)


def _mm_tn(a, b, precise=False):
    return _dot(a, b, ((0,), (0,)), precise)


def _rb(x, precise=False):
    return x.astype(F32) if precise else x.astype(BF16).astype(F32)


def _sigmoid(z):
    return 1.0 / (1.0 + jnp.exp(-z))


def _log_sigmoid(z):
    return jnp.minimum(z, 0.0) - jnp.log1p(jnp.exp(-jnp.abs(z)))


def _norm_matmul_kernel(x_ref, g_ref, w_ref, b_ref, o_ref, xn_ref, *, act, precise):
    @pl.when(pl.program_id(1) == 0)
    def _():
        xf = x_ref[...]
        r = lax.rsqrt(jnp.mean(xf * xf, axis=-1, keepdims=True) + EPS)
        xn_ref[...] = ((xf * r) * g_ref[...]).astype(xn_ref.dtype)

    acc = _mm(xn_ref[...], w_ref[...], precise)
    if act == "sigmoid":
        acc = _sigmoid(acc)
    elif act == "gates":
        z = acc + b_ref[...]
        lane = lax.broadcasted_iota(jnp.int32, z.shape, 1)
        acc = jnp.where(lane < ML_HEADS, z, _log_sigmoid(z))
    o_ref[...] = acc.astype(o_ref.dtype)


def _norm_matmul(x, g, w, bias, *, li, col_blk0, n_out, tm, tn, out_dtype, act, precise, name):
    m, d = x.shape
    if w.ndim == 3:
        w_spec = pl.BlockSpec((None, d, tn), lambda i, j: (li, 0, j + col_blk0))
    else:
        w_spec = pl.BlockSpec((d, tn), lambda i, j: (0, j + col_blk0))
    return pl.pallas_call(
        functools.partial(_norm_matmul_kernel, act=act, precise=precise),
        grid=(m // tm, n_out // tn),
        in_specs=[
            pl.BlockSpec((tm, d), lambda i, j: (i, 0)),
            pl.BlockSpec((1, d), lambda i, j: (0, 0)),
            w_spec,
            pl.BlockSpec((1, tn), lambda i, j: (0, j)),
        ],
        out_specs=pl.BlockSpec((tm, tn), lambda i, j: (i, j)),
        out_shape=jax.ShapeDtypeStruct((m, n_out), out_dtype),
        scratch_shapes=[pltpu.VMEM((tm, d), F32 if precise else BF16)],
        compiler_params=_cparams(("parallel", "arbitrary")),
        name=name,
    )(x, g, w, bias)


def _merge_kernel(att_ref, hg_ref, sa_ref, sb_ref, wpa_ref, wpm_ref, o_ref, *, precise):
    a = _mm(att_ref[...], wpa_ref[...], precise)
    m = _mm(hg_ref[...], wpm_ref[...], precise)
    o_ref[...] = (sa_ref[...].astype(F32) * a + sb_ref[...].astype(F32) * m).astype(o_ref.dtype)


def _merge(att, hg, sg, w_pa, w_pm, *, li, tm, tn, out_dtype, precise, name):
    m = att.shape[0]
    nb = D_MODEL // tn
    return pl.pallas_call(
        functools.partial(_merge_kernel, precise=precise),
        grid=(m // tm, nb),
        in_specs=[
            pl.BlockSpec((tm, ATT_WIDTH), lambda i, j: (i, 0)),
            pl.BlockSpec((tm, ML_HEADS * ML_V), lambda i, j: (i, 0)),
            pl.BlockSpec((tm, tn), lambda i, j: (i, j)),
            pl.BlockSpec((tm, tn), lambda i, j: (i, j + nb)),
            pl.BlockSpec((None, ATT_WIDTH, tn), lambda i, j: (li, 0, j)),
            pl.BlockSpec((None, ML_HEADS * ML_V, tn), lambda i, j: (li, 0, j)),
        ],
        out_specs=pl.BlockSpec((tm, tn), lambda i, j: (i, j)),
        out_shape=jax.ShapeDtypeStruct((m, D_MODEL), out_dtype),
        compiler_params=_cparams(("parallel", "arbitrary")),
        name=name,
    )(att, hg, sg, sg, w_pa, w_pm)


def _outproj_kernel(x_ref, mg_ref, w_ref, o_ref, *, precise):
    o_ref[...] = x_ref[...] + _mm(mg_ref[...], w_ref[...], precise)


def _outproj(x, mg, w_out, *, li, tm, tn, precise, name):
    m = x.shape[0]
    return pl.pallas_call(
        functools.partial(_outproj_kernel, precise=precise),
        grid=(m // tm, D_MODEL // tn),
        in_specs=[
            pl.BlockSpec((tm, tn), lambda i, j: (i, j)),
            pl.BlockSpec((tm, D_MODEL), lambda i, j: (i, 0)),
            pl.BlockSpec((None, D_MODEL, tn), lambda i, j: (li, 0, j)),
        ],
        out_specs=pl.BlockSpec((tm, tn), lambda i, j: (i, j)),
        out_shape=jax.ShapeDtypeStruct((m, D_MODEL), F32),
        compiler_params=_cparams(("parallel", "arbitrary")),
        name=name,
    )(x, mg, w_out)


def _rope_tables(pos):
    half = ROPE_DIM // 2
    inv = jnp.power(jnp.float32(ROPE_THETA), -jnp.arange(half, dtype=F32) / half)
    ang = pos.astype(F32)[:, None] * inv[None, :]
    n = pos.shape[0]
    cos_t = jnp.concatenate([jnp.cos(ang), jnp.cos(ang), jnp.ones((n, HEAD_DIM - ROPE_DIM), F32)], axis=-1)
    sin_t = jnp.concatenate([jnp.sin(ang), jnp.sin(ang), jnp.zeros((n, HEAD_DIM - ROPE_DIM), F32)], axis=-1)
    return cos_t, sin_t


def _rope_perm():
    half = ROPE_DIM // 2
    r = jnp.arange(HEAD_DIM)[:, None]
    c = jnp.arange(HEAD_DIM)[None, :]
    p = jnp.where((c < half) & (r == c + half), -1.0, 0.0) + jnp.where(
        (c >= half) & (c < ROPE_DIM) & (r == c - half), 1.0, 0.0)
    return p.astype(BF16)


ROPE_ROWS = 512


def _attn_prompt_kernel(cos_ref, sin_ref, perm_ref, *refs, seq):
    q_refs, k_refs, v_refs = refs[0:3], refs[3:6], refs[6:9]
    att_ref = refs[9]
    kc_refs, vc_refs = refs[10:13], refs[13:16]
    qs, ks, vs, acc_s, m_s, l_s = refs[16:22]
    perm = perm_ref[...]
    nk = ATT_GROUPS[0][0] // ATT_GROUPS[0][1]
    row = lax.broadcasted_iota(jnp.int32, (nk, nk), 0)
    col = lax.broadcasted_iota(jnp.int32, (nk, nk), 1)
    cur_ok = col <= row
    prev_ok = col >= row
    neg = jnp.float32(-jnp.inf)

    for g, (win, dil) in enumerate(ATT_GROUPS):
        assert win // dil == nk
        span = nk * dil
        shift = dil.bit_length() - 1

        def rope_body(c, carry, g=g):
            r0 = pl.multiple_of(c * ROPE_ROWS, ROPE_ROWS)
            cs = cos_ref[pl.ds(r0, ROPE_ROWS), :]
            sn = sin_ref[pl.ds(r0, ROPE_ROWS), :]
            qb = q_refs[g][0, pl.ds(r0, ROPE_ROWS), :]
            kb = k_refs[g][0, pl.ds(r0, ROPE_ROWS), :]
            qr = qb.astype(F32) * cs + jnp.dot(qb, perm, preferred_element_type=F32) * sn
            kr = kb.astype(F32) * cs + jnp.dot(kb, perm, preferred_element_type=F32) * sn
            qs[pl.ds(r0, ROPE_ROWS), :] = qr
            ks[pl.ds(r0, ROPE_ROWS), :] = kr
            vs[pl.ds(r0, ROPE_ROWS), :] = v_refs[g][0, pl.ds(r0, ROPE_ROWS), :].astype(F32)
            return carry

        lax.fori_loop(0, seq // ROPE_ROWS, rope_body, 0)
        keep = min(win, seq)
        kc_refs[g][0] = ks[seq - keep:seq, :]
        vc_refs[g][0] = vs[seq - keep:seq, :]

        def rows(start, dil=dil):
            if dil == 1:
                return pl.ds(pl.multiple_of(start, nk), nk)
            return pl.ds(start, nk, stride=dil)

        def blk_body(blk, carry, g=g, dil=dil, span=span, shift=shift, rows=rows):
            n = blk >> shift
            r = blk & (dil - 1)
            start = n * span + r
            has_prev = n > 0
            start_p = jnp.where(has_prev, start - span, start)
            q = qs[rows(start), :]
            k_c = ks[rows(start), :]
            k_p = ks[rows(start_p), :]
            v_c = vs[rows(start), :]
            v_p = vs[rows(start_p), :]
            s_c = jnp.where(cur_ok, _mm_nt(q, k_c) * ATT_SCALE, neg)
            no_prev = jnp.where(has_prev, jnp.float32(0.0), neg)
            s_p = jnp.where(prev_ok, _mm_nt(q, k_p) * ATT_SCALE + no_prev, neg)
            m = jnp.maximum(jnp.max(s_c, axis=-1, keepdims=True), jnp.max(s_p, axis=-1, keepdims=True))
            p_c = jnp.exp(s_c - m)
            p_p = jnp.exp(s_p - m)
            l = jnp.sum(p_c, axis=-1, keepdims=True) + jnp.sum(p_p, axis=-1, keepdims=True)
            acc = _mm(p_c, v_c) + _mm(p_p, v_p)
            m_b = jnp.broadcast_to(m, (nk, HEAD_DIM))
            l_b = jnp.broadcast_to(l, (nk, HEAD_DIM))
            if g == 0:
                acc_s[rows(start), :] = acc
                m_s[rows(start), :] = m_b
                l_s[rows(start), :] = l_b
            else:
                m_old = m_s[rows(start), :]
                m_new = jnp.maximum(m_old, m_b)
                a_old = jnp.exp(m_old - m_new)
                a_new = jnp.exp(m_b - m_new)
                acc_s[rows(start), :] = acc_s[rows(start), :] * a_old + acc * a_new
                l_s[rows(start), :] = l_s[rows(start), :] * a_old + l_b * a_new
                m_s[rows(start), :] = m_new
            return carry

        lax.fori_loop(0, seq // nk, blk_body, 0)

    def out_body(c, carry):
        r0 = pl.multiple_of(c * ROPE_ROWS, ROPE_ROWS)
        att_ref[0, pl.ds(r0, ROPE_ROWS), :] = (
            acc_s[pl.ds(r0, ROPE_ROWS), :] / l_s[pl.ds(r0, ROPE_ROWS), :]).astype(att_ref.dtype)
        return carry

    lax.fori_loop(0, seq // ROPE_ROWS, out_body, 0)


def _attn_prompt(u3, cos_t, sin_t, *, name):
    bsz, seq, _ = u3.shape
    perm = _rope_perm()
    in_specs = [
        pl.BlockSpec((seq, HEAD_DIM), lambda b, h: (0, 0)),
        pl.BlockSpec((seq, HEAD_DIM), lambda b, h: (0, 0)),
        pl.BlockSpec((HEAD_DIM, HEAD_DIM), lambda b, h: (0, 0)),
    ]
    for off in (O_AQ, O_AK, O_AV):
        for g in range(N_ATT_GROUPS):
            blk0 = off // HEAD_DIM + g * ATT_HEADS
            in_specs.append(pl.BlockSpec((1, seq, HEAD_DIM), lambda b, h, blk0=blk0: (b, 0, blk0 + h)))
    keeps = [min(w, seq) for w, _ in ATT_GROUPS]
    out_specs = [pl.BlockSpec((1, seq, HEAD_DIM), lambda b, h: (b, 0, h))]
    out_shape = [jax.ShapeDtypeStruct((bsz, seq, ATT_WIDTH), BF16)]
    for _ in range(2):
        for keep in keeps:
            out_specs.append(pl.BlockSpec((1, keep, HEAD_DIM), lambda b, h: (b, 0, h)))
            out_shape.append(jax.ShapeDtypeStruct((bsz, keep, ATT_WIDTH), F32))
    outs = pl.pallas_call(
        functools.partial(_attn_prompt_kernel, seq=seq),
        grid=(bsz, ATT_HEADS),
        in_specs=in_specs,
        out_specs=out_specs,
        out_shape=out_shape,
        scratch_shapes=[pltpu.VMEM((seq, HEAD_DIM), F32) for _ in range(6)],
        compiler_params=_cparams(("parallel", "arbitrary")),
        name=name,
    )(cos_t, sin_t, perm, *([u3] * 9))
    return outs[0], outs[1:4], outs[4:7]


def _attn_sample_kernel(cos_ref, sin_ref, q_ref, k_ref, v_ref, c0_ref, c1_ref, c2_ref,
                        att_ref, kn_ref, *, n_new):
    cache_refs = (c0_ref, c1_ref, c2_ref)
    cs = cos_ref[...]
    sn = sin_ref[...]
    rows_pad = q_ref.shape[1]
    half = ROPE_DIM // 2
    lane = lax.broadcasted_iota(jnp.int32, (rows_pad, HEAD_DIM), 1)

    def rot_half(x):
        return jnp.where(lane < half, -pltpu.roll(x, HEAD_DIM - half, axis=1),
                         jnp.where(lane < ROPE_DIM, pltpu.roll(x, half, axis=1), 0.0))

    nk = ATT_GROUPS[0][0] // ATT_GROUPS[0][1]
    kv_rows = 2 * ATT_HEADS
    w_iota = lax.broadcasted_iota(jnp.int32, (nk, 1), 0)
    u_iota = lax.broadcasted_iota(jnp.int32, (rows_pad, 1), 0)
    neg = jnp.float32(-jnp.inf)
    att_ref[...] = jnp.zeros(att_ref.shape, att_ref.dtype)

    for h in range(ATT_HEADS):
        parts = [[None] * N_ATT_GROUPS for _ in range(n_new)]
        for g, (win, dil) in enumerate(ATT_GROUPS):
            hs = (g * ATT_HEADS + h) * HEAD_DIM
            qh = q_ref[0, :, hs:hs + HEAD_DIM]
            kh = k_ref[0, :, hs:hs + HEAD_DIM]
            vh = v_ref[0, :, hs:hs + HEAD_DIM]
            qr = qh * cs + rot_half(qh) * sn
            kr = kh * cs + rot_half(kh) * sn
            kn_ref[0, :, hs:hs + HEAD_DIM] = kr
            cref = cache_refs[g]
            for t in range(n_new):
                q_t = qr[t:t + 1, :]
                if dil == 1:
                    p0 = 0
                else:
                    p0 = t
                k_c = cref[pl.ds(p0 * kv_rows + h, nk, stride=kv_rows * dil), :]
                v_c = cref[pl.ds(p0 * kv_rows + ATT_HEADS + h, nk, stride=kv_rows * dil), :]
                s_c = jnp.sum(k_c * q_t, axis=-1, keepdims=True) * ATT_SCALE
                s_n = jnp.sum(kr * q_t, axis=-1, keepdims=True) * ATT_SCALE
                if dil == 1:
                    s_c = jnp.where(w_iota >= t, s_c, neg)
                    s_n = jnp.where(u_iota <= t, s_n, neg)
                else:
                    s_n = jnp.where(u_iota == t, s_n, neg)
                m = jnp.maximum(jnp.max(s_c, axis=0, keepdims=True), jnp.max(s_n, axis=0, keepdims=True))
                p_c = jnp.exp(s_c - m)
                p_n = jnp.exp(s_n - m)
                l = jnp.sum(p_c, axis=0, keepdims=True) + jnp.sum(p_n, axis=0, keepdims=True)
                acc = jnp.sum(p_c * v_c, axis=0, keepdims=True) + jnp.sum(p_n * vh, axis=0, keepdims=True)
                parts[t][g] = (m, l, acc)
        for t in range(n_new):
            m_all = functools.reduce(jnp.maximum, [p[0] for p in parts[t]])
            num = sum(jnp.exp(p[0] - m_all) * p[2] for p in parts[t])
            den = sum(jnp.exp(p[0] - m_all) * p[1] for p in parts[t])
            att_ref[0, t:t + 1, h * HEAD_DIM:(h + 1) * HEAD_DIM] = num / den


def _attn_sample(u3, caches2d, cos_t, sin_t, *, li, n_new, name):
    bsz, rows_pad, _ = u3.shape
    qkv_w = N_ATT_GROUPS * ATT_WIDTH
    in_specs = [
        pl.BlockSpec((rows_pad, HEAD_DIM), lambda b: (0, 0)),
        pl.BlockSpec((rows_pad, HEAD_DIM), lambda b: (0, 0)),
        pl.BlockSpec((1, rows_pad, qkv_w), lambda b: (b, 0, O_AQ // qkv_w)),
        pl.BlockSpec((1, rows_pad, qkv_w), lambda b: (b, 0, O_AK // qkv_w)),
        pl.BlockSpec((1, rows_pad, qkv_w), lambda b: (b, 0, O_AV // qkv_w)),
    ]
    for (win, _), c in zip(ATT_GROUPS, caches2d):
        rows = c.shape[0] // (DEPTH * bsz)
        in_specs.append(pl.BlockSpec((rows, HEAD_DIM), lambda b, bsz=bsz: (li * bsz + b, 0)))
    return pl.pallas_call(
        functools.partial(_attn_sample_kernel, n_new=n_new),
        grid=(bsz,),
        in_specs=in_specs,
        out_specs=[
            pl.BlockSpec((1, rows_pad, ATT_WIDTH), lambda b: (b, 0, 0)),
            pl.BlockSpec((1, rows_pad, qkv_w), lambda b: (b, 0, 0)),
        ],
        out_shape=[
            jax.ShapeDtypeStruct((bsz, rows_pad, ATT_WIDTH), F32),
            jax.ShapeDtypeStruct((bsz, rows_pad, qkv_w), F32),
        ],
        compiler_params=_cparams(("parallel",)),
        name=name,
    )(cos_t, sin_t, u3, u3, u3, *caches2d)


ML_HPB = 2


def _mlstm_kernel(q_ref, k_ref, v_ref, mo_ref, gc_ref, gr_ref, mlg_ref, c0_ref, n0_ref, m0_ref,
                  h_ref, c_out, n_out, m_out, ct_s, n_s, m_s, *, chunk, precise):
    s_idx = pl.program_id(2)
    n_s_blocks = pl.num_programs(2)
    sb = q_ref.shape[1]
    L = chunk

    @pl.when(s_idx == 0)
    def _():
        for hh in range(ML_HPB):
            ct_s[hh] = c0_ref[0, hh].T
            n_s[hh] = n0_ref[0, 0, hh:hh + 1, :]
            m_s[hh] = m0_ref[0, 0, hh:hh + 1, :]

    row = lax.broadcasted_iota(jnp.int32, (L, L), 0)
    col = lax.broadcasted_iota(jnp.int32, (L, L), 1)
    causal = row >= col
    neg = jnp.float32(-jnp.inf)

    def chunk_body(c, carry):
        r0 = pl.multiple_of(c * L, L)
        gc = gc_ref[0, 0, pl.ds(r0, L), :]
        gr = gr_ref[0, 0, :, pl.ds(r0, L)]
        for hh in range(ML_HPB):
            ig_col = gc[:, hh:hh + 1]
            lf_col = gc[:, ML_HPB + hh:ML_HPB + hh + 1]
            ig_row = gr[hh:hh + 1, :]
            lf_row = gr[ML_HPB + hh:ML_HPB + hh + 1, :]
            b_col = jnp.sum(jnp.where(causal, lf_row, 0.0), axis=1, keepdims=True)
            b_row = jnp.sum(jnp.where(row <= col, lf_col, 0.0), axis=0, keepdims=True)
            qf = q_ref[0, pl.ds(r0, L), hh * ML_QK:(hh + 1) * ML_QK]
            kf = k_ref[0, pl.ds(r0, L), hh * ML_QK:(hh + 1) * ML_QK]
            vf = v_ref[0, pl.ds(r0, L), hh * ML_V:(hh + 1) * ML_V]
            mo = mo_ref[0, pl.ds(r0, L), hh * ML_V:(hh + 1) * ML_V].astype(F32)
            ct = ct_s[hh]
            n_row = n_s[hh]
            m_prev = m_s[hh][:, 0:1]
            inter = b_col + m_prev
            dm = jnp.where(causal, b_col - b_row + ig_row, neg)
            m_t = jnp.maximum(inter, jnp.max(dm, axis=1, keepdims=True))
            w_intra = jnp.exp(dm - m_t)
            w_inter = jnp.exp(inter - m_t)
            ks = kf.astype(F32) * ML_K_SCALE
            sw = _mm_nt(qf, ks, precise) * w_intra
            num = w_inter * _mm(qf, ct, precise) + _mm(sw, vf, precise)
            den = w_inter * jnp.sum(_rb(qf, precise) * _rb(n_row, precise), axis=1, keepdims=True) + jnp.sum(
                sw, axis=1, keepdims=True)
            hv = num / jnp.maximum(jnp.abs(den), jnp.exp(-m_t))
            hn = hv * lax.rsqrt(jnp.mean(hv * hv, axis=-1, keepdims=True) + EPS)
            hn = hn * mlg_ref[:, hh * ML_V:(hh + 1) * ML_V]
            h_ref[0, pl.ds(r0, L), hh * ML_V:(hh + 1) * ML_V] = (hn * _sigmoid(mo)).astype(h_ref.dtype)
            b_last = b_col[L - 1:L, :]
            m_new = m_t[L - 1:L, :]
            decay = jnp.exp(b_last + m_prev - m_new)
            wk_col = jnp.exp(b_last - b_col + ig_col - m_new)
            ct_s[hh] = decay * ct + _mm_tn(ks * wk_col, vf, precise)
            n_s[hh] = decay * n_row + jnp.sum(_rb(ks, precise) * _rb(wk_col, precise), axis=0, keepdims=True)
            m_s[hh] = jnp.broadcast_to(m_new, (1, LANES))
        return carry

    lax.fori_loop(0, sb // L, chunk_body, 0)

    @pl.when(s_idx == n_s_blocks - 1)
    def _():
        for hh in range(ML_HPB):
            c_out[0, hh] = ct_s[hh].T
            n_out[0, 0, hh:hh + 1, :] = n_s[hh]
            m_out[0, 0, hh:hh + 1, :] = m_s[hh]


def _mlstm(u3, gc, gr, mlg, c0, n0, m0, *, chunk, sb, out_dtype, precise, name):
    bsz, seq, _ = u3.shape
    hg_n = ML_HEADS // ML_HPB
    qk_w, v_w = ML_HPB * ML_QK, ML_HPB * ML_V
    n0r = n0.reshape(bsz, hg_n, ML_HPB, ML_QK)
    m0r = jnp.broadcast_to(m0.reshape(bsz, hg_n, ML_HPB, 1), (bsz, hg_n, ML_HPB, LANES))
    state_spec = pl.BlockSpec((1, 1, ML_HPB, LANES), lambda b, hg, s: (b, hg, 0, 0))
    c_spec = pl.BlockSpec((1, ML_HPB, ML_V, ML_QK), lambda b, hg, s: (b, hg, 0, 0))
    h, c_new, n_new, m_new = pl.pallas_call(
        functools.partial(_mlstm_kernel, chunk=chunk, precise=precise),
        grid=(bsz, hg_n, seq // sb),
        in_specs=[
            pl.BlockSpec((1, sb, qk_w), lambda b, hg, s: (b, s, O_MQ // qk_w + hg)),
            pl.BlockSpec((1, sb, qk_w), lambda b, hg, s: (b, s, O_MK // qk_w + hg)),
            pl.BlockSpec((1, sb, v_w), lambda b, hg, s: (b, s, O_MV // v_w + hg)),
            pl.BlockSpec((1, sb, v_w), lambda b, hg, s: (b, s, O_MO // v_w + hg)),
            pl.BlockSpec((1, 1, sb, 2 * ML_HPB), lambda b, hg, s: (b, hg, s, 0)),
            pl.BlockSpec((1, 1, 2 * ML_HPB, sb), lambda b, hg, s: (b, hg, 0, s)),
            pl.BlockSpec((1, v_w), lambda b, hg, s: (0, hg)),
            c_spec, state_spec, state_spec,
        ],
        out_specs=[
            pl.BlockSpec((1, sb, v_w), lambda b, hg, s: (b, s, hg)),
            c_spec, state_spec, state_spec,
        ],
        out_shape=[
            jax.ShapeDtypeStruct((bsz, seq, ML_HEADS * ML_V), out_dtype),
            jax.ShapeDtypeStruct((bsz, ML_HEADS, ML_V, ML_QK), F32),
            jax.ShapeDtypeStruct((bsz, hg_n, ML_HPB, LANES), F32),
            jax.ShapeDtypeStruct((bsz, hg_n, ML_HPB, LANES), F32),
        ],
        scratch_shapes=[
            pltpu.VMEM((ML_HPB, ML_QK, ML_V), F32),
            pltpu.VMEM((ML_HPB, 1, ML_QK), F32),
            pltpu.VMEM((ML_HPB, 1, LANES), F32),
        ],
        compiler_params=_cparams(("parallel", "parallel", "arbitrary")),
        name=name,
    )(u3, u3, u3, u3, gc, gr, mlg, c0, n0r, m0r)
    return h, c_new, n_new.reshape(bsz, ML_HEADS, ML_QK), m_new[..., 0].reshape(bsz, ML_HEADS)


def _gate_layouts(gates, bsz, seq, seq_pad):
    hg_n = ML_HEADS // ML_HPB
    ig = gates[:, :ML_HEADS].reshape(bsz, seq, ML_HEADS)
    lf = gates[:, ML_HEADS:2 * ML_HEADS].reshape(bsz, seq, ML_HEADS)
    if seq_pad > seq:
        ig = jnp.pad(ig, ((0, 0), (0, seq_pad - seq), (0, 0)), constant_values=-jnp.inf)
        lf = jnp.pad(lf, ((0, 0), (0, seq_pad - seq), (0, 0)))
    ig = ig.reshape(bsz, seq_pad, hg_n, ML_HPB).transpose(0, 2, 1, 3)
    lf = lf.reshape(bsz, seq_pad, hg_n, ML_HPB).transpose(0, 2, 1, 3)
    gc = jnp.concatenate([ig, lf], axis=-1)
    return gc, gc.transpose(0, 1, 3, 2)


def _router_kernel(x_ref, g_ref, w_ref, b_ref, xn_ref, route_ref, *, precise):
    xf = x_ref[...]
    r = lax.rsqrt(jnp.mean(xf * xf, axis=-1, keepdims=True) + EPS)
    xn = (xf * r) * g_ref[...]
    xn_ref[...] = xn.astype(xn_ref.dtype)
    logits = _mm(xn, w_ref[...], precise) + b_ref[...]
    lane = lax.broadcasted_iota(jnp.int32, logits.shape, 1).astype(F32)
    neg = jnp.float32(-jnp.inf)
    big = jnp.float32(LANES)
    gl = jnp.where(lane < N_GROUPS, logits, neg)
    g_max = jnp.max(gl, axis=-1, keepdims=True)
    g_val = 1.0 / jnp.sum(jnp.exp(gl - g_max), axis=-1, keepdims=True)
    g_idx = jnp.min(jnp.where(gl == g_max, lane, big), axis=-1, keepdims=True)
    lo = N_GROUPS + EXP_PER_GROUP * g_idx
    es = jnp.where((lane >= lo) & (lane < lo + EXP_PER_GROUP), logits, neg)
    t0 = jnp.max(es, axis=-1, keepdims=True)
    i0 = jnp.min(jnp.where(es == t0, lane, big), axis=-1, keepdims=True)
    es1 = jnp.where(lane == i0, neg, es)
    t1 = jnp.max(es1, axis=-1, keepdims=True)
    i1 = jnp.min(jnp.where(es1 == t1, lane, big), axis=-1, keepdims=True)
    e1 = jnp.exp(t1 - t0)
    w0 = g_val / (1.0 + e1)
    w1 = g_val * e1 / (1.0 + e1)
    route = jnp.where(lane == 0, i0 - N_GROUPS,
                      jnp.where(lane == 1, i1 - N_GROUPS,
                                jnp.where(lane == 2, w0, jnp.where(lane == 3, w1, 0.0))))
    route_ref[...] = route


def _router(x, g, w_r, b_r, *, tm, xn_dtype, precise, name):
    m, d = x.shape
    return pl.pallas_call(
        functools.partial(_router_kernel, precise=precise),
        grid=(m // tm,),
        in_specs=[
            pl.BlockSpec((tm, d), lambda i: (i, 0)),
            pl.BlockSpec((1, d), lambda i: (0, 0)),
            pl.BlockSpec((d, LANES), lambda i: (0, 0)),
            pl.BlockSpec((1, LANES), lambda i: (0, 0)),
        ],
        out_specs=[pl.BlockSpec((tm, d), lambda i: (i, 0)), pl.BlockSpec((tm, LANES), lambda i: (i, 0))],
        out_shape=[jax.ShapeDtypeStruct((m, d), xn_dtype), jax.ShapeDtypeStruct((m, LANES), F32)],
        compiler_params=_cparams(("parallel",)),
        name=name,
    )(x, g, w_r, b_r)


def _ffn_kernel(te_ref, nv_ref, xs_ref, wr_ref, wg_ref, wu_ref, wd_ref, y_ref, wg_s, wu_s, wd_s, *, precise):
    i = pl.program_id(0)
    valid = i < nv_ref[0]
    prev = te_ref[jnp.maximum(i - 1, 0)]

    @pl.when(valid & ((i == 0) | (te_ref[i] != prev)))
    def _():
        wg_s[...] = wg_ref[...].astype(wg_s.dtype)
        wu_s[...] = wu_ref[...].astype(wu_s.dtype)
        wd_s[...] = wd_ref[...].astype(wd_s.dtype)

    @pl.when(valid)
    def _():
        xs = xs_ref[...]
        gt = _mm(xs, wg_s[...], precise)
        up = _mm(xs, wu_s[...], precise)
        hdn = (gt * _sigmoid(gt) * up) * wr_ref[...]
        y_ref[...] = _mm(hdn, wd_s[...], precise).astype(y_ref.dtype)


def _ffn(xs, wrow, te, nv, w_gate, w_up, w_down, *, li, tm, out_dtype, precise, name):
    rows, d = xs.shape
    n_tiles = rows // tm
    f = EXPERT_FF
    wdt = F32 if precise else BF16

    def row_blk(i, te, nv):
        return (jnp.minimum(i, nv[0] - 1), 0)

    return pl.pallas_call(
        functools.partial(_ffn_kernel, precise=precise),
        grid_spec=pltpu.PrefetchScalarGridSpec(
            num_scalar_prefetch=2,
            grid=(n_tiles,),
            in_specs=[
                pl.BlockSpec((tm, d), row_blk),
                pl.BlockSpec((tm, 1), row_blk),
                pl.BlockSpec((None, d, f), lambda i, te, nv: (li, 0, te[i])),
                pl.BlockSpec((None, d, f), lambda i, te, nv: (li, 0, te[i])),
                pl.BlockSpec((None, f, d), lambda i, te, nv: (li, te[i], 0)),
            ],
            out_specs=pl.BlockSpec((tm, d), row_blk),
            scratch_shapes=[pltpu.VMEM((d, f), wdt), pltpu.VMEM((d, f), wdt), pltpu.VMEM((f, d), wdt)],
        ),
        out_shape=jax.ShapeDtypeStruct((rows, d), out_dtype),
        compiler_params=_cparams(("arbitrary",)),
        name=name,
    )(te, nv, xs, wrow, w_gate, w_up, w_down)


def _dispatch_plan(eid, tm):
    n_pairs = eid.shape[0] * 2
    flat_e = eid.reshape(-1)
    onehot = (flat_e[:, None] == jnp.arange(N_EXPERTS, dtype=jnp.int32)[None, :]).astype(jnp.int32)
    csum = jnp.cumsum(onehot, axis=0)
    rank = jnp.take_along_axis(csum, flat_e[:, None], axis=1)[:, 0] - 1
    counts = csum[-1]
    padded = ((counts + tm - 1) // tm) * tm
    ends = jnp.cumsum(padded)
    dest = (ends - padded)[flat_e] + rank
    rows = n_pairs + N_EXPERTS * tm
    n_tiles = rows // tm
    src = jnp.zeros((rows,), jnp.int32).at[dest].set(jnp.arange(n_pairs, dtype=jnp.int32) // 2)
    tile_start = jnp.arange(n_tiles, dtype=jnp.int32) * tm
    te = jnp.minimum(jnp.searchsorted(ends, tile_start, side="right"), N_EXPERTS - 1).astype(jnp.int32)
    nv = (ends[-1] // tm).astype(jnp.int32).reshape(1)
    return src, dest, te, nv


def _moe(x, g, w_r, b_r, w_gate, w_up, w_down, *, li, tm_route, tm_ffn, act_dtype, precise, name):
    xn, route = _router(x, g, w_r, b_r, tm=tm_route, xn_dtype=act_dtype, precise=precise,
                        name=name + "_router")
    eid = route[:, 0:2].astype(jnp.int32)
    src, dest, te, nv = _dispatch_plan(eid, tm_ffn)
    xs = jnp.take(xn, src, axis=0)
    wrow = jnp.zeros((src.shape[0], 1), F32).at[dest, 0].set(route[:, 2:4].reshape(-1))
    y = _ffn(xs, wrow, te, nv, w_gate, w_up, w_down, li=li, tm=tm_ffn, out_dtype=act_dtype,
             precise=precise, name=name + "_ffn")
    picked = jnp.take(y, dest, axis=0).astype(F32).reshape(x.shape[0], 2, x.shape[1])
    return x + (picked[:, 0] + picked[:, 1])


def _rmsnorm_kernel(x_ref, g_ref, o_ref):
    xf = x_ref[...]
    r = lax.rsqrt(jnp.mean(xf * xf, axis=-1, keepdims=True) + EPS)
    o_ref[...] = (xf * r) * g_ref[...]


def _rmsnorm(x, g, *, tm, name):
    m, d = x.shape
    return pl.pallas_call(
        _rmsnorm_kernel,
        grid=(m // tm,),
        in_specs=[pl.BlockSpec((tm, d), lambda i: (i, 0)), pl.BlockSpec((1, d), lambda i: (0, 0))],
        out_specs=pl.BlockSpec((tm, d), lambda i: (i, 0)),
        out_shape=jax.ShapeDtypeStruct((m, d), F32),
        compiler_params=_cparams(("parallel",)),
        name=name,
    )(x, g)


def _layer_weights(p, li):
    w_in_l = p["w_in"][li]
    w_sg = w_in_l[:, O_GA:]
    w_if = jnp.pad(w_in_l[:, O_MI:O_GA], ((0, 0), (0, LANES - 2 * ML_HEADS)))
    b_if = jnp.pad(p["b_if"][li], (0, LANES - 2 * ML_HEADS)).reshape(1, LANES)
    w_r = jnp.pad(jnp.concatenate([p["w_rg"][li], p["w_re"][li]], axis=1),
                  ((0, 0), (0, LANES - N_GROUPS - N_EXPERTS)))
    b_r = jnp.pad(jnp.concatenate([p["b_rg"][li], p["b_re"][li]]), (0, LANES - N_GROUPS - N_EXPERTS)).reshape(1, LANES)
    return w_sg, w_if, b_if, w_r, b_r


def _run_trunk(x3, p, layer_w, *, caches, c0s, n0s, m0s, pos0):
    bsz, seq, d = x3.shape
    m = bsz * seq
    x = x3.reshape(m, d)
    precise = caches is not None
    if caches is None:
        act_dtype, tag = BF16, "p"
        tm, tn, tm_route, tm_ffn, seq_pad = 1024, 512, 512, 256, seq
    else:
        act_dtype, tag = F32, "s"
        tm, tn, tm_route, tm_ffn, seq_pad = m, 512, m, 16, 16
    cos_t, sin_t = _rope_tables(pos0 + jnp.arange(seq_pad, dtype=jnp.int32))
    zeros_bias = jnp.zeros((1, N_MAIN), F32)
    new_kv = [[] for _ in ATT_GROUPS]
    c_all, n_all, m_all = [], [], []
    for li in range(DEPTH):
        w_sg, w_if, b_if, w_r, b_r = layer_w[li]
        g_attn = p["attn_norm_g"][li].reshape(1, d)
        common = dict(li=li, tm=tm, precise=precise)
        u = _norm_matmul(x, g_attn, p["w_in"], zeros_bias, col_blk0=0, n_out=N_MAIN, tn=tn,
                         out_dtype=act_dtype, act="none", name=f"{tag}{li}_inproj", **common)
        sg = _norm_matmul(x, g_attn, w_sg, zeros_bias, col_blk0=0, n_out=2 * D_MODEL, tn=tn,
                          out_dtype=act_dtype, act="sigmoid", name=f"{tag}{li}_gateproj", **common)
        gates = _norm_matmul(x, g_attn, w_if, b_if, col_blk0=0, n_out=LANES, tn=LANES,
                             out_dtype=F32, act="gates", name=f"{tag}{li}_ifproj", **common)
        u3 = u.reshape(bsz, seq, N_MAIN)
        if seq_pad > seq:
            u3 = jnp.pad(u3, ((0, 0), (0, seq_pad - seq), (0, 0)))
        if caches is None:
            att3, kcs, vcs = _attn_prompt(u3, cos_t, sin_t, name=f"{tag}{li}_attn")
            for gi in range(N_ATT_GROUPS):
                keep = kcs[gi].shape[1]
                new_kv[gi].append(jnp.stack([kcs[gi].reshape(bsz, keep, ATT_HEADS, HEAD_DIM),
                                             vcs[gi].reshape(bsz, keep, ATT_HEADS, HEAD_DIM)], axis=2))
            att = att3.reshape(m, ATT_WIDTH)
        else:
            caches2d = [c.reshape(-1, HEAD_DIM) for c in caches]
            att3, kn3 = _attn_sample(u3, caches2d, cos_t, sin_t, li=li, n_new=seq, name=f"{tag}{li}_attn")
            att = att3[:, :seq].reshape(m, ATT_WIDTH)
            nh = N_ATT_GROUPS * ATT_HEADS
            k_new = kn3[:, :seq].reshape(bsz, seq, nh, HEAD_DIM)
            v_new = u3[:, :seq, O_AV:O_AV + nh * HEAD_DIM].reshape(bsz, seq, nh, HEAD_DIM)
            for gi in range(N_ATT_GROUPS):
                sl = slice(gi * ATT_HEADS, (gi + 1) * ATT_HEADS)
                fresh = jnp.stack([k_new[:, :, sl], v_new[:, :, sl]], axis=2)
                new_kv[gi].append(jnp.concatenate([caches[gi][li][:, seq:], fresh], axis=1))
        gc, gr = _gate_layouts(gates, bsz, seq, seq_pad)
        mlg = p["ml_norm_g"][li].reshape(1, ML_HEADS * ML_V)
        hg3, c_new, n_new, m_new = _mlstm(
            u3, gc, gr, mlg, c0s[li], n0s[li], m0s[li], chunk=min(128, seq_pad), sb=min(512, seq_pad),
            out_dtype=act_dtype, precise=precise, name=f"{tag}{li}_mlstm")
        hg = hg3[:, :seq].reshape(m, ML_HEADS * ML_V)
        c_all.append(c_new)
        n_all.append(n_new)
        m_all.append(m_new)
        merged = _merge(att, hg, sg, p["w_pa"], p["w_pm"], li=li, tm=tm, tn=tn, out_dtype=act_dtype,
                        precise=precise, name=f"{tag}{li}_merge")
        x = _outproj(x, merged, p["w_out"], li=li, tm=tm, tn=tn, precise=precise, name=f"{tag}{li}_outproj")
        x = _moe(x, p["ffn_norm_g"][li].reshape(1, d), w_r, b_r, p["w_gate"], p["w_up"], p["w_down"],
                 li=li, tm_route=tm_route, tm_ffn=tm_ffn, act_dtype=act_dtype, precise=precise,
                 name=f"{tag}{li}_moe")
    y = _rmsnorm(x, p["final_norm_g"].reshape(1, d), tm=min(m, 512), name=f"{tag}_final_norm")
    return (y.reshape(bsz, seq, d), [jnp.stack(a) for a in new_kv],
            jnp.stack(c_all), jnp.stack(n_all), jnp.stack(m_all))


def kernel(x_prompt, x_sample, cache_kv_w128, cache_kv_w512, cache_kv_w2048, state_C, state_n, state_m,
           attn_norm_g, w_in, b_if, ml_norm_g, w_pa, w_pm, w_out, ffn_norm_g, w_rg, b_rg, w_re, b_re,
           w_gate, w_up, w_down, final_norm_g):
    p = dict(attn_norm_g=attn_norm_g, w_in=w_in, b_if=b_if, ml_norm_g=ml_norm_g, w_pa=w_pa, w_pm=w_pm,
             w_out=w_out, ffn_norm_g=ffn_norm_g, w_rg=w_rg, b_rg=b_rg, w_re=w_re, b_re=b_re,
             w_gate=w_gate, w_up=w_up, w_down=w_down, final_norm_g=final_norm_g)
    layer_w = [_layer_weights(p, li) for li in range(DEPTH)]
    bp = x_prompt.shape[0]
    c0 = jnp.zeros((DEPTH, bp, ML_HEADS, ML_V, ML_QK), F32)
    n0 = jnp.zeros((DEPTH, bp, ML_HEADS, ML_QK), F32)
    m0 = jnp.zeros((DEPTH, bp, ML_HEADS), F32)
    y_p, p_kv, p_c, p_n, p_m = _run_trunk(x_prompt, p, layer_w, caches=None,
                                          c0s=c0, n0s=n0, m0s=m0, pos0=0)
    y_s, s_kv, s_c, s_n, s_m = _run_trunk(x_sample, p, layer_w,
                                          caches=[cache_kv_w128, cache_kv_w512, cache_kv_w2048],
                                          c0s=state_C, n0s=state_n, m0s=state_m, pos0=PAST_LEN)
    return (y_p, y_s, p_kv[0], p_kv[1], p_kv[2], p_c, p_n, p_m,
            s_kv[0], s_kv[1], s_kv[2], s_c, s_n, s_m)
```

```python
import functools

import jax
import jax.numpy as jnp
from jax import lax
from jax.experimental import pallas as pl
from jax.experimental.pallas import tpu as pltpu

F32 = jnp.float32
BF16 = jnp.bfloat16

D_MODEL = 2048
DEPTH = 2
PAST_LEN = 16384
ATT_GROUPS = ((128, 1), (512, 4), (2048, 16))
N_ATT_GROUPS = 3
ATT_HEADS = 4
HEAD_DIM = 128
ATT_WIDTH = ATT_HEADS * HEAD_DIM
ATT_SCALE = HEAD_DIM ** -0.5
ROPE_DIM = HEAD_DIM // 4
ROPE_THETA = 500000.0
ML_HEADS = 8
ML_QK = 128
ML_V = 256
ML_K_SCALE = ML_QK ** -0.5
N_GROUPS = 4
EXP_PER_GROUP = 8
N_EXPERTS = N_GROUPS * EXP_PER_GROUP
EXPERT_FF = 256
EPS = 1e-6

O_AQ, O_AK, O_AV = 0, 1536, 3072
O_MQ, O_MK, O_MV, O_MO = 4608, 5632, 6656, 8704
O_MI, O_MF, O_GA, O_GB = 10752, 10760, 10768, 12816
N_MAIN = O_MI
LANES = 128
VMEM_LIMIT_MB = 56


def _cparams(sem):
    return pltpu.CompilerParams(dimension_semantics=sem, vmem_limit_bytes=VMEM_LIMIT_MB * 1024 * 1024)


def _dot(a, b, dims, precise):
    dn = (dims, ((), ()))
    if precise:
        return lax.dot_general(a.astype(F32), b.astype(F32), dn, preferred_element_type=F32,
                               precision=lax.Precision.HIGHEST)
    return lax.dot_general(a.astype(BF16), b.astype(BF16), dn, preferred_element_type=F32)


def _mm(a, b, precise=False):
    return _dot(a, b, ((1,), (0,)), precise)


def _mm_nt(a, b, precise=False):
    return _dot(a, b, ((1,), (1,)), precise)


def _mm_tn(a, b, precise=False):
    return _dot(a, b, ((0,), (0,)), precise)


def _rb(x, precise=False):
    return x.astype(F32) if precise else x.astype(BF16).astype(F32)


def _sigmoid(z):
    return 1.0 / (1.0 + jnp.exp(-z))


def _log_sigmoid(z):
    return jnp.minimum(z, 0.0) - jnp.log1p(jnp.exp(-jnp.abs(z)))


def _norm_matmul_kernel(x_ref, g_ref, w_ref, b_ref, o_ref, xn_ref, *, act, precise):
    @pl.when(pl.program_id(1) == 0)
    def _():
        xf = x_ref[...]
        r = lax.rsqrt(jnp.mean(xf * xf, axis=-1, keepdims=True) + EPS)
        xn_ref[...] = ((xf * r) * g_ref[...]).astype(xn_ref.dtype)

    acc = _mm(xn_ref[...], w_ref[...], precise)
    if act == "sigmoid":
        acc = _sigmoid(acc)
    elif act == "gates":
        z = acc + b_ref[...]
        lane = lax.broadcasted_iota(jnp.int32, z.shape, 1)
        acc = jnp.where(lane < ML_HEADS, z, _log_sigmoid(z))
    o_ref[...] = acc.astype(o_ref.dtype)


def _norm_matmul(x, g, w, bias, *, li, col_blk0, n_out, tm, tn, out_dtype, act, precise, name):
    m, d = x.shape
    if w.ndim == 3:
        w_spec = pl.BlockSpec((None, d, tn), lambda i, j: (li, 0, j + col_blk0))
    else:
        w_spec = pl.BlockSpec((d, tn), lambda i, j: (0, j + col_blk0))
    return pl.pallas_call(
        functools.partial(_norm_matmul_kernel, act=act, precise=precise),
        grid=(m // tm, n_out // tn),
        in_specs=[
            pl.BlockSpec((tm, d), lambda i, j: (i, 0)),
            pl.BlockSpec((1, d), lambda i, j: (0, 0)),
            w_spec,
            pl.BlockSpec((1, tn), lambda i, j: (0, j)),
        ],
        out_specs=pl.BlockSpec((tm, tn), lambda i, j: (i, j)),
        out_shape=jax.ShapeDtypeStruct((m, n_out), out_dtype),
        scratch_shapes=[pltpu.VMEM((tm, d), F32 if precise else BF16)],
        compiler_params=_cparams(("parallel", "arbitrary")),
        name=name,
    )(x, g, w, bias)


def _merge_kernel(att_ref, hg_ref, sa_ref, sb_ref, wpa_ref, wpm_ref, o_ref, *, precise):
    a = _mm(att_ref[...], wpa_ref[...], precise)
    m = _mm(hg_ref[...], wpm_ref[...], precise)
    o_ref[...] = (sa_ref[...].astype(F32) * a + sb_ref[...].astype(F32) * m).astype(o_ref.dtype)


def _merge(att, hg, sg, w_pa, w_pm, *, li, tm, tn, out_dtype, precise, name):
    m = att.shape[0]
    nb = D_MODEL // tn
    return pl.pallas_call(
        functools.partial(_merge_kernel, precise=precise),
        grid=(m // tm, nb),
        in_specs=[
            pl.BlockSpec((tm, ATT_WIDTH), lambda i, j: (i, 0)),
            pl.BlockSpec((tm, ML_HEADS * ML_V), lambda i, j: (i, 0)),
            pl.BlockSpec((tm, tn), lambda i, j: (i, j)),
            pl.BlockSpec((tm, tn), lambda i, j: (i, j + nb)),
            pl.BlockSpec((None, ATT_WIDTH, tn), lambda i, j: (li, 0, j)),
            pl.BlockSpec((None, ML_HEADS * ML_V, tn), lambda i, j: (li, 0, j)),
        ],
        out_specs=pl.BlockSpec((tm, tn), lambda i, j: (i, j)),
        out_shape=jax.ShapeDtypeStruct((m, D_MODEL), out_dtype),
        compiler_params=_cparams(("parallel", "arbitrary")),
        name=name,
    )(att, hg, sg, sg, w_pa, w_pm)


def _outproj_kernel(x_ref, mg_ref, w_ref, o_ref, *, precise):
    o_ref[...] = x_ref[...] + _mm(mg_ref[...], w_ref[...], precise)


def _outproj(x, mg, w_out, *, li, tm, tn, precise, name):
    m = x.shape[0]
    return pl.pallas_call(
        functools.partial(_outproj_kernel, precise=precise),
        grid=(m // tm, D_MODEL // tn),
        in_specs=[
            pl.BlockSpec((tm, tn), lambda i, j: (i, j)),
            pl.BlockSpec((tm, D_MODEL), lambda i, j: (i, 0)),
            pl.BlockSpec((None, D_MODEL, tn), lambda i, j: (li, 0, j)),
        ],
        out_specs=pl.BlockSpec((tm, tn), lambda i, j: (i, j)),
        out_shape=jax.ShapeDtypeStruct((m, D_MODEL), F32),
        compiler_params=_cparams(("parallel", "arbitrary")),
        name=name,
    )(x, mg, w_out)


def _rope_tables(pos):
    half = ROPE_DIM // 2
    inv = jnp.power(jnp.float32(ROPE_THETA), -jnp.arange(half, dtype=F32) / half)
    ang = pos.astype(F32)[:, None] * inv[None, :]
    n = pos.shape[0]
    cos_t = jnp.concatenate([jnp.cos(ang), jnp.cos(ang), jnp.ones((n, HEAD_DIM - ROPE_DIM), F32)], axis=-1)
    sin_t = jnp.concatenate([jnp.sin(ang), jnp.sin(ang), jnp.zeros((n, HEAD_DIM - ROPE_DIM), F32)], axis=-1)
    return cos_t, sin_t


def _rope_perm():
    half = ROPE_DIM // 2
    r = jnp.arange(HEAD_DIM)[:, None]
    c = jnp.arange(HEAD_DIM)[None, :]
    p = jnp.where((c < half) & (r == c + half), -1.0, 0.0) + jnp.where(
        (c >= half) & (c < ROPE_DIM) & (r == c - half), 1.0, 0.0)
    return p.astype(BF16)


ROPE_ROWS = 512


def _attn_prompt_kernel(cos_ref, sin_ref, perm_ref, *refs, seq):
    q_refs, k_refs, v_refs = refs[0:3], refs[3:6], refs[6:9]
    att_ref = refs[9]
    kc_refs, vc_refs = refs[10:13], refs[13:16]
    qs, ks, vs, acc_s, m_s, l_s = refs[16:22]
    perm = perm_ref[...]
    nk = ATT_GROUPS[0][0] // ATT_GROUPS[0][1]
    row = lax.broadcasted_iota(jnp.int32, (nk, nk), 0)
    col = lax.broadcasted_iota(jnp.int32, (nk, nk), 1)
    cur_ok = col <= row
    prev_ok = col >= row
    neg = jnp.float32(-jnp.inf)

    for g, (win, dil) in enumerate(ATT_GROUPS):
        assert win // dil == nk
        span = nk * dil
        shift = dil.bit_length() - 1

        def rope_body(c, carry, g=g):
            r0 = pl.multiple_of(c * ROPE_ROWS, ROPE_ROWS)
            cs = cos_ref[pl.ds(r0, ROPE_ROWS), :]
            sn = sin_ref[pl.ds(r0, ROPE_ROWS), :]
            qb = q_refs[g][0, pl.ds(r0, ROPE_ROWS), :]
            kb = k_refs[g][0, pl.ds(r0, ROPE_ROWS), :]
            qr = qb.astype(F32) * cs + jnp.dot(qb, perm, preferred_element_type=F32) * sn
            kr = kb.astype(F32) * cs + jnp.dot(kb, perm, preferred_element_type=F32) * sn
            qs[pl.ds(r0, ROPE_ROWS), :] = qr
            ks[pl.ds(r0, ROPE_ROWS), :] = kr
            vs[pl.ds(r0, ROPE_ROWS), :] = v_refs[g][0, pl.ds(r0, ROPE_ROWS), :].astype(F32)
            return carry

        lax.fori_loop(0, seq // ROPE_ROWS, rope_body, 0)
        keep = min(win, seq)
        kc_refs[g][0] = ks[seq - keep:seq, :]
        vc_refs[g][0] = vs[seq - keep:seq, :]

        def rows(start, dil=dil):
            if dil == 1:
                return pl.ds(pl.multiple_of(start, nk), nk)
            return pl.ds(start, nk, stride=dil)

        def blk_body(blk, carry, g=g, dil=dil, span=span, shift=shift, rows=rows):
            n = blk >> shift
            r = blk & (dil - 1)
            start = n * span + r
            has_prev = n > 0
            start_p = jnp.where(has_prev, start - span, start)
            q = qs[rows(start), :]
            k_c = ks[rows(start), :]
            k_p = ks[rows(start_p), :]
            v_c = vs[rows(start), :]
            v_p = vs[rows(start_p), :]
            s_c = jnp.where(cur_ok, _mm_nt(q, k_c) * ATT_SCALE, neg)
            no_prev = jnp.where(has_prev, jnp.float32(0.0), neg)
            s_p = jnp.where(prev_ok, _mm_nt(q, k_p) * ATT_SCALE + no_prev, neg)
            m = jnp.maximum(jnp.max(s_c, axis=-1, keepdims=True), jnp.max(s_p, axis=-1, keepdims=True))
            p_c = jnp.exp(s_c - m)
            p_p = jnp.exp(s_p - m)
            l = jnp.sum(p_c, axis=-1, keepdims=True) + jnp.sum(p_p, axis=-1, keepdims=True)
            acc = _mm(p_c, v_c) + _mm(p_p, v_p)
            m_b = jnp.broadcast_to(m, (nk, HEAD_DIM))
            l_b = jnp.broadcast_to(l, (nk, HEAD_DIM))
            if g == 0:
                acc_s[rows(start), :] = acc
                m_s[rows(start), :] = m_b
                l_s[rows(start), :] = l_b
            else:
                m_old = m_s[rows(start), :]
                m_new = jnp.maximum(m_old, m_b)
                a_old = jnp.exp(m_old - m_new)
                a_new = jnp.exp(m_b - m_new)
                acc_s[rows(start), :] = acc_s[rows(start), :] * a_old + acc * a_new
                l_s[rows(start), :] = l_s[rows(start), :] * a_old + l_b * a_new
                m_s[rows(start), :] = m_new
            return carry

        lax.fori_loop(0, seq // nk, blk_body, 0, unroll=2)

    def out_body(c, carry):
        r0 = pl.multiple_of(c * ROPE_ROWS, ROPE_ROWS)
        att_ref[0, pl.ds(r0, ROPE_ROWS), :] = (
            acc_s[pl.ds(r0, ROPE_ROWS), :] / l_s[pl.ds(r0, ROPE_ROWS), :]).astype(att_ref.dtype)
        return carry

    lax.fori_loop(0, seq // ROPE_ROWS, out_body, 0)


def _attn_prompt(u3, cos_t, sin_t, *, name):
    bsz, seq, _ = u3.shape
    perm = _rope_perm()
    in_specs = [
        pl.BlockSpec((seq, HEAD_DIM), lambda b, h: (0, 0)),
        pl.BlockSpec((seq, HEAD_DIM), lambda b, h: (0, 0)),
        pl.BlockSpec((HEAD_DIM, HEAD_DIM), lambda b, h: (0, 0)),
    ]
    for off in (O_AQ, O_AK, O_AV):
        for g in range(N_ATT_GROUPS):
            blk0 = off // HEAD_DIM + g * ATT_HEADS
            in_specs.append(pl.BlockSpec((1, seq, HEAD_DIM), lambda b, h, blk0=blk0: (b, 0, blk0 + h)))
    keeps = [min(w, seq) for w, _ in ATT_GROUPS]
    out_specs = [pl.BlockSpec((1, seq, HEAD_DIM), lambda b, h: (b, 0, h))]
    out_shape = [jax.ShapeDtypeStruct((bsz, seq, ATT_WIDTH), BF16)]
    for _ in range(2):
        for keep in keeps:
            out_specs.append(pl.BlockSpec((1, keep, HEAD_DIM), lambda b, h: (b, 0, h)))
            out_shape.append(jax.ShapeDtypeStruct((bsz, keep, ATT_WIDTH), F32))
    outs = pl.pallas_call(
        functools.partial(_attn_prompt_kernel, seq=seq),
        grid=(bsz, ATT_HEADS),
        in_specs=in_specs,
        out_specs=out_specs,
        out_shape=out_shape,
        scratch_shapes=[pltpu.VMEM((seq, HEAD_DIM), F32) for _ in range(6)],
        compiler_params=_cparams(("parallel", "arbitrary")),
        name=name,
    )(cos_t, sin_t, perm, *([u3] * 9))
    return outs[0], outs[1:4], outs[4:7]


def _attn_sample_kernel(cos_ref, sin_ref, q_ref, k_ref, v_ref, c0_ref, c1_ref, c2_ref,
                        att_ref, kn_ref, *, n_new):
    cache_refs = (c0_ref, c1_ref, c2_ref)
    cs = cos_ref[...]
    sn = sin_ref[...]
    rows_pad = q_ref.shape[1]
    half = ROPE_DIM // 2
    lane = lax.broadcasted_iota(jnp.int32, (rows_pad, HEAD_DIM), 1)

    def rot_half(x):
        return jnp.where(lane < half, -pltpu.roll(x, HEAD_DIM - half, axis=1),
                         jnp.where(lane < ROPE_DIM, pltpu.roll(x, half, axis=1), 0.0))

    nk = ATT_GROUPS[0][0] // ATT_GROUPS[0][1]
    kv_rows = 2 * ATT_HEADS
    w_iota = lax.broadcasted_iota(jnp.int32, (nk, 1), 0)
    u_iota = lax.broadcasted_iota(jnp.int32, (rows_pad, 1), 0)
    neg = jnp.float32(-jnp.inf)
    att_ref[...] = jnp.zeros(att_ref.shape, att_ref.dtype)

    for h in range(ATT_HEADS):
        parts = [[None] * N_ATT_GROUPS for _ in range(n_new)]
        for g, (win, dil) in enumerate(ATT_GROUPS):
            hs = (g * ATT_HEADS + h) * HEAD_DIM
            qh = q_ref[0, :, hs:hs + HEAD_DIM]
            kh = k_ref[0, :, hs:hs + HEAD_DIM]
            vh = v_ref[0, :, hs:hs + HEAD_DIM]
            qr = _rb(qh * cs + rot_half(qh) * sn)
            kr = kh * cs + rot_half(kh) * sn
            kn_ref[0, :, hs:hs + HEAD_DIM] = kr
            kr = _rb(kr)
            vh = _rb(vh)
            cref = cache_refs[g]
            for t in range(n_new):
                q_t = qr[t:t + 1, :]
                if dil == 1:
                    p0 = 0
                else:
                    p0 = t
                k_c = _rb(cref[pl.ds(p0 * kv_rows + h, nk, stride=kv_rows * dil), :])
                v_c = _rb(cref[pl.ds(p0 * kv_rows + ATT_HEADS + h, nk, stride=kv_rows * dil), :])
                s_c = jnp.sum(k_c * q_t, axis=-1, keepdims=True) * ATT_SCALE
                s_n = jnp.sum(kr * q_t, axis=-1, keepdims=True) * ATT_SCALE
                if dil == 1:
                    s_c = jnp.where(w_iota >= t, s_c, neg)
                    s_n = jnp.where(u_iota <= t, s_n, neg)
                else:
                    s_n = jnp.where(u_iota == t, s_n, neg)
                m = jnp.maximum(jnp.max(s_c, axis=0, keepdims=True), jnp.max(s_n, axis=0, keepdims=True))
                p_c = jnp.exp(s_c - m)
                p_n = jnp.exp(s_n - m)
                l = jnp.sum(p_c, axis=0, keepdims=True) + jnp.sum(p_n, axis=0, keepdims=True)
                acc = jnp.sum(_rb(p_c) * v_c, axis=0, keepdims=True) + jnp.sum(_rb(p_n) * vh, axis=0, keepdims=True)
                parts[t][g] = (m, l, acc)
        for t in range(n_new):
            m_all = functools.reduce(jnp.maximum, [p[0] for p in parts[t]])
            num = sum(jnp.exp(p[0] - m_all) * p[2] for p in parts[t])
            den = sum(jnp.exp(p[0] - m_all) * p[1] for p in parts[t])
            att_ref[0, t:t + 1, h * HEAD_DIM:(h + 1) * HEAD_DIM] = num / den


def _attn_sample(u3, caches2d, cos_t, sin_t, *, li, n_new, name):
    bsz, rows_pad, _ = u3.shape
    qkv_w = N_ATT_GROUPS * ATT_WIDTH
    in_specs = [
        pl.BlockSpec((rows_pad, HEAD_DIM), lambda b: (0, 0)),
        pl.BlockSpec((rows_pad, HEAD_DIM), lambda b: (0, 0)),
        pl.BlockSpec((1, rows_pad, qkv_w), lambda b: (b, 0, O_AQ // qkv_w)),
        pl.BlockSpec((1, rows_pad, qkv_w), lambda b: (b, 0, O_AK // qkv_w)),
        pl.BlockSpec((1, rows_pad, qkv_w), lambda b: (b, 0, O_AV // qkv_w)),
    ]
    for (win, _), c in zip(ATT_GROUPS, caches2d):
        rows = c.shape[0] // (DEPTH * bsz)
        in_specs.append(pl.BlockSpec((rows, HEAD_DIM), lambda b, bsz=bsz: (li * bsz + b, 0)))
    return pl.pallas_call(
        functools.partial(_attn_sample_kernel, n_new=n_new),
        grid=(bsz,),
        in_specs=in_specs,
        out_specs=[
            pl.BlockSpec((1, rows_pad, ATT_WIDTH), lambda b: (b, 0, 0)),
            pl.BlockSpec((1, rows_pad, qkv_w), lambda b: (b, 0, 0)),
        ],
        out_shape=[
            jax.ShapeDtypeStruct((bsz, rows_pad, ATT_WIDTH), F32),
            jax.ShapeDtypeStruct((bsz, rows_pad, qkv_w), F32),
        ],
        compiler_params=_cparams(("parallel",)),
        name=name,
    )(cos_t, sin_t, u3, u3, u3, *caches2d)


ML_HPB = 2


def _mlstm_kernel(q_ref, k_ref, v_ref, mo_ref, gc_ref, gr_ref, mlg_ref, c0_ref, n0_ref, m0_ref,
                  h_ref, c_out, n_out, m_out, ct_s, n_s, m_s, *, chunk, precise):
    s_idx = pl.program_id(2)
    n_s_blocks = pl.num_programs(2)
    sb = q_ref.shape[1]
    L = chunk

    @pl.when(s_idx == 0)
    def _():
        for hh in range(ML_HPB):
            ct_s[hh] = c0_ref[0, hh].T
            n_s[hh] = n0_ref[0, 0, hh:hh + 1, :]
            m_s[hh] = m0_ref[0, 0, hh:hh + 1, :]

    row = lax.broadcasted_iota(jnp.int32, (L, L), 0)
    col = lax.broadcasted_iota(jnp.int32, (L, L), 1)
    causal = row >= col
    neg = jnp.float32(-jnp.inf)

    def chunk_body(c, carry):
        r0 = pl.multiple_of(c * L, L)
        gc = gc_ref[0, 0, pl.ds(r0, L), :]
        gr = gr_ref[0, 0, :, pl.ds(r0, L)]
        for hh in range(ML_HPB):
            ig_col = gc[:, hh:hh + 1]
            lf_col = gc[:, ML_HPB + hh:ML_HPB + hh + 1]
            ig_row = gr[hh:hh + 1, :]
            lf_row = gr[ML_HPB + hh:ML_HPB + hh + 1, :]
            b_col = jnp.sum(jnp.where(causal, lf_row, 0.0), axis=1, keepdims=True)
            b_row = jnp.sum(jnp.where(row <= col, lf_col, 0.0), axis=0, keepdims=True)
            qf = q_ref[0, pl.ds(r0, L), hh * ML_QK:(hh + 1) * ML_QK]
            kf = k_ref[0, pl.ds(r0, L), hh * ML_QK:(hh + 1) * ML_QK]
            vf = v_ref[0, pl.ds(r0, L), hh * ML_V:(hh + 1) * ML_V]
            mo = mo_ref[0, pl.ds(r0, L), hh * ML_V:(hh + 1) * ML_V].astype(F32)
            ct = ct_s[hh]
            n_row = n_s[hh]
            m_prev = m_s[hh][:, 0:1]
            inter = b_col + m_prev
            dm = jnp.where(causal, b_col - b_row + ig_row, neg)
            m_t = jnp.maximum(inter, jnp.max(dm, axis=1, keepdims=True))
            w_intra = jnp.exp(dm - m_t)
            w_inter = jnp.exp(inter - m_t)
            ks = kf.astype(F32) * ML_K_SCALE
            sw = _mm_nt(qf, ks, precise) * w_intra
            num = w_inter * _mm(qf, ct, precise) + _mm(sw, vf, precise)
            den = w_inter * jnp.sum(_rb(qf, precise) * _rb(n_row, precise), axis=1, keepdims=True) + jnp.sum(
                sw, axis=1, keepdims=True)
            hv = num / jnp.maximum(jnp.abs(den), jnp.exp(-m_t))
            hn = hv * lax.rsqrt(jnp.mean(hv * hv, axis=-1, keepdims=True) + EPS)
            hn = hn * mlg_ref[:, hh * ML_V:(hh + 1) * ML_V]
            h_ref[0, pl.ds(r0, L), hh * ML_V:(hh + 1) * ML_V] = (hn * _sigmoid(mo)).astype(h_ref.dtype)
            b_last = b_col[L - 1:L, :]
            m_new = m_t[L - 1:L, :]
            decay = jnp.exp(b_last + m_prev - m_new)
            wk_col = jnp.exp(b_last - b_col + ig_col - m_new)
            ct_s[hh] = decay * ct + _mm_tn(ks * wk_col, vf, precise)
            n_s[hh] = decay * n_row + jnp.sum(_rb(ks, precise) * _rb(wk_col, precise), axis=0, keepdims=True)
            m_s[hh] = jnp.broadcast_to(m_new, (1, LANES))
        return carry

    lax.fori_loop(0, sb // L, chunk_body, 0)

    @pl.when(s_idx == n_s_blocks - 1)
    def _():
        for hh in range(ML_HPB):
            c_out[0, hh] = ct_s[hh].T
            n_out[0, 0, hh:hh + 1, :] = n_s[hh]
            m_out[0, 0, hh:hh + 1, :] = m_s[hh]


def _mlstm(u3, gc, gr, mlg, c0, n0, m0, *, chunk, sb, out_dtype, precise, name):
    bsz, seq, _ = u3.shape
    hg_n = ML_HEADS // ML_HPB
    qk_w, v_w = ML_HPB * ML_QK, ML_HPB * ML_V
    n0r = n0.reshape(bsz, hg_n, ML_HPB, ML_QK)
    m0r = jnp.broadcast_to(m0.reshape(bsz, hg_n, ML_HPB, 1), (bsz, hg_n, ML_HPB, LANES))
    state_spec = pl.BlockSpec((1, 1, ML_HPB, LANES), lambda b, hg, s: (b, hg, 0, 0))
    c_spec = pl.BlockSpec((1, ML_HPB, ML_V, ML_QK), lambda b, hg, s: (b, hg, 0, 0))
    h, c_new, n_new, m_new = pl.pallas_call(
        functools.partial(_mlstm_kernel, chunk=chunk, precise=precise),
        grid=(bsz, hg_n, seq // sb),
        in_specs=[
            pl.BlockSpec((1, sb, qk_w), lambda b, hg, s: (b, s, O_MQ // qk_w + hg)),
            pl.BlockSpec((1, sb, qk_w), lambda b, hg, s: (b, s, O_MK // qk_w + hg)),
            pl.BlockSpec((1, sb, v_w), lambda b, hg, s: (b, s, O_MV // v_w + hg)),
            pl.BlockSpec((1, sb, v_w), lambda b, hg, s: (b, s, O_MO // v_w + hg)),
            pl.BlockSpec((1, 1, sb, 2 * ML_HPB), lambda b, hg, s: (b, hg, s, 0)),
            pl.BlockSpec((1, 1, 2 * ML_HPB, sb), lambda b, hg, s: (b, hg, 0, s)),
            pl.BlockSpec((1, v_w), lambda b, hg, s: (0, hg)),
            c_spec, state_spec, state_spec,
        ],
        out_specs=[
            pl.BlockSpec((1, sb, v_w), lambda b, hg, s: (b, s, hg)),
            c_spec, state_spec, state_spec,
        ],
        out_shape=[
            jax.ShapeDtypeStruct((bsz, seq, ML_HEADS * ML_V), out_dtype),
            jax.ShapeDtypeStruct((bsz, ML_HEADS, ML_V, ML_QK), F32),
            jax.ShapeDtypeStruct((bsz, hg_n, ML_HPB, LANES), F32),
            jax.ShapeDtypeStruct((bsz, hg_n, ML_HPB, LANES), F32),
        ],
        scratch_shapes=[
            pltpu.VMEM((ML_HPB, ML_QK, ML_V), F32),
            pltpu.VMEM((ML_HPB, 1, ML_QK), F32),
            pltpu.VMEM((ML_HPB, 1, LANES), F32),
        ],
        compiler_params=_cparams(("parallel", "parallel", "arbitrary")),
        name=name,
    )(u3, u3, u3, u3, gc, gr, mlg, c0, n0r, m0r)
    return h, c_new, n_new.reshape(bsz, ML_HEADS, ML_QK), m_new[..., 0].reshape(bsz, ML_HEADS)


def _gate_layouts(gates, bsz, seq, seq_pad):
    hg_n = ML_HEADS // ML_HPB
    ig = gates[:, :ML_HEADS].reshape(bsz, seq, ML_HEADS)
    lf = gates[:, ML_HEADS:2 * ML_HEADS].reshape(bsz, seq, ML_HEADS)
    if seq_pad > seq:
        ig = jnp.pad(ig, ((0, 0), (0, seq_pad - seq), (0, 0)), constant_values=-jnp.inf)
        lf = jnp.pad(lf, ((0, 0), (0, seq_pad - seq), (0, 0)))
    ig = ig.reshape(bsz, seq_pad, hg_n, ML_HPB).transpose(0, 2, 1, 3)
    lf = lf.reshape(bsz, seq_pad, hg_n, ML_HPB).transpose(0, 2, 1, 3)
    gc = jnp.concatenate([ig, lf], axis=-1)
    return gc, gc.transpose(0, 1, 3, 2)


def _router_kernel(x_ref, g_ref, w_ref, b_ref, xn_ref, route_ref, *, precise):
    xf = x_ref[...]
    r = lax.rsqrt(jnp.mean(xf * xf, axis=-1, keepdims=True) + EPS)
    xn = (xf * r) * g_ref[...]
    xn_ref[...] = xn.astype(xn_ref.dtype)
    logits = _mm(xn, w_ref[...], precise) + b_ref[...]
    lane = lax.broadcasted_iota(jnp.int32, logits.shape, 1).astype(F32)
    neg = jnp.float32(-jnp.inf)
    big = jnp.float32(LANES)
    gl = jnp.where(lane < N_GROUPS, logits, neg)
    g_max = jnp.max(gl, axis=-1, keepdims=True)
    g_val = 1.0 / jnp.sum(jnp.exp(gl - g_max), axis=-1, keepdims=True)
    g_idx = jnp.min(jnp.where(gl == g_max, lane, big), axis=-1, keepdims=True)
    lo = N_GROUPS + EXP_PER_GROUP * g_idx
    es = jnp.where((lane >= lo) & (lane < lo + EXP_PER_GROUP), logits, neg)
    t0 = jnp.max(es, axis=-1, keepdims=True)
    i0 = jnp.min(jnp.where(es == t0, lane, big), axis=-1, keepdims=True)
    es1 = jnp.where(lane == i0, neg, es)
    t1 = jnp.max(es1, axis=-1, keepdims=True)
    i1 = jnp.min(jnp.where(es1 == t1, lane, big), axis=-1, keepdims=True)
    e1 = jnp.exp(t1 - t0)
    w0 = g_val / (1.0 + e1)
    w1 = g_val * e1 / (1.0 + e1)
    route = jnp.where(lane == 0, i0 - N_GROUPS,
                      jnp.where(lane == 1, i1 - N_GROUPS,
                                jnp.where(lane == 2, w0, jnp.where(lane == 3, w1, 0.0))))
    route_ref[...] = route


def _router(x, g, w_r, b_r, *, tm, precise, name):
    m, d = x.shape
    return pl.pallas_call(
        functools.partial(_router_kernel, precise=precise),
        grid=(m // tm,),
        in_specs=[
            pl.BlockSpec((tm, d), lambda i: (i, 0)),
            pl.BlockSpec((1, d), lambda i: (0, 0)),
            pl.BlockSpec((d, LANES), lambda i: (0, 0)),
            pl.BlockSpec((1, LANES), lambda i: (0, 0)),
        ],
        out_specs=[pl.BlockSpec((tm, d), lambda i: (i, 0)), pl.BlockSpec((tm, LANES), lambda i: (i, 0))],
        out_shape=[jax.ShapeDtypeStruct((m, d), F32), jax.ShapeDtypeStruct((m, LANES), F32)],
        compiler_params=_cparams(("parallel",)),
        name=name,
    )(x, g, w_r, b_r)


def _ffn_kernel(te_ref, nv_ref, xs_ref, wr_ref, wg_ref, wu_ref, wd_ref, y_ref, wg_s, wu_s, wd_s, *, precise):
    i = pl.program_id(0)
    valid = i < nv_ref[0]
    prev = te_ref[jnp.maximum(i - 1, 0)]

    @pl.when(valid & ((i == 0) | (te_ref[i] != prev)))
    def _():
        wg_s[...] = wg_ref[...].astype(wg_s.dtype)
        wu_s[...] = wu_ref[...].astype(wu_s.dtype)
        wd_s[...] = wd_ref[...].astype(wd_s.dtype)

    @pl.when(valid)
    def _():
        xs = xs_ref[...]
        gt = _mm(xs, wg_s[...], precise)
        up = _mm(xs, wu_s[...], precise)
        y_ref[...] = _mm((gt * _sigmoid(gt) * up) * wr_ref[...], wd_s[...], precise)


def _ffn(xs, wrow, te, nv, w_gate, w_up, w_down, *, li, tm, precise, name):
    rows, d = xs.shape
    n_tiles = rows // tm
    f = EXPERT_FF
    wdt = F32 if precise else BF16

    def row_blk(i, te, nv):
        return (jnp.minimum(i, nv[0] - 1), 0)

    return pl.pallas_call(
        functools.partial(_ffn_kernel, precise=precise),
        grid_spec=pltpu.PrefetchScalarGridSpec(
            num_scalar_prefetch=2,
            grid=(n_tiles,),
            in_specs=[
                pl.BlockSpec((tm, d), row_blk),
                pl.BlockSpec((tm, 1), row_blk),
                pl.BlockSpec((None, d, f), lambda i, te, nv: (li, 0, te[i])),
                pl.BlockSpec((None, d, f), lambda i, te, nv: (li, 0, te[i])),
                pl.BlockSpec((None, f, d), lambda i, te, nv: (li, te[i], 0)),
            ],
            out_specs=pl.BlockSpec((tm, d), row_blk),
            scratch_shapes=[pltpu.VMEM((d, f), wdt), pltpu.VMEM((d, f), wdt), pltpu.VMEM((f, d), wdt)],
        ),
        out_shape=jax.ShapeDtypeStruct((rows, d), F32),
        compiler_params=_cparams(("arbitrary",)),
        name=name,
    )(te, nv, xs, wrow, w_gate, w_up, w_down)


def _rank_kernel(route_ref, offs_ref, dest_ref, run_s):
    @pl.when(pl.program_id(0) == 0)
    def _():
        run_s[...] = jnp.zeros(run_s.shape, run_s.dtype)

    route = route_ref[...]
    tm = route.shape[0]
    lane = lax.broadcasted_iota(jnp.int32, route.shape, 1).astype(F32)
    is0 = lane == route[:, 0:1]
    is1 = lane == route[:, 1:2]
    onehot = jnp.where(is0, 1.0, 0.0) + jnp.where(is1, 1.0, 0.0)
    r = lax.broadcasted_iota(jnp.int32, (tm, tm), 0)
    c = lax.broadcasted_iota(jnp.int32, (tm, tm), 1)
    earlier = jnp.where(c < r, 1.0, 0.0)
    before = _mm(earlier, onehot)
    base = offs_ref[...] + run_s[...] + before
    d0 = jnp.sum(jnp.where(is0, base, 0.0), axis=-1, keepdims=True)
    d1 = jnp.sum(jnp.where(is1, base, 0.0), axis=-1, keepdims=True)
    run_s[...] = run_s[...] + jnp.sum(onehot, axis=0, keepdims=True)
    dest_ref[...] = jnp.where(lane == 0.0, d0, jnp.where(lane == 1.0, d1, 0.0)).astype(jnp.int32)


def _rank(route, offs_row, *, tm, name):
    m = route.shape[0]
    return pl.pallas_call(
        _rank_kernel,
        grid=(m // tm,),
        in_specs=[pl.BlockSpec((tm, LANES), lambda i: (i, 0)), pl.BlockSpec((1, LANES), lambda i: (0, 0))],
        out_specs=pl.BlockSpec((tm, LANES), lambda i: (i, 0)),
        out_shape=jax.ShapeDtypeStruct((m, LANES), jnp.int32),
        scratch_shapes=[pltpu.VMEM((1, LANES), F32)],
        compiler_params=_cparams(("arbitrary",)),
        name=name,
    )(route, offs_row)


def _row_copy(src, src_row, dst, dst_row, sem):
    return pltpu.make_async_copy(src.at[pl.ds(src_row, 1)], dst.at[pl.ds(dst_row, 1)], sem)


def _scatter_kernel(dest_ref, last_ref, xn_ref, xs_hbm, zero_s, sem, zsem, *, tile):
    i = pl.program_id(0)
    tm = xn_ref.shape[0]

    def zero_copy(e):
        return pltpu.make_async_copy(zero_s, xs_hbm.at[pl.ds(pl.multiple_of(last_ref[e], tile), tile)], zsem)

    @pl.when(i == 0)
    def _():
        zero_s[...] = jnp.zeros(zero_s.shape, zero_s.dtype)

        def z_start(e, carry):
            @pl.when(last_ref[e] >= 0)
            def _():
                zero_copy(e).start()
            return carry

        def z_wait(e, carry):
            @pl.when(last_ref[e] >= 0)
            def _():
                zero_copy(e).wait()
            return carry

        lax.fori_loop(0, N_EXPERTS, z_start, 0)
        lax.fori_loop(0, N_EXPERTS, z_wait, 0)

    base = i * (2 * tm)

    def start(r, carry):
        _row_copy(xn_ref, r, xs_hbm, dest_ref[base + 2 * r], sem).start()
        _row_copy(xn_ref, r, xs_hbm, dest_ref[base + 2 * r + 1], sem).start()
        return carry

    def wait(r, carry):
        _row_copy(xn_ref, r, xs_hbm, dest_ref[base + 2 * r], sem).wait()
        _row_copy(xn_ref, r, xs_hbm, dest_ref[base + 2 * r + 1], sem).wait()
        return carry

    lax.fori_loop(0, tm, start, 0)
    lax.fori_loop(0, tm, wait, 0)


def _scatter(dest, last_tile, xn, *, rows, tm, tile, name):
    m, d = xn.shape
    return pl.pallas_call(
        functools.partial(_scatter_kernel, tile=tile),
        grid_spec=pltpu.PrefetchScalarGridSpec(
            num_scalar_prefetch=2,
            grid=(m // tm,),
            in_specs=[pl.BlockSpec((tm, d), lambda i, dest, last: (i, 0))],
            out_specs=pl.BlockSpec(memory_space=pl.ANY),
            scratch_shapes=[pltpu.VMEM((tile, d), F32), pltpu.SemaphoreType.DMA(()), pltpu.SemaphoreType.DMA(())],
        ),
        out_shape=jax.ShapeDtypeStruct((rows, d), F32),
        compiler_params=_cparams(("arbitrary",)),
        name=name,
    )(dest, last_tile, xn)


def _combine_kernel(dest_ref, x_ref, y_hbm, o_ref, ya, yb, sem):
    i = pl.program_id(0)
    tm = x_ref.shape[0]
    base = i * (2 * tm)

    def start(r, carry):
        _row_copy(y_hbm, dest_ref[base + 2 * r], ya, r, sem).start()
        _row_copy(y_hbm, dest_ref[base + 2 * r + 1], yb, r, sem).start()
        return carry

    def wait(r, carry):
        _row_copy(y_hbm, dest_ref[base + 2 * r], ya, r, sem).wait()
        _row_copy(y_hbm, dest_ref[base + 2 * r + 1], yb, r, sem).wait()
        return carry

    lax.fori_loop(0, tm, start, 0)
    lax.fori_loop(0, tm, wait, 0)
    o_ref[...] = x_ref[...] + (ya[...] + yb[...])


def _combine(dest, x, y, *, tm, name):
    m, d = x.shape
    return pl.pallas_call(
        _combine_kernel,
        grid_spec=pltpu.PrefetchScalarGridSpec(
            num_scalar_prefetch=1,
            grid=(m // tm,),
            in_specs=[
                pl.BlockSpec((tm, d), lambda i, dest: (i, 0)),
                pl.BlockSpec(memory_space=pl.ANY),
            ],
            out_specs=pl.BlockSpec((tm, d), lambda i, dest: (i, 0)),
            scratch_shapes=[pltpu.VMEM((tm, d), F32), pltpu.VMEM((tm, d), F32), pltpu.SemaphoreType.DMA(())],
        ),
        out_shape=jax.ShapeDtypeStruct((m, d), F32),
        compiler_params=_cparams(("arbitrary",)),
        name=name,
    )(dest, x, y)


def _moe(x, g, w_r, b_r, w_gate, w_up, w_down, *, li, tm_tok, tm_ffn, precise, name):
    m = x.shape[0]
    xn, route = _router(x, g, w_r, b_r, tm=tm_tok, precise=precise, name=name + "_router")
    eid = route[:, 0:2].astype(jnp.int32)
    counts = jnp.sum((eid[:, :, None] == jnp.arange(N_EXPERTS, dtype=jnp.int32)).astype(jnp.int32), axis=(0, 1))
    padded = ((counts + tm_ffn - 1) // tm_ffn) * tm_ffn
    ends = jnp.cumsum(padded)
    offs = ends - padded
    rows = 2 * m + N_EXPERTS * tm_ffn
    tile_start = jnp.arange(rows // tm_ffn, dtype=jnp.int32) * tm_ffn
    te = jnp.minimum(jnp.sum((ends[None, :] <= tile_start[:, None]).astype(jnp.int32), axis=1), N_EXPERTS - 1)
    nv = (ends[-1] // tm_ffn).reshape(1)
    last_tile = jnp.where(padded > 0, ends - tm_ffn, -1)
    offs_row = jnp.pad(offs.astype(F32), (0, LANES - N_EXPERTS)).reshape(1, LANES)
    dest = _rank(route, offs_row, tm=tm_tok, name=name + "_rank")[:, 0:2].reshape(-1)
    xs = _scatter(dest, last_tile, xn, rows=rows, tm=tm_tok, tile=tm_ffn, name=name + "_scatter")
    wrow = jnp.zeros((rows, 1), F32).at[dest, 0].set(route[:, 2:4].reshape(-1))
    y = _ffn(xs, wrow, te, nv, w_gate, w_up, w_down, li=li, tm=tm_ffn, precise=precise, name=name + "_ffn")
    return _combine(dest, x, y, tm=tm_tok, name=name + "_combine")


def _rmsnorm_kernel(x_ref, g_ref, o_ref):
    xf = x_ref[...]
    r = lax.rsqrt(jnp.mean(xf * xf, axis=-1, keepdims=True) + EPS)
    o_ref[...] = (xf * r) * g_ref[...]


def _rmsnorm(x, g, *, tm, name):
    m, d = x.shape
    return pl.pallas_call(
        _rmsnorm_kernel,
        grid=(m // tm,),
        in_specs=[pl.BlockSpec((tm, d), lambda i: (i, 0)), pl.BlockSpec((1, d), lambda i: (0, 0))],
        out_specs=pl.BlockSpec((tm, d), lambda i: (i, 0)),
        out_shape=jax.ShapeDtypeStruct((m, d), F32),
        compiler_params=_cparams(("parallel",)),
        name=name,
    )(x, g)


def _layer_weights(p, li):
    w_in_l = p["w_in"][li]
    w_sg = w_in_l[:, O_GA:]
    w_if = jnp.pad(w_in_l[:, O_MI:O_GA], ((0, 0), (0, LANES - 2 * ML_HEADS)))
    b_if = jnp.pad(p["b_if"][li], (0, LANES - 2 * ML_HEADS)).reshape(1, LANES)
    w_r = jnp.pad(jnp.concatenate([p["w_rg"][li], p["w_re"][li]], axis=1),
                  ((0, 0), (0, LANES - N_GROUPS - N_EXPERTS)))
    b_r = jnp.pad(jnp.concatenate([p["b_rg"][li], p["b_re"][li]]), (0, LANES - N_GROUPS - N_EXPERTS)).reshape(1, LANES)
    return w_sg, w_if, b_if, w_r, b_r


def _run_trunk(x3, p, layer_w, *, caches, c0s, n0s, m0s, pos0):
    bsz, seq, d = x3.shape
    m = bsz * seq
    x = x3.reshape(m, d)
    precise = False
    if caches is None:
        act_dtype, tag = BF16, "p"
        tm, tn, tm_tok, tm_ffn, seq_pad = 1024, 512, 256, 256, seq
    else:
        act_dtype, tag = F32, "s"
        tm, tn, tm_tok, tm_ffn, seq_pad = m, 512, m, 16, 16
    cos_t, sin_t = _rope_tables(pos0 + jnp.arange(seq_pad, dtype=jnp.int32))
    zeros_bias = jnp.zeros((1, N_MAIN), F32)
    new_kv = [[] for _ in ATT_GROUPS]
    c_all, n_all, m_all = [], [], []
    for li in range(DEPTH):
        w_sg, w_if, b_if, w_r, b_r = layer_w[li]
        g_attn = p["attn_norm_g"][li].reshape(1, d)
        common = dict(li=li, tm=tm, precise=precise)
        u = _norm_matmul(x, g_attn, p["w_in"], zeros_bias, col_blk0=0, n_out=N_MAIN, tn=tn,
                         out_dtype=act_dtype, act="none", name=f"{tag}{li}_inproj", **common)
        sg = _norm_matmul(x, g_attn, w_sg, zeros_bias, col_blk0=0, n_out=2 * D_MODEL, tn=tn,
                          out_dtype=act_dtype, act="sigmoid", name=f"{tag}{li}_gateproj", **common)
        gates = _norm_matmul(x, g_attn, w_if, b_if, col_blk0=0, n_out=LANES, tn=LANES,
                             out_dtype=F32, act="gates", name=f"{tag}{li}_ifproj", **common)
        u3 = u.reshape(bsz, seq, N_MAIN)
        if seq_pad > seq:
            u3 = jnp.pad(u3, ((0, 0), (0, seq_pad - seq), (0, 0)))
        if caches is None:
            att3, kcs, vcs = _attn_prompt(u3, cos_t, sin_t, name=f"{tag}{li}_attn")
            for gi in range(N_ATT_GROUPS):
                keep = kcs[gi].shape[1]
                new_kv[gi].append(jnp.stack([kcs[gi].reshape(bsz, keep, ATT_HEADS, HEAD_DIM),
                                             vcs[gi].reshape(bsz, keep, ATT_HEADS, HEAD_DIM)], axis=2))
            att = att3.reshape(m, ATT_WIDTH)
        else:
            caches2d = [c.reshape(-1, HEAD_DIM) for c in caches]
            att3, kn3 = _attn_sample(u3, caches2d, cos_t, sin_t, li=li, n_new=seq, name=f"{tag}{li}_attn")
            att = att3[:, :seq].reshape(m, ATT_WIDTH)
            nh = N_ATT_GROUPS * ATT_HEADS
            k_new = kn3[:, :seq].reshape(bsz, seq, nh, HEAD_DIM)
            v_new = u3[:, :seq, O_AV:O_AV + nh * HEAD_DIM].reshape(bsz, seq, nh, HEAD_DIM)
            for gi in range(N_ATT_GROUPS):
                sl = slice(gi * ATT_HEADS, (gi + 1) * ATT_HEADS)
                new_kv[gi].append(jnp.stack([k_new[:, :, sl], v_new[:, :, sl]], axis=2))
        gc, gr = _gate_layouts(gates, bsz, seq, seq_pad)
        mlg = p["ml_norm_g"][li].reshape(1, ML_HEADS * ML_V)
        hg3, c_new, n_new, m_new = _mlstm(
            u3, gc, gr, mlg, c0s[li], n0s[li], m0s[li], chunk=min(128, seq_pad), sb=min(512, seq_pad),
            out_dtype=act_dtype, precise=precise, name=f"{tag}{li}_mlstm")
        hg = hg3[:, :seq].reshape(m, ML_HEADS * ML_V)
        c_all.append(c_new)
        n_all.append(n_new)
        m_all.append(m_new)
        merged = _merge(att, hg, sg, p["w_pa"], p["w_pm"], li=li, tm=tm, tn=tn, out_dtype=act_dtype,
                        precise=precise, name=f"{tag}{li}_merge")
        x = _outproj(x, merged, p["w_out"], li=li, tm=tm, tn=tn, precise=precise, name=f"{tag}{li}_outproj")
        x = _moe(x, p["ffn_norm_g"][li].reshape(1, d), w_r, b_r, p["w_gate"], p["w_up"], p["w_down"],
                 li=li, tm_tok=tm_tok, tm_ffn=tm_ffn, precise=precise, name=f"{tag}{li}_moe")
    y = _rmsnorm(x, p["final_norm_g"].reshape(1, d), tm=min(m, 512), name=f"{tag}_final_norm")
    new_kv = [jnp.stack(a) for a in new_kv]
    if caches is not None:
        new_kv = [jnp.concatenate([c[:, :, seq:], fresh], axis=2) for c, fresh in zip(caches, new_kv)]
    return (y.reshape(bsz, seq, d), new_kv, jnp.stack(c_all), jnp.stack(n_all), jnp.stack(m_all))


def kernel(x_prompt, x_sample, cache_kv_w128, cache_kv_w512, cache_kv_w2048, state_C, state_n, state_m,
           attn_norm_g, w_in, b_if, ml_norm_g, w_pa, w_pm, w_out, ffn_norm_g, w_rg, b_rg, w_re, b_re,
           w_gate, w_up, w_down, final_norm_g):
    p = dict(attn_norm_g=attn_norm_g, w_in=w_in, b_if=b_if, ml_norm_g=ml_norm_g, w_pa=w_pa, w_pm=w_pm,
             w_out=w_out, ffn_norm_g=ffn_norm_g, w_rg=w_rg, b_rg=b_rg, w_re=w_re, b_re=b_re,
             w_gate=w_gate, w_up=w_up, w_down=w_down, final_norm_g=final_norm_g)
    layer_w = [_layer_weights(p, li) for li in range(DEPTH)]
    bp = x_prompt.shape[0]
    c0 = jnp.zeros((DEPTH, bp, ML_HEADS, ML_V, ML_QK), F32)
    n0 = jnp.zeros((DEPTH, bp, ML_HEADS, ML_QK), F32)
    m0 = jnp.zeros((DEPTH, bp, ML_HEADS), F32)
    y_p, p_kv, p_c, p_n, p_m = _run_trunk(x_prompt, p, layer_w, caches=None,
                                          c0s=c0, n0s=n0, m0s=m0, pos0=0)
    y_s, s_kv, s_c, s_n, s_m = _run_trunk(x_sample, p, layer_w,
                                          caches=[cache_kv_w128, cache_kv_w512, cache_kv_w2048],
                                          c0s=state_C, n0s=state_n, m0s=state_m, pos0=PAST_LEN)
    return (y_p, y_s, p_kv[0], p_kv[1], p_kv[2], p_c, p_n, p_m,
            s_kv[0], s_kv[1], s_kv[2], s_c, s_n, s_m)
```

```python
import functools

import jax
import jax.numpy as jnp
from jax import lax
from jax.experimental import pallas as pl
from jax.experimental.pallas import tpu as pltpu

F32 = jnp.float32
BF16 = jnp.bfloat16

D_MODEL = 2048
DEPTH = 2
PAST_LEN = 16384
ATT_GROUPS = ((128, 1), (512, 4), (2048, 16))
N_ATT_GROUPS = 3
ATT_HEADS = 4
HEAD_DIM = 128
ATT_WIDTH = ATT_HEADS * HEAD_DIM
ATT_SCALE = HEAD_DIM ** -0.5
ROPE_DIM = HEAD_DIM // 4
ROPE_THETA = 500000.0
ML_HEADS = 8
ML_QK = 128
ML_V = 256
ML_K_SCALE = ML_QK ** -0.5
N_GROUPS = 4
EXP_PER_GROUP = 8
N_EXPERTS = N_GROUPS * EXP_PER_GROUP
EXPERT_FF = 256
EPS = 1e-6

O_AQ, O_AK, O_AV = 0, 1536, 3072
O_MQ, O_MK, O_MV, O_MO = 4608, 5632, 6656, 8704
O_MI, O_MF, O_GA, O_GB = 10752, 10760, 10768, 12816
N_MAIN = O_MI
LANES = 128
TN_MAIN = 768
VMEM_LIMIT_MB = 56


def _cparams(sem):
    return pltpu.CompilerParams(dimension_semantics=sem, vmem_limit_bytes=VMEM_LIMIT_MB * 1024 * 1024)


def _dot(a, b, dims, precise):
    dn = (dims, ((), ()))
    if precise:
        return lax.dot_general(a.astype(F32), b.astype(F32), dn, preferred_element_type=F32,
                               precision=lax.Precision.HIGHEST)
    return lax.dot_general(a.astype(BF16), b.astype(BF16), dn, preferred_element_type=F32)


def _mm(a, b, precise=False):
    return _dot(a, b, ((1,), (0,)), precise)


def _mm_nt(a, b, precise=False):
    return _dot(a, b, ((1,), (1,)), precise)


def _mm_tn(a, b, precise=False):
    return _dot(a, b, ((0,), (0,)), precise)


def _rb(x, precise=False):
    return x.astype(F32) if precise else x.astype(BF16).astype(F32)


def _sigmoid(z):
    return 1.0 / (1.0 + jnp.exp(-z))


def _log_sigmoid(z):
    return jnp.minimum(z, 0.0) - jnp.log1p(jnp.exp(-jnp.abs(z)))


def _norm_matmul_kernel(x_ref, g_ref, w_ref, b_ref, o_ref, xn_ref, *, act, precise):
    @pl.when(pl.program_id(1) == 0)
    def _():
        xf = x_ref[...]
        r = lax.rsqrt(jnp.mean(xf * xf, axis=-1, keepdims=True) + EPS)
        xn_ref[...] = ((xf * r) * g_ref[...]).astype(xn_ref.dtype)

    acc = _mm(xn_ref[...], w_ref[...], precise)
    if act == "sigmoid":
        acc = _sigmoid(acc)
    elif act == "gates_tail":
        @pl.when(pl.program_id(1) == 0)
        def _():
            z = acc + b_ref[...]
            lane = lax.broadcasted_iota(jnp.int32, z.shape, 1)
            o_ref[...] = jnp.where(lane < ML_HEADS, z, _log_sigmoid(z)).astype(o_ref.dtype)

        @pl.when(pl.program_id(1) != 0)
        def _():
            o_ref[...] = _sigmoid(acc).astype(o_ref.dtype)
        return
    o_ref[...] = acc.astype(o_ref.dtype)


def _norm_matmul(x, g, w, bias, *, li, col_blk0, col_stride=1, n_out, tm, tn, out_dtype, act, precise, name):
    m, d = x.shape
    w_spec = pl.BlockSpec((None, d, tn), lambda i, j: (li, 0, col_blk0 + j * col_stride))
    return pl.pallas_call(
        functools.partial(_norm_matmul_kernel, act=act, precise=precise),
        grid=(m // tm, n_out // tn),
        in_specs=[
            pl.BlockSpec((tm, d), lambda i, j: (i, 0)),
            pl.BlockSpec((1, d), lambda i, j: (0, 0)),
            w_spec,
            pl.BlockSpec((1, tn), lambda i, j: (0, j)),
        ],
        out_specs=pl.BlockSpec((tm, tn), lambda i, j: (i, j)),
        out_shape=jax.ShapeDtypeStruct((m, n_out), out_dtype),
        scratch_shapes=[pltpu.VMEM((tm, d), F32 if precise else BF16)],
        compiler_params=_cparams(("parallel", "arbitrary")),
        name=name,
    )(x, g, w, bias)


def _merge_kernel(att_ref, hg_ref, sa_ref, sb_ref, wpa_ref, wpm_ref, o_ref, *, precise):
    a = _mm(att_ref[...], wpa_ref[...], precise)
    m = _mm(hg_ref[...], wpm_ref[...], precise)
    o_ref[...] = (sa_ref[...].astype(F32) * a + sb_ref[...].astype(F32) * m).astype(o_ref.dtype)


def _merge(att, hg, sg, w_pa, w_pm, *, li, tm, tn, out_dtype, precise, name):
    m = att.shape[0]
    nb = D_MODEL // tn
    return pl.pallas_call(
        functools.partial(_merge_kernel, precise=precise),
        grid=(m // tm, nb),
        in_specs=[
            pl.BlockSpec((tm, ATT_WIDTH), lambda i, j: (i, 0)),
            pl.BlockSpec((tm, ML_HEADS * ML_V), lambda i, j: (i, 0)),
            pl.BlockSpec((tm, tn), lambda i, j: (i, j)),
            pl.BlockSpec((tm, tn), lambda i, j: (i, j + nb)),
            pl.BlockSpec((None, ATT_WIDTH, tn), lambda i, j: (li, 0, j)),
            pl.BlockSpec((None, ML_HEADS * ML_V, tn), lambda i, j: (li, 0, j)),
        ],
        out_specs=pl.BlockSpec((tm, tn), lambda i, j: (i, j)),
        out_shape=jax.ShapeDtypeStruct((m, D_MODEL), out_dtype),
        compiler_params=_cparams(("parallel", "arbitrary")),
        name=name,
    )(att, hg, sg, sg, w_pa, w_pm)


def _outproj_kernel(x_ref, mg_ref, w_ref, o_ref, *, precise):
    o_ref[...] = x_ref[...] + _mm(mg_ref[...], w_ref[...], precise)


def _outproj(x, mg, w_out, *, li, tm, tn, precise, name):
    m = x.shape[0]
    return pl.pallas_call(
        functools.partial(_outproj_kernel, precise=precise),
        grid=(m // tm, D_MODEL // tn),
        in_specs=[
            pl.BlockSpec((tm, tn), lambda i, j: (i, j)),
            pl.BlockSpec((tm, D_MODEL), lambda i, j: (i, 0)),
            pl.BlockSpec((None, D_MODEL, tn), lambda i, j: (li, 0, j)),
        ],
        out_specs=pl.BlockSpec((tm, tn), lambda i, j: (i, j)),
        out_shape=jax.ShapeDtypeStruct((m, D_MODEL), F32),
        compiler_params=_cparams(("parallel", "arbitrary")),
        name=name,
    )(x, mg, w_out)


def _rope_tables(pos):
    half = ROPE_DIM // 2
    inv = jnp.power(jnp.float32(ROPE_THETA), -jnp.arange(half, dtype=F32) / half)
    ang = pos.astype(F32)[:, None] * inv[None, :]
    n = pos.shape[0]
    cos_t = jnp.concatenate([jnp.cos(ang), jnp.cos(ang), jnp.ones((n, HEAD_DIM - ROPE_DIM), F32)], axis=-1)
    sin_t = jnp.concatenate([jnp.sin(ang), jnp.sin(ang), jnp.zeros((n, HEAD_DIM - ROPE_DIM), F32)], axis=-1)
    return cos_t, sin_t


def _rope_perm():
    half = ROPE_DIM // 2
    r = jnp.arange(HEAD_DIM)[:, None]
    c = jnp.arange(HEAD_DIM)[None, :]
    p = jnp.where((c < half) & (r == c + half), -1.0, 0.0) + jnp.where(
        (c >= half) & (c < ROPE_DIM) & (r == c - half), 1.0, 0.0)
    return p.astype(BF16)


ROPE_ROWS = 512


def _attn_prompt_kernel(cos_ref, sin_ref, perm_ref, *refs, seq):
    q_refs, k_refs, v_refs = refs[0:3], refs[3:6], refs[6:9]
    att_ref = refs[9]
    kc_refs, vc_refs = refs[10:13], refs[13:16]
    qs, ks, vs = refs[16:19]
    o_s, lse_s = refs[19:22], refs[22:25]
    perm = perm_ref[...]
    nk = ATT_GROUPS[0][0] // ATT_GROUPS[0][1]
    row = lax.broadcasted_iota(jnp.int32, (nk, nk), 0)
    col = lax.broadcasted_iota(jnp.int32, (nk, nk), 1)
    cur_ok = col <= row
    prev_ok = col >= row
    neg = jnp.float32(-jnp.inf)

    for g, (win, dil) in enumerate(ATT_GROUPS):
        assert win // dil == nk
        span = nk * dil
        shift = dil.bit_length() - 1

        def rope_body(c, carry, g=g):
            r0 = pl.multiple_of(c * ROPE_ROWS, ROPE_ROWS)
            cs = cos_ref[pl.ds(r0, ROPE_ROWS), :]
            sn = sin_ref[pl.ds(r0, ROPE_ROWS), :]
            qb = q_refs[g][0, pl.ds(r0, ROPE_ROWS), :]
            kb = k_refs[g][0, pl.ds(r0, ROPE_ROWS), :]
            qr = qb.astype(F32) * cs + jnp.dot(qb, perm, preferred_element_type=F32) * sn
            kr = kb.astype(F32) * cs + jnp.dot(kb, perm, preferred_element_type=F32) * sn
            qs[pl.ds(r0, ROPE_ROWS), :] = qr
            ks[pl.ds(r0, ROPE_ROWS), :] = kr
            vs[pl.ds(r0, ROPE_ROWS), :] = v_refs[g][0, pl.ds(r0, ROPE_ROWS), :].astype(F32)
            return carry

        lax.fori_loop(0, seq // ROPE_ROWS, rope_body, 0)
        keep = min(win, seq)
        kc_refs[g][0] = ks[seq - keep:seq, :]
        vc_refs[g][0] = vs[seq - keep:seq, :]

        def rows(start, dil=dil):
            if dil == 1:
                return pl.ds(pl.multiple_of(start, nk), nk)
            return pl.ds(start, nk, stride=dil)

        def blk_body(blk, carry, g=g, dil=dil, span=span, shift=shift, rows=rows):
            n = blk >> shift
            r = blk & (dil - 1)
            start = n * span + r
            has_prev = n > 0
            start_p = jnp.where(has_prev, start - span, start)
            q = qs[rows(start), :]
            k_c = ks[rows(start), :]
            k_p = ks[rows(start_p), :]
            v_c = vs[rows(start), :]
            v_p = vs[rows(start_p), :]
            s_c = jnp.where(cur_ok, _mm_nt(q, k_c) * ATT_SCALE, neg)
            no_prev = jnp.where(has_prev, jnp.float32(0.0), neg)
            s_p = jnp.where(prev_ok, _mm_nt(q, k_p) * ATT_SCALE + no_prev, neg)
            m = jnp.maximum(jnp.max(s_c, axis=-1, keepdims=True), jnp.max(s_p, axis=-1, keepdims=True))
            p_c = jnp.exp(s_c - m)
            p_p = jnp.exp(s_p - m)
            l = jnp.sum(p_c, axis=-1, keepdims=True) + jnp.sum(p_p, axis=-1, keepdims=True)
            acc = _mm(p_c, v_c) + _mm(p_p, v_p)
            o_s[g][rows(start), :] = acc * (1.0 / l)
            lse_s[g][rows(start), :] = jnp.broadcast_to(m + jnp.log(l), (nk, HEAD_DIM))
            return carry

        lax.fori_loop(0, seq // nk, blk_body, 0, unroll=2)

    def out_body(c, carry):
        sl = pl.ds(pl.multiple_of(c * nk, nk), nk)
        lse = [lse_s[g][sl, :] for g in range(N_ATT_GROUPS)]
        top = functools.reduce(jnp.maximum, lse)
        w = [jnp.exp(x - top) for x in lse]
        num = sum(w[g] * o_s[g][sl, :] for g in range(N_ATT_GROUPS))
        att_ref[0, sl, :] = (num / sum(w)).astype(att_ref.dtype)
        return carry

    lax.fori_loop(0, seq // nk, out_body, 0)


def _attn_prompt(u3, cos_t, sin_t, *, name):
    bsz, seq, _ = u3.shape
    perm = _rope_perm()
    in_specs = [
        pl.BlockSpec((seq, HEAD_DIM), lambda b, h: (0, 0)),
        pl.BlockSpec((seq, HEAD_DIM), lambda b, h: (0, 0)),
        pl.BlockSpec((HEAD_DIM, HEAD_DIM), lambda b, h: (0, 0)),
    ]
    for off in (O_AQ, O_AK, O_AV):
        for g in range(N_ATT_GROUPS):
            blk0 = off // HEAD_DIM + g * ATT_HEADS
            in_specs.append(pl.BlockSpec((1, seq, HEAD_DIM), lambda b, h, blk0=blk0: (b, 0, blk0 + h)))
    keeps = [min(w, seq) for w, _ in ATT_GROUPS]
    out_specs = [pl.BlockSpec((1, seq, HEAD_DIM), lambda b, h: (b, 0, h))]
    out_shape = [jax.ShapeDtypeStruct((bsz, seq, ATT_WIDTH), BF16)]
    for _ in range(2):
        for keep in keeps:
            out_specs.append(pl.BlockSpec((1, keep, HEAD_DIM), lambda b, h: (b, 0, h)))
            out_shape.append(jax.ShapeDtypeStruct((bsz, keep, ATT_WIDTH), F32))
    outs = pl.pallas_call(
        functools.partial(_attn_prompt_kernel, seq=seq),
        grid=(bsz, ATT_HEADS),
        in_specs=in_specs,
        out_specs=out_specs,
        out_shape=out_shape,
        scratch_shapes=[pltpu.VMEM((seq, HEAD_DIM), F32) for _ in range(3 + 2 * N_ATT_GROUPS)],
        compiler_params=_cparams(("parallel", "arbitrary")),
        name=name,
    )(cos_t, sin_t, perm, *([u3] * 9))
    return outs[0], outs[1:4], outs[4:7]


def _attn_sample_kernel(cos_ref, sin_ref, q_ref, k_ref, v_ref, c0_ref, c1_ref, c2_ref,
                        att_ref, kn_ref, *, n_new):
    cache_refs = (c0_ref, c1_ref, c2_ref)
    cs = cos_ref[...]
    sn = sin_ref[...]
    rows_pad = q_ref.shape[1]
    half = ROPE_DIM // 2
    lane = lax.broadcasted_iota(jnp.int32, (rows_pad, HEAD_DIM), 1)

    def rot_half(x):
        return jnp.where(lane < half, -pltpu.roll(x, HEAD_DIM - half, axis=1),
                         jnp.where(lane < ROPE_DIM, pltpu.roll(x, half, axis=1), 0.0))

    nk = ATT_GROUPS[0][0] // ATT_GROUPS[0][1]
    kv_rows = 2 * ATT_HEADS
    w_iota = lax.broadcasted_iota(jnp.int32, (nk, 1), 0)
    u_iota = lax.broadcasted_iota(jnp.int32, (rows_pad, 1), 0)
    neg = jnp.float32(-jnp.inf)
    att_ref[...] = jnp.zeros(att_ref.shape, att_ref.dtype)

    for h in range(ATT_HEADS):
        parts = [[None] * N_ATT_GROUPS for _ in range(n_new)]
        for g, (win, dil) in enumerate(ATT_GROUPS):
            hs = (g * ATT_HEADS + h) * HEAD_DIM
            qh = q_ref[0, :, hs:hs + HEAD_DIM]
            kh = k_ref[0, :, hs:hs + HEAD_DIM]
            vh = v_ref[0, :, hs:hs + HEAD_DIM]
            qr = _rb(qh * cs + rot_half(qh) * sn)
            kr = kh * cs + rot_half(kh) * sn
            kn_ref[0, :, hs:hs + HEAD_DIM] = kr
            kr = _rb(kr)
            vh = _rb(vh)
            cref = cache_refs[g]
            for t in range(n_new):
                q_t = qr[t:t + 1, :]
                if dil == 1:
                    p0 = 0
                else:
                    p0 = t
                k_c = _rb(cref[pl.ds(p0 * kv_rows + h, nk, stride=kv_rows * dil), :])
                v_c = _rb(cref[pl.ds(p0 * kv_rows + ATT_HEADS + h, nk, stride=kv_rows * dil), :])
                s_c = jnp.sum(k_c * q_t, axis=-1, keepdims=True) * ATT_SCALE
                s_n = jnp.sum(kr * q_t, axis=-1, keepdims=True) * ATT_SCALE
                if dil == 1:
                    s_c = jnp.where(w_iota >= t, s_c, neg)
                    s_n = jnp.where(u_iota <= t, s_n, neg)
                else:
                    s_n = jnp.where(u_iota == t, s_n, neg)
                m = jnp.maximum(jnp.max(s_c, axis=0, keepdims=True), jnp.max(s_n, axis=0, keepdims=True))
                p_c = jnp.exp(s_c - m)
                p_n = jnp.exp(s_n - m)
                l = jnp.sum(p_c, axis=0, keepdims=True) + jnp.sum(p_n, axis=0, keepdims=True)
                acc = jnp.sum(_rb(p_c) * v_c, axis=0, keepdims=True) + jnp.sum(_rb(p_n) * vh, axis=0, keepdims=True)
                parts[t][g] = (m, l, acc)
        for t in range(n_new):
            m_all = functools.reduce(jnp.maximum, [p[0] for p in parts[t]])
            num = sum(jnp.exp(p[0] - m_all) * p[2] for p in parts[t])
            den = sum(jnp.exp(p[0] - m_all) * p[1] for p in parts[t])
            att_ref[0, t:t + 1, h * HEAD_DIM:(h + 1) * HEAD_DIM] = num / den


def _attn_sample(u3, caches2d, cos_t, sin_t, *, li, n_new, name):
    bsz, rows_pad, _ = u3.shape
    qkv_w = N_ATT_GROUPS * ATT_WIDTH
    in_specs = [
        pl.BlockSpec((rows_pad, HEAD_DIM), lambda b: (0, 0)),
        pl.BlockSpec((rows_pad, HEAD_DIM), lambda b: (0, 0)),
        pl.BlockSpec((1, rows_pad, qkv_w), lambda b: (b, 0, O_AQ // qkv_w)),
        pl.BlockSpec((1, rows_pad, qkv_w), lambda b: (b, 0, O_AK // qkv_w)),
        pl.BlockSpec((1, rows_pad, qkv_w), lambda b: (b, 0, O_AV // qkv_w)),
    ]
    for (win, _), c in zip(ATT_GROUPS, caches2d):
        rows = c.shape[0] // (DEPTH * bsz)
        in_specs.append(pl.BlockSpec((rows, HEAD_DIM), lambda b, bsz=bsz: (li * bsz + b, 0)))
    return pl.pallas_call(
        functools.partial(_attn_sample_kernel, n_new=n_new),
        grid=(bsz,),
        in_specs=in_specs,
        out_specs=[
            pl.BlockSpec((1, rows_pad, ATT_WIDTH), lambda b: (b, 0, 0)),
            pl.BlockSpec((1, rows_pad, qkv_w), lambda b: (b, 0, 0)),
        ],
        out_shape=[
            jax.ShapeDtypeStruct((bsz, rows_pad, ATT_WIDTH), F32),
            jax.ShapeDtypeStruct((bsz, rows_pad, qkv_w), F32),
        ],
        compiler_params=_cparams(("parallel",)),
        name=name,
    )(cos_t, sin_t, u3, u3, u3, *caches2d)


def _cache_shift_kernel(cache_hbm, fresh_hbm, out_hbm, sem):
    depth, bsz, rows = cache_hbm.shape[:3]
    n_new = fresh_hbm.shape[2]
    copies = []
    for li in range(depth):
        for b in range(bsz):
            copies.append(pltpu.make_async_copy(cache_hbm.at[li, b, pl.ds(n_new, rows - n_new)],
                                                out_hbm.at[li, b, pl.ds(0, rows - n_new)], sem))
            copies.append(pltpu.make_async_copy(fresh_hbm.at[li, b],
                                                out_hbm.at[li, b, pl.ds(rows - n_new, n_new)], sem))
    for c in copies:
        c.start()
    for c in copies:
        c.wait()


def _cache_shift(cache, fresh, *, name):
    any_spec = pl.BlockSpec(memory_space=pl.ANY)
    return pl.pallas_call(
        _cache_shift_kernel,
        in_specs=[any_spec, any_spec],
        out_specs=any_spec,
        out_shape=jax.ShapeDtypeStruct(cache.shape, cache.dtype),
        scratch_shapes=[pltpu.SemaphoreType.DMA(())],
        name=name,
    )(cache, fresh)


ML_HPB = 2
ML_VPB = 2


def _mlstm_kernel(q_ref, k_ref, *refs, chunk, precise):
    nvb = ML_HPB // ML_VPB
    v_refs, mo_refs = refs[:nvb], refs[nvb:2 * nvb]
    gc_ref, gr_ref, mlg_ref, c0_ref, n0_ref, m0_ref, h_ref, c_out, n_out, m_out, ct_s, n_s, m_s = refs[2 * nvb:]
    s_idx = pl.program_id(2)
    n_s_blocks = pl.num_programs(2)
    sb = q_ref.shape[1]
    L = chunk

    @pl.when(s_idx == 0)
    def _():
        for hh in range(ML_HPB):
            ct_s[hh] = c0_ref[0, hh].T
            n_s[hh] = n0_ref[0, 0, hh:hh + 1, :]
            m_s[hh] = m0_ref[0, 0, hh:hh + 1, :]

    row = lax.broadcasted_iota(jnp.int32, (L, L), 0)
    col = lax.broadcasted_iota(jnp.int32, (L, L), 1)
    causal = row >= col
    neg = jnp.float32(-jnp.inf)

    def chunk_body(c, carry):
        r0 = pl.multiple_of(c * L, L)
        gc = gc_ref[0, 0, pl.ds(r0, L), :]
        gr = gr_ref[0, 0, :, pl.ds(r0, L)]
        for hh in range(ML_HPB):
            ig_col = gc[:, hh:hh + 1]
            lf_col = gc[:, ML_HPB + hh:ML_HPB + hh + 1]
            ig_row = gr[hh:hh + 1, :]
            lf_row = gr[ML_HPB + hh:ML_HPB + hh + 1, :]
            b_col = jnp.sum(jnp.where(causal, lf_row, 0.0), axis=1, keepdims=True)
            b_row = jnp.sum(jnp.where(row <= col, lf_col, 0.0), axis=0, keepdims=True)
            qf = q_ref[0, pl.ds(r0, L), hh * ML_QK:(hh + 1) * ML_QK]
            kf = k_ref[0, pl.ds(r0, L), hh * ML_QK:(hh + 1) * ML_QK]
            vcols = slice((hh % ML_VPB) * ML_V, (hh % ML_VPB + 1) * ML_V)
            vf = v_refs[hh // ML_VPB][0, pl.ds(r0, L), vcols]
            mo = mo_refs[hh // ML_VPB][0, pl.ds(r0, L), vcols].astype(F32)
            ct = ct_s[hh]
            n_row = n_s[hh]
            m_prev = m_s[hh][:, 0:1]
            inter = b_col + m_prev
            dm = jnp.where(causal, b_col - b_row + ig_row, neg)
            m_t = jnp.maximum(inter, jnp.max(dm, axis=1, keepdims=True))
            w_intra = jnp.exp(dm - m_t)
            w_inter = jnp.exp(inter - m_t)
            ks = kf.astype(F32) * ML_K_SCALE
            sw = _mm_nt(qf, ks, precise) * w_intra
            num = w_inter * _mm(qf, ct, precise) + _mm(sw, vf, precise)
            den = w_inter * jnp.sum(_rb(qf, precise) * _rb(n_row, precise), axis=1, keepdims=True) + jnp.sum(
                sw, axis=1, keepdims=True)
            hv = num / jnp.maximum(jnp.abs(den), jnp.exp(-m_t))
            hn = hv * lax.rsqrt(jnp.mean(hv * hv, axis=-1, keepdims=True) + EPS)
            hn = hn * mlg_ref[:, hh * ML_V:(hh + 1) * ML_V]
            h_ref[0, pl.ds(r0, L), hh * ML_V:(hh + 1) * ML_V] = (hn * _sigmoid(mo)).astype(h_ref.dtype)
            b_last = b_col[L - 1:L, :]
            m_new = m_t[L - 1:L, :]
            decay = jnp.exp(b_last + m_prev - m_new)
            wk_col = jnp.exp(b_last - b_col + ig_col - m_new)
            ct_s[hh] = decay * ct + _mm_tn(ks * wk_col, vf, precise)
            n_s[hh] = decay * n_row + jnp.sum(_rb(ks, precise) * _rb(wk_col, precise), axis=0, keepdims=True)
            m_s[hh] = jnp.broadcast_to(m_new, (1, LANES))
        return carry

    lax.fori_loop(0, sb // L, chunk_body, 0)

    @pl.when(s_idx == n_s_blocks - 1)
    def _():
        for hh in range(ML_HPB):
            c_out[0, hh] = ct_s[hh].T
            n_out[0, 0, hh:hh + 1, :] = n_s[hh]
            m_out[0, 0, hh:hh + 1, :] = m_s[hh]


def _mlstm(u3, gc, gr, mlg, c0, n0, m0, *, chunk, sb, out_dtype, precise, name):
    bsz, seq, _ = u3.shape
    hg_n = ML_HEADS // ML_HPB
    qk_w, v_w, vb_w = ML_HPB * ML_QK, ML_HPB * ML_V, ML_VPB * ML_V
    nvb = ML_HPB // ML_VPB
    v_specs = [pl.BlockSpec((1, sb, vb_w), lambda b, hg, s, j=j, o=off // vb_w: (b, s, o + hg * nvb + j))
               for off in (O_MV, O_MO) for j in range(nvb)]
    n0r = n0.reshape(bsz, hg_n, ML_HPB, ML_QK)
    m0r = jnp.broadcast_to(m0.reshape(bsz, hg_n, ML_HPB, 1), (bsz, hg_n, ML_HPB, LANES))
    state_spec = pl.BlockSpec((1, 1, ML_HPB, LANES), lambda b, hg, s: (b, hg, 0, 0))
    c_spec = pl.BlockSpec((1, ML_HPB, ML_V, ML_QK), lambda b, hg, s: (b, hg, 0, 0))
    h, c_new, n_new, m_new = pl.pallas_call(
        functools.partial(_mlstm_kernel, chunk=chunk, precise=precise),
        grid=(bsz, hg_n, seq // sb),
        in_specs=[
            pl.BlockSpec((1, sb, qk_w), lambda b, hg, s: (b, s, O_MQ // qk_w + hg)),
            pl.BlockSpec((1, sb, qk_w), lambda b, hg, s: (b, s, O_MK // qk_w + hg)),
            *v_specs,
            pl.BlockSpec((1, 1, sb, 2 * ML_HPB), lambda b, hg, s: (b, hg, s, 0)),
            pl.BlockSpec((1, 1, 2 * ML_HPB, sb), lambda b, hg, s: (b, hg, 0, s)),
            pl.BlockSpec((1, v_w), lambda b, hg, s: (0, hg)),
            c_spec, state_spec, state_spec,
        ],
        out_specs=[
            pl.BlockSpec((1, sb, v_w), lambda b, hg, s: (b, s, hg)),
            c_spec, state_spec, state_spec,
        ],
        out_shape=[
            jax.ShapeDtypeStruct((bsz, seq, ML_HEADS * ML_V), out_dtype),
            jax.ShapeDtypeStruct((bsz, ML_HEADS, ML_V, ML_QK), F32),
            jax.ShapeDtypeStruct((bsz, hg_n, ML_HPB, LANES), F32),
            jax.ShapeDtypeStruct((bsz, hg_n, ML_HPB, LANES), F32),
        ],
        scratch_shapes=[
            pltpu.VMEM((ML_HPB, ML_QK, ML_V), F32),
            pltpu.VMEM((ML_HPB, 1, ML_QK), F32),
            pltpu.VMEM((ML_HPB, 1, LANES), F32),
        ],
        compiler_params=_cparams(("parallel", "parallel", "arbitrary")),
        name=name,
    )(u3, u3, *([u3] * (2 * nvb)), gc, gr, mlg, c0, n0r, m0r)
    return h, c_new, n_new.reshape(bsz, ML_HEADS, ML_QK), m_new[..., 0].reshape(bsz, ML_HEADS)


def _gate_layouts(gates, bsz, seq, seq_pad):
    hg_n = ML_HEADS // ML_HPB
    ig = gates[:, :ML_HEADS].reshape(bsz, seq, ML_HEADS)
    lf = gates[:, ML_HEADS:2 * ML_HEADS].reshape(bsz, seq, ML_HEADS)
    if seq_pad > seq:
        ig = jnp.pad(ig, ((0, 0), (0, seq_pad - seq), (0, 0)), constant_values=-jnp.inf)
        lf = jnp.pad(lf, ((0, 0), (0, seq_pad - seq), (0, 0)))
    ig = ig.reshape(bsz, seq_pad, hg_n, ML_HPB).transpose(0, 2, 1, 3)
    lf = lf.reshape(bsz, seq_pad, hg_n, ML_HPB).transpose(0, 2, 1, 3)
    gc = jnp.concatenate([ig, lf], axis=-1)
    return gc, gc.transpose(0, 1, 3, 2)


def _router_kernel(x_ref, g_ref, w_ref, b_ref, xn_ref, route_ref, *, precise):
    xf = x_ref[...]
    r = lax.rsqrt(jnp.mean(xf * xf, axis=-1, keepdims=True) + EPS)
    xn = (xf * r) * g_ref[...]
    xn_ref[...] = xn.astype(xn_ref.dtype)
    logits = _mm(xn, w_ref[...], precise) + b_ref[...]
    lane = lax.broadcasted_iota(jnp.int32, logits.shape, 1).astype(F32)
    neg = jnp.float32(-jnp.inf)
    big = jnp.float32(LANES)
    gl = jnp.where(lane < N_GROUPS, logits, neg)
    g_max = jnp.max(gl, axis=-1, keepdims=True)
    g_val = 1.0 / jnp.sum(jnp.exp(gl - g_max), axis=-1, keepdims=True)
    g_idx = jnp.min(jnp.where(gl == g_max, lane, big), axis=-1, keepdims=True)
    lo = N_GROUPS + EXP_PER_GROUP * g_idx
    es = jnp.where((lane >= lo) & (lane < lo + EXP_PER_GROUP), logits, neg)
    t0 = jnp.max(es, axis=-1, keepdims=True)
    i0 = jnp.min(jnp.where(es == t0, lane, big), axis=-1, keepdims=True)
    es1 = jnp.where(lane == i0, neg, es)
    t1 = jnp.max(es1, axis=-1, keepdims=True)
    i1 = jnp.min(jnp.where(es1 == t1, lane, big), axis=-1, keepdims=True)
    e1 = jnp.exp(t1 - t0)
    w0 = g_val / (1.0 + e1)
    w1 = g_val * e1 / (1.0 + e1)
    route = jnp.where(lane == 0, i0 - N_GROUPS,
                      jnp.where(lane == 1, i1 - N_GROUPS,
                                jnp.where(lane == 2, w0, jnp.where(lane == 3, w1, 0.0))))
    route_ref[...] = route


def _router(x, g, w_r, b_r, *, tm, precise, name):
    m, d = x.shape
    return pl.pallas_call(
        functools.partial(_router_kernel, precise=precise),
        grid=(m // tm,),
        in_specs=[
            pl.BlockSpec((tm, d), lambda i: (i, 0)),
            pl.BlockSpec((1, d), lambda i: (0, 0)),
            pl.BlockSpec((d, LANES), lambda i: (0, 0)),
            pl.BlockSpec((1, LANES), lambda i: (0, 0)),
        ],
        out_specs=[pl.BlockSpec((tm, d), lambda i: (i, 0)), pl.BlockSpec((tm, LANES), lambda i: (i, 0))],
        out_shape=[jax.ShapeDtypeStruct((m, d), F32), jax.ShapeDtypeStruct((m, LANES), F32)],
        compiler_params=_cparams(("parallel",)),
        name=name,
    )(x, g, w_r, b_r)


def _ffn_kernel(te_ref, nv_ref, xs_ref, wr_ref, wg_ref, wu_ref, wd_ref, y_ref, wg_s, wu_s, wd_s, *, precise):
    i = pl.program_id(0)
    valid = i < nv_ref[0]
    prev = te_ref[jnp.maximum(i - 1, 0)]

    @pl.when(valid & ((i == 0) | (te_ref[i] != prev)))
    def _():
        wg_s[...] = wg_ref[...].astype(wg_s.dtype)
        wu_s[...] = wu_ref[...].astype(wu_s.dtype)
        wd_s[...] = wd_ref[...].astype(wd_s.dtype)

    @pl.when(valid)
    def _():
        xs = xs_ref[...]
        gt = _mm(xs, wg_s[...], precise)
        up = _mm(xs, wu_s[...], precise)
        y_ref[...] = _mm((gt * _sigmoid(gt) * up) * wr_ref[...], wd_s[...], precise)


def _ffn(xs, wrow, te, nv, w_gate, w_up, w_down, *, li, tm, precise, name):
    rows, d = xs.shape
    n_tiles = rows // tm
    f = EXPERT_FF
    wdt = F32 if precise else BF16

    def row_blk(i, te, nv):
        return (jnp.minimum(i, nv[0] - 1), 0)

    return pl.pallas_call(
        functools.partial(_ffn_kernel, precise=precise),
        grid_spec=pltpu.PrefetchScalarGridSpec(
            num_scalar_prefetch=2,
            grid=(n_tiles,),
            in_specs=[
                pl.BlockSpec((tm, d), row_blk),
                pl.BlockSpec((tm, 1), row_blk),
                pl.BlockSpec((None, d, f), lambda i, te, nv: (li, 0, te[i])),
                pl.BlockSpec((None, d, f), lambda i, te, nv: (li, 0, te[i])),
                pl.BlockSpec((None, f, d), lambda i, te, nv: (li, te[i], 0)),
            ],
            out_specs=pl.BlockSpec((tm, d), row_blk),
            scratch_shapes=[pltpu.VMEM((d, f), wdt), pltpu.VMEM((d, f), wdt), pltpu.VMEM((f, d), wdt)],
        ),
        out_shape=jax.ShapeDtypeStruct((rows, d), F32),
        compiler_params=_cparams(("arbitrary",)),
        name=name,
    )(te, nv, xs, wrow, w_gate, w_up, w_down)


def _rank_kernel(route_ref, offs_ref, dest_ref, run_s):
    @pl.when(pl.program_id(0) == 0)
    def _():
        run_s[...] = jnp.zeros(run_s.shape, run_s.dtype)

    route = route_ref[...]
    tm = route.shape[0]
    lane = lax.broadcasted_iota(jnp.int32, route.shape, 1).astype(F32)
    is0 = lane == route[:, 0:1]
    is1 = lane == route[:, 1:2]
    onehot = jnp.where(is0, 1.0, 0.0) + jnp.where(is1, 1.0, 0.0)
    r = lax.broadcasted_iota(jnp.int32, (tm, tm), 0)
    c = lax.broadcasted_iota(jnp.int32, (tm, tm), 1)
    earlier = jnp.where(c < r, 1.0, 0.0)
    before = _mm(earlier, onehot)
    base = offs_ref[...] + run_s[...] + before
    d0 = jnp.sum(jnp.where(is0, base, 0.0), axis=-1, keepdims=True)
    d1 = jnp.sum(jnp.where(is1, base, 0.0), axis=-1, keepdims=True)
    run_s[...] = run_s[...] + jnp.sum(onehot, axis=0, keepdims=True)
    dest_ref[...] = jnp.where(lane == 0.0, d0, jnp.where(lane == 1.0, d1, 0.0)).astype(jnp.int32)


def _rank(route, offs_row, *, tm, name):
    m = route.shape[0]
    return pl.pallas_call(
        _rank_kernel,
        grid=(m // tm,),
        in_specs=[pl.BlockSpec((tm, LANES), lambda i: (i, 0)), pl.BlockSpec((1, LANES), lambda i: (0, 0))],
        out_specs=pl.BlockSpec((tm, LANES), lambda i: (i, 0)),
        out_shape=jax.ShapeDtypeStruct((m, LANES), jnp.int32),
        scratch_shapes=[pltpu.VMEM((1, LANES), F32)],
        compiler_params=_cparams(("arbitrary",)),
        name=name,
    )(route, offs_row)


ROW_DMA_UNROLL = 8


def _row_copy(src, src_row, dst, dst_row, sem):
    return pltpu.make_async_copy(src.at[pl.ds(src_row, 1)], dst.at[pl.ds(dst_row, 1)], sem)


def _scatter_kernel(dest_ref, last_ref, xn_ref, xs_hbm, zero_s, sem, zsem, *, tile):
    i = pl.program_id(0)
    tm = xn_ref.shape[0]

    def zero_copy(e):
        return pltpu.make_async_copy(zero_s, xs_hbm.at[pl.ds(pl.multiple_of(last_ref[e], tile), tile)], zsem)

    @pl.when(i == 0)
    def _():
        zero_s[...] = jnp.zeros(zero_s.shape, zero_s.dtype)

        def z_start(e, carry):
            @pl.when(last_ref[e] >= 0)
            def _():
                zero_copy(e).start()
            return carry

        def z_wait(e, carry):
            @pl.when(last_ref[e] >= 0)
            def _():
                zero_copy(e).wait()
            return carry

        lax.fori_loop(0, N_EXPERTS, z_start, 0)
        lax.fori_loop(0, N_EXPERTS, z_wait, 0)

    base = i * (2 * tm)

    def start(r, carry):
        _row_copy(xn_ref, r, xs_hbm, dest_ref[base + 2 * r], sem).start()
        _row_copy(xn_ref, r, xs_hbm, dest_ref[base + 2 * r + 1], sem).start()
        return carry

    def wait(r, carry):
        _row_copy(xn_ref, r, xs_hbm, dest_ref[base + 2 * r], sem).wait()
        _row_copy(xn_ref, r, xs_hbm, dest_ref[base + 2 * r + 1], sem).wait()
        return carry

    lax.fori_loop(0, tm, start, 0, unroll=ROW_DMA_UNROLL)
    lax.fori_loop(0, tm, wait, 0, unroll=ROW_DMA_UNROLL)


def _scatter(dest, last_tile, xn, *, rows, tm, tile, name):
    m, d = xn.shape
    return pl.pallas_call(
        functools.partial(_scatter_kernel, tile=tile),
        grid_spec=pltpu.PrefetchScalarGridSpec(
            num_scalar_prefetch=2,
            grid=(m // tm,),
            in_specs=[pl.BlockSpec((tm, d), lambda i, dest, last: (i, 0))],
            out_specs=pl.BlockSpec(memory_space=pl.ANY),
            scratch_shapes=[pltpu.VMEM((tile, d), F32), pltpu.SemaphoreType.DMA(()), pltpu.SemaphoreType.DMA(())],
        ),
        out_shape=jax.ShapeDtypeStruct((rows, d), F32),
        compiler_params=_cparams(("arbitrary",)),
        name=name,
    )(dest, last_tile, xn)


def _combine_kernel(dest_ref, x_ref, y_hbm, o_ref, ya, yb, sem):
    i = pl.program_id(0)
    tm = x_ref.shape[0]
    base = i * (2 * tm)

    def start(r, carry):
        _row_copy(y_hbm, dest_ref[base + 2 * r], ya, r, sem).start()
        _row_copy(y_hbm, dest_ref[base + 2 * r + 1], yb, r, sem).start()
        return carry

    def wait(r, carry):
        _row_copy(y_hbm, dest_ref[base + 2 * r], ya, r, sem).wait()
        _row_copy(y_hbm, dest_ref[base + 2 * r + 1], yb, r, sem).wait()
        return carry

    lax.fori_loop(0, tm, start, 0, unroll=ROW_DMA_UNROLL)
    lax.fori_loop(0, tm, wait, 0, unroll=ROW_DMA_UNROLL)
    o_ref[...] = x_ref[...] + (ya[...] + yb[...])


def _combine(dest, x, y, *, tm, name):
    m, d = x.shape
    return pl.pallas_call(
        _combine_kernel,
        grid_spec=pltpu.PrefetchScalarGridSpec(
            num_scalar_prefetch=1,
            grid=(m // tm,),
            in_specs=[
                pl.BlockSpec((tm, d), lambda i, dest: (i, 0)),
                pl.BlockSpec(memory_space=pl.ANY),
            ],
            out_specs=pl.BlockSpec((tm, d), lambda i, dest: (i, 0)),
            scratch_shapes=[pltpu.VMEM((tm, d), F32), pltpu.VMEM((tm, d), F32), pltpu.SemaphoreType.DMA(())],
        ),
        out_shape=jax.ShapeDtypeStruct((m, d), F32),
        compiler_params=_cparams(("arbitrary",)),
        name=name,
    )(dest, x, y)


def _moe(x, g, w_r, b_r, w_gate, w_up, w_down, *, li, tm_tok, tm_ffn, precise, name):
    m = x.shape[0]
    xn, route = _router(x, g, w_r, b_r, tm=tm_tok, precise=precise, name=name + "_router")
    eid = route[:, 0:2].astype(jnp.int32)
    counts = jnp.sum((eid[:, :, None] == jnp.arange(N_EXPERTS, dtype=jnp.int32)).astype(jnp.int32), axis=(0, 1))
    padded = ((counts + tm_ffn - 1) // tm_ffn) * tm_ffn
    ends = jnp.cumsum(padded)
    offs = ends - padded
    rows = 2 * m + N_EXPERTS * tm_ffn
    tile_start = jnp.arange(rows // tm_ffn, dtype=jnp.int32) * tm_ffn
    te = jnp.minimum(jnp.sum((ends[None, :] <= tile_start[:, None]).astype(jnp.int32), axis=1), N_EXPERTS - 1)
    nv = (ends[-1] // tm_ffn).reshape(1)
    last_tile = jnp.where(padded > 0, ends - tm_ffn, -1)
    offs_row = jnp.pad(offs.astype(F32), (0, LANES - N_EXPERTS)).reshape(1, LANES)
    dest = _rank(route, offs_row, tm=tm_tok, name=name + "_rank")[:, 0:2].reshape(-1)
    xs = _scatter(dest, last_tile, xn, rows=rows, tm=tm_tok, tile=tm_ffn, name=name + "_scatter")
    wrow = jnp.zeros((rows, 1), F32).at[dest, 0].set(route[:, 2:4].reshape(-1))
    y = _ffn(xs, wrow, te, nv, w_gate, w_up, w_down, li=li, tm=tm_ffn, precise=precise, name=name + "_ffn")
    return _combine(dest, x, y, tm=tm_tok, name=name + "_combine")


def _rmsnorm_kernel(x_ref, g_ref, o_ref):
    xf = x_ref[...]
    r = lax.rsqrt(jnp.mean(xf * xf, axis=-1, keepdims=True) + EPS)
    o_ref[...] = (xf * r) * g_ref[...]


def _rmsnorm(x, g, *, tm, name):
    m, d = x.shape
    return pl.pallas_call(
        _rmsnorm_kernel,
        grid=(m // tm,),
        in_specs=[pl.BlockSpec((tm, d), lambda i: (i, 0)), pl.BlockSpec((1, d), lambda i: (0, 0))],
        out_specs=pl.BlockSpec((tm, d), lambda i: (i, 0)),
        out_shape=jax.ShapeDtypeStruct((m, d), F32),
        compiler_params=_cparams(("parallel",)),
        name=name,
    )(x, g)


def _layer_weights(p, li):
    b_if = jnp.pad(p["b_if"][li], (0, 2 * LANES - 2 * ML_HEADS)).reshape(1, 2 * LANES)
    w_r = jnp.pad(jnp.concatenate([p["w_rg"][li], p["w_re"][li]], axis=1),
                  ((0, 0), (0, LANES - N_GROUPS - N_EXPERTS)))
    b_r = jnp.pad(jnp.concatenate([p["b_rg"][li], p["b_re"][li]]), (0, LANES - N_GROUPS - N_EXPERTS)).reshape(1, LANES)
    return b_if, w_r, b_r


def _run_trunk(x3, p, layer_w, *, caches, c0s, n0s, m0s, pos0):
    bsz, seq, d = x3.shape
    m = bsz * seq
    x = x3.reshape(m, d)
    precise = False
    if caches is None:
        act_dtype, tag = BF16, "p"
        tm, tn, tm_tok, tm_ffn, seq_pad = 1024, 512, 256, 256, seq
    else:
        act_dtype, tag = F32, "s"
        tm, tn, tm_tok, tm_ffn, seq_pad = m, 512, m, 16, 16
    cos_t, sin_t = _rope_tables(pos0 + jnp.arange(seq_pad, dtype=jnp.int32))
    zeros_bias = jnp.zeros((1, N_MAIN), F32)
    new_kv = [[] for _ in ATT_GROUPS]
    c_all, n_all, m_all = [], [], []
    for li in range(DEPTH):
        b_if, w_r, b_r = layer_w[li]
        g_attn = p["attn_norm_g"][li].reshape(1, d)
        common = dict(li=li, tm=tm, precise=precise)
        u = _norm_matmul(x, g_attn, p["w_in"], zeros_bias, col_blk0=0, n_out=N_MAIN, tn=TN_MAIN,
                         out_dtype=act_dtype, act="none", name=f"{tag}{li}_inproj", **common)
        n_sg = 2 * D_MODEL
        sg_main = _norm_matmul(x, g_attn, p["w_in"], zeros_bias, col_blk0=N_MAIN // tn, n_out=n_sg, tn=tn,
                               out_dtype=act_dtype, act="sigmoid", name=f"{tag}{li}_gateproj", **common)
        tail_blk = (N_MAIN + n_sg) // LANES
        gt = _norm_matmul(x, g_attn, p["w_in"], b_if, col_blk0=N_MAIN // LANES, col_stride=tail_blk - N_MAIN // LANES,
                          n_out=2 * LANES, tn=LANES, out_dtype=F32, act="gates_tail", name=f"{tag}{li}_ifproj",
                          **common)
        gates = gt[:, :LANES]
        n_gate_cols = 2 * ML_HEADS
        sg = jnp.concatenate([sg_main[:, n_gate_cols:], gt[:, LANES:LANES + n_gate_cols].astype(act_dtype)], axis=1)
        u3 = u.reshape(bsz, seq, N_MAIN)
        if seq_pad > seq:
            u3 = jnp.pad(u3, ((0, 0), (0, seq_pad - seq), (0, 0)))
        if caches is None:
            att3, kcs, vcs = _attn_prompt(u3, cos_t, sin_t, name=f"{tag}{li}_attn")
            for gi in range(N_ATT_GROUPS):
                keep = kcs[gi].shape[1]
                new_kv[gi].append(jnp.stack([kcs[gi].reshape(bsz, keep, ATT_HEADS, HEAD_DIM),
                                             vcs[gi].reshape(bsz, keep, ATT_HEADS, HEAD_DIM)], axis=2))
            att = att3.reshape(m, ATT_WIDTH)
        else:
            caches2d = [c.reshape(-1, HEAD_DIM) for c in caches]
            att3, kn3 = _attn_sample(u3, caches2d, cos_t, sin_t, li=li, n_new=seq, name=f"{tag}{li}_attn")
            att = att3[:, :seq].reshape(m, ATT_WIDTH)
            nh = N_ATT_GROUPS * ATT_HEADS
            k_new = kn3[:, :seq].reshape(bsz, seq, nh, HEAD_DIM)
            v_new = u3[:, :seq, O_AV:O_AV + nh * HEAD_DIM].reshape(bsz, seq, nh, HEAD_DIM)
            for gi in range(N_ATT_GROUPS):
                sl = slice(gi * ATT_HEADS, (gi + 1) * ATT_HEADS)
                new_kv[gi].append(jnp.stack([k_new[:, :, sl], v_new[:, :, sl]], axis=2))
        gc, gr = _gate_layouts(gates, bsz, seq, seq_pad)
        mlg = p["ml_norm_g"][li].reshape(1, ML_HEADS * ML_V)
        hg3, c_new, n_new, m_new = _mlstm(
            u3, gc, gr, mlg, c0s[li], n0s[li], m0s[li], chunk=min(128, seq_pad), sb=min(512, seq_pad),
            out_dtype=act_dtype, precise=precise, name=f"{tag}{li}_mlstm")
        hg = hg3[:, :seq].reshape(m, ML_HEADS * ML_V)
        c_all.append(c_new)
        n_all.append(n_new)
        m_all.append(m_new)
        merged = _merge(att, hg, sg, p["w_pa"], p["w_pm"], li=li, tm=tm, tn=tn, out_dtype=act_dtype,
                        precise=precise, name=f"{tag}{li}_merge")
        x = _outproj(x, merged, p["w_out"], li=li, tm=tm, tn=tn, precise=precise, name=f"{tag}{li}_outproj")
        x = _moe(x, p["ffn_norm_g"][li].reshape(1, d), w_r, b_r, p["w_gate"], p["w_up"], p["w_down"],
                 li=li, tm_tok=tm_tok, tm_ffn=tm_ffn, precise=precise, name=f"{tag}{li}_moe")
    y = _rmsnorm(x, p["final_norm_g"].reshape(1, d), tm=min(m, 512), name=f"{tag}_final_norm")
    new_kv = [jnp.stack(a) for a in new_kv]
    if caches is not None:
        new_kv = [_cache_shift(c, fresh, name=f"s_cache_shift{gi}") for gi, (c, fresh) in enumerate(zip(caches, new_kv))]
    return (y.reshape(bsz, seq, d), new_kv, jnp.stack(c_all), jnp.stack(n_all), jnp.stack(m_all))


def kernel(x_prompt, x_sample, cache_kv_w128, cache_kv_w512, cache_kv_w2048, state_C, state_n, state_m,
           attn_norm_g, w_in, b_if, ml_norm_g, w_pa, w_pm, w_out, ffn_norm_g, w_rg, b_rg, w_re, b_re,
           w_gate, w_up, w_down, final_norm_g):
    p = dict(attn_norm_g=attn_norm_g, w_in=w_in, b_if=b_if, ml_norm_g=ml_norm_g, w_pa=w_pa, w_pm=w_pm,
             w_out=w_out, ffn_norm_g=ffn_norm_g, w_rg=w_rg, b_rg=b_rg, w_re=w_re, b_re=b_re,
             w_gate=w_gate, w_up=w_up, w_down=w_down, final_norm_g=final_norm_g)
    layer_w = [_layer_weights(p, li) for li in range(DEPTH)]
    bp = x_prompt.shape[0]
    c0 = jnp.zeros((DEPTH, bp, ML_HEADS, ML_V, ML_QK), F32)
    n0 = jnp.zeros((DEPTH, bp, ML_HEADS, ML_QK), F32)
    m0 = jnp.zeros((DEPTH, bp, ML_HEADS), F32)
    y_p, p_kv, p_c, p_n, p_m = _run_trunk(x_prompt, p, layer_w, caches=None,
                                          c0s=c0, n0s=n0, m0s=m0, pos0=0)
    y_s, s_kv, s_c, s_n, s_m = _run_trunk(x_sample, p, layer_w,
                                          caches=[cache_kv_w128, cache_kv_w512, cache_kv_w2048],
                                          c0s=state_C, n0s=state_n, m0s=state_m, pos0=PAST_LEN)
    return (y_p, y_s, p_kv[0], p_kv[1], p_kv[2], p_c, p_n, p_m,
            s_kv[0], s_kv[1], s_kv[2], s_c, s_n, s_m)
```

```python
import functools

import jax
import jax.numpy as jnp
from jax import lax
from jax.experimental import pallas as pl
from jax.experimental.pallas import tpu as pltpu

F32 = jnp.float32
BF16 = jnp.bfloat16

D_MODEL = 2048
DEPTH = 2
PAST_LEN = 16384
ATT_GROUPS = ((128, 1), (512, 4), (2048, 16))
N_ATT_GROUPS = 3
ATT_HEADS = 4
HEAD_DIM = 128
ATT_WIDTH = ATT_HEADS * HEAD_DIM
ATT_SCALE = HEAD_DIM ** -0.5
ROPE_DIM = HEAD_DIM // 4
ROPE_THETA = 500000.0
ML_HEADS = 8
ML_QK = 128
ML_V = 256
ML_K_SCALE = ML_QK ** -0.5
N_GROUPS = 4
EXP_PER_GROUP = 8
N_EXPERTS = N_GROUPS * EXP_PER_GROUP
EXPERT_FF = 256
EPS = 1e-6

O_AQ, O_AK, O_AV = 0, 1536, 3072
O_MQ, O_MK, O_MV, O_MO = 4608, 5632, 6656, 8704
O_MI, O_MF, O_GA, O_GB = 10752, 10760, 10768, 12816
N_MAIN = O_MI
LANES = 128
TN_MAIN = 768
VMEM_LIMIT_MB = 56


def _cparams(sem):
    return pltpu.CompilerParams(dimension_semantics=sem, vmem_limit_bytes=VMEM_LIMIT_MB * 1024 * 1024)


def _dot(a, b, dims, precise):
    dn = (dims, ((), ()))
    if precise:
        return lax.dot_general(a.astype(F32), b.astype(F32), dn, preferred_element_type=F32,
                               precision=lax.Precision.HIGHEST)
    return lax.dot_general(a.astype(BF16), b.astype(BF16), dn, preferred_element_type=F32)


def _mm(a, b, precise=False):
    return _dot(a, b, ((1,), (0,)), precise)


def _mm_nt(a, b, precise=False):
    return _dot(a, b, ((1,), (1,)), precise)


def _mm_tn(a, b, precise=False):
    return _dot(a, b, ((0,), (0,)), precise)


def _rb(x, precise=False):
    return x.astype(F32) if precise else x.astype(BF16).astype(F32)


def _sigmoid(z):
    return 1.0 / (1.0 + jnp.exp(-z))


def _log_sigmoid(z):
    return jnp.minimum(z, 0.0) - jnp.log1p(jnp.exp(-jnp.abs(z)))


def _norm_matmul_kernel(x_ref, g_ref, w_ref, b_ref, o_ref, xn_ref, *, act, precise):
    @pl.when(pl.program_id(1) == 0)
    def _():
        xf = x_ref[...]
        r = lax.rsqrt(jnp.mean(xf * xf, axis=-1, keepdims=True) + EPS)
        xn_ref[...] = ((xf * r) * g_ref[...]).astype(xn_ref.dtype)

    acc = _mm_nt(xn_ref[...], w_ref[...], precise)
    if act == "sigmoid":
        acc = _sigmoid(acc)
    elif act == "gates_tail":
        @pl.when(pl.program_id(1) == 0)
        def _():
            z = acc + b_ref[...]
            lane = lax.broadcasted_iota(jnp.int32, z.shape, 1)
            o_ref[...] = jnp.where(lane < ML_HEADS, z, _log_sigmoid(z)).astype(o_ref.dtype)

        @pl.when(pl.program_id(1) != 0)
        def _():
            o_ref[...] = _sigmoid(acc).astype(o_ref.dtype)
        return
    o_ref[...] = acc.astype(o_ref.dtype)


def _norm_matmul(x, g, w, bias, *, li, col_blk0, col_stride=1, n_out, tm, tn, out_dtype, act, precise, name):
    m, d = x.shape
    w_spec = pl.BlockSpec((None, tn, d), lambda i, j: (li, col_blk0 + j * col_stride, 0))
    return pl.pallas_call(
        functools.partial(_norm_matmul_kernel, act=act, precise=precise),
        grid=(m // tm, n_out // tn),
        in_specs=[
            pl.BlockSpec((tm, d), lambda i, j: (i, 0)),
            pl.BlockSpec((1, d), lambda i, j: (0, 0)),
            w_spec,
            pl.BlockSpec((1, tn), lambda i, j: (0, j)),
        ],
        out_specs=pl.BlockSpec((tm, tn), lambda i, j: (i, j)),
        out_shape=jax.ShapeDtypeStruct((m, n_out), out_dtype),
        scratch_shapes=[pltpu.VMEM((tm, d), F32 if precise else BF16)],
        compiler_params=_cparams(("parallel", "arbitrary")),
        name=name,
    )(x, g, w, bias)


def _merge_kernel(att_ref, hg_ref, sa_ref, sb_ref, wpa_ref, wpm_ref, o_ref, *, precise):
    a = _mm(att_ref[...], wpa_ref[...], precise)
    m = _mm(hg_ref[...], wpm_ref[...], precise)
    o_ref[...] = (sa_ref[...].astype(F32) * a + sb_ref[...].astype(F32) * m).astype(o_ref.dtype)


def _merge(att, hg, sg, w_pa, w_pm, *, li, tm, tn, out_dtype, precise, name):
    m = att.shape[0]
    nb = D_MODEL // tn
    return pl.pallas_call(
        functools.partial(_merge_kernel, precise=precise),
        grid=(m // tm, nb),
        in_specs=[
            pl.BlockSpec((tm, ATT_WIDTH), lambda i, j: (i, 0)),
            pl.BlockSpec((tm, ML_HEADS * ML_V), lambda i, j: (i, 0)),
            pl.BlockSpec((tm, tn), lambda i, j: (i, j)),
            pl.BlockSpec((tm, tn), lambda i, j: (i, j + nb)),
            pl.BlockSpec((None, ATT_WIDTH, tn), lambda i, j: (li, 0, j)),
            pl.BlockSpec((None, ML_HEADS * ML_V, tn), lambda i, j: (li, 0, j)),
        ],
        out_specs=pl.BlockSpec((tm, tn), lambda i, j: (i, j)),
        out_shape=jax.ShapeDtypeStruct((m, D_MODEL), out_dtype),
        compiler_params=_cparams(("parallel", "arbitrary")),
        name=name,
    )(att, hg, sg, sg, w_pa, w_pm)


def _outproj_kernel(x_ref, mg_ref, w_ref, o_ref, *, precise):
    o_ref[...] = x_ref[...] + _mm(mg_ref[...], w_ref[...], precise)


def _outproj(x, mg, w_out, *, li, tm, tn, precise, name):
    m = x.shape[0]
    return pl.pallas_call(
        functools.partial(_outproj_kernel, precise=precise),
        grid=(m // tm, D_MODEL // tn),
        in_specs=[
            pl.BlockSpec((tm, tn), lambda i, j: (i, j)),
            pl.BlockSpec((tm, D_MODEL), lambda i, j: (i, 0)),
            pl.BlockSpec((None, D_MODEL, tn), lambda i, j: (li, 0, j)),
        ],
        out_specs=pl.BlockSpec((tm, tn), lambda i, j: (i, j)),
        out_shape=jax.ShapeDtypeStruct((m, D_MODEL), F32),
        compiler_params=_cparams(("parallel", "arbitrary")),
        name=name,
    )(x, mg, w_out)


def _rope_tables(pos):
    half = ROPE_DIM // 2
    inv = jnp.power(jnp.float32(ROPE_THETA), -jnp.arange(half, dtype=F32) / half)
    ang = pos.astype(F32)[:, None] * inv[None, :]
    n = pos.shape[0]
    cos_t = jnp.concatenate([jnp.cos(ang), jnp.cos(ang), jnp.ones((n, HEAD_DIM - ROPE_DIM), F32)], axis=-1)
    sin_t = jnp.concatenate([jnp.sin(ang), jnp.sin(ang), jnp.zeros((n, HEAD_DIM - ROPE_DIM), F32)], axis=-1)
    return cos_t, sin_t


def _rope_perm():
    half = ROPE_DIM // 2
    r = jnp.arange(HEAD_DIM)[:, None]
    c = jnp.arange(HEAD_DIM)[None, :]
    p = jnp.where((c < half) & (r == c + half), -1.0, 0.0) + jnp.where(
        (c >= half) & (c < ROPE_DIM) & (r == c - half), 1.0, 0.0)
    return p.astype(BF16)


ROPE_ROWS = 512


def _attn_prompt_kernel(cos_ref, sin_ref, perm_ref, *refs, seq):
    q_refs, k_refs, v_refs = refs[0:3], refs[3:6], refs[6:9]
    att_ref = refs[9]
    kc_refs, vc_refs = refs[10:13], refs[13:16]
    qs, ks, vs = refs[16:19]
    o_s, lse_s = refs[19:22], refs[22:25]
    perm = perm_ref[...]
    nk = ATT_GROUPS[0][0] // ATT_GROUPS[0][1]
    row = lax.broadcasted_iota(jnp.int32, (nk, nk), 0)
    col = lax.broadcasted_iota(jnp.int32, (nk, nk), 1)
    cur_ok = col <= row
    prev_ok = col >= row
    neg = jnp.float32(-jnp.inf)

    for g, (win, dil) in enumerate(ATT_GROUPS):
        assert win // dil == nk
        span = nk * dil
        shift = dil.bit_length() - 1

        def rope_body(c, carry, g=g):
            r0 = pl.multiple_of(c * ROPE_ROWS, ROPE_ROWS)
            cs = cos_ref[pl.ds(r0, ROPE_ROWS), :]
            sn = sin_ref[pl.ds(r0, ROPE_ROWS), :]
            qb = q_refs[g][0, pl.ds(r0, ROPE_ROWS), :]
            kb = k_refs[g][0, pl.ds(r0, ROPE_ROWS), :]
            qr = qb.astype(F32) * cs + jnp.dot(qb, perm, preferred_element_type=F32) * sn
            kr = kb.astype(F32) * cs + jnp.dot(kb, perm, preferred_element_type=F32) * sn
            qs[pl.ds(r0, ROPE_ROWS), :] = qr
            ks[pl.ds(r0, ROPE_ROWS), :] = kr
            vs[pl.ds(r0, ROPE_ROWS), :] = v_refs[g][0, pl.ds(r0, ROPE_ROWS), :].astype(F32)
            return carry

        lax.fori_loop(0, seq // ROPE_ROWS, rope_body, 0)
        keep = min(win, seq)
        kc_refs[g][0] = ks[seq - keep:seq, :]
        vc_refs[g][0] = vs[seq - keep:seq, :]

        def rows(start, dil=dil):
            if dil == 1:
                return pl.ds(pl.multiple_of(start, nk), nk)
            return pl.ds(start, nk, stride=dil)

        def blk_body(blk, carry, g=g, dil=dil, span=span, shift=shift, rows=rows):
            n = blk >> shift
            r = blk & (dil - 1)
            start = n * span + r
            has_prev = n > 0
            start_p = jnp.where(has_prev, start - span, start)
            q = qs[rows(start), :]
            k_c = ks[rows(start), :]
            k_p = ks[rows(start_p), :]
            v_c = vs[rows(start), :]
            v_p = vs[rows(start_p), :]
            s_c = jnp.where(cur_ok, _mm_nt(q, k_c) * ATT_SCALE, neg)
            no_prev = jnp.where(has_prev, jnp.float32(0.0), neg)
            s_p = jnp.where(prev_ok, _mm_nt(q, k_p) * ATT_SCALE + no_prev, neg)
            m = jnp.max(jnp.maximum(s_c, s_p), axis=-1, keepdims=True)
            p_c = jnp.exp(s_c - m)
            p_p = jnp.exp(s_p - m)
            l = jnp.sum(p_c + p_p, axis=-1, keepdims=True)
            acc = _mm(p_c, v_c) + _mm(p_p, v_p)
            o_s[g][rows(start), :] = acc * (1.0 / l)
            lse_s[g][rows(start), :] = jnp.broadcast_to(m + jnp.log(l), (nk, HEAD_DIM))
            return carry

        lax.fori_loop(0, seq // nk, blk_body, 0, unroll=4)

    def out_body(c, carry):
        sl = pl.ds(pl.multiple_of(c * nk, nk), nk)
        lse = [lse_s[g][sl, :] for g in range(N_ATT_GROUPS)]
        top = functools.reduce(jnp.maximum, lse)
        w = [jnp.exp(x - top) for x in lse]
        num = sum(w[g] * o_s[g][sl, :] for g in range(N_ATT_GROUPS))
        att_ref[0, sl, :] = (num / sum(w)).astype(att_ref.dtype)
        return carry

    lax.fori_loop(0, seq // nk, out_body, 0)


def _attn_prompt(u3, cos_t, sin_t, *, name):
    bsz, seq, _ = u3.shape
    perm = _rope_perm()
    in_specs = [
        pl.BlockSpec((seq, HEAD_DIM), lambda b, h: (0, 0)),
        pl.BlockSpec((seq, HEAD_DIM), lambda b, h: (0, 0)),
        pl.BlockSpec((HEAD_DIM, HEAD_DIM), lambda b, h: (0, 0)),
    ]
    for off in (O_AQ, O_AK, O_AV):
        for g in range(N_ATT_GROUPS):
            blk0 = off // HEAD_DIM + g * ATT_HEADS
            in_specs.append(pl.BlockSpec((1, seq, HEAD_DIM), lambda b, h, blk0=blk0: (b, 0, blk0 + h)))
    keeps = [min(w, seq) for w, _ in ATT_GROUPS]
    out_specs = [pl.BlockSpec((1, seq, HEAD_DIM), lambda b, h: (b, 0, h))]
    out_shape = [jax.ShapeDtypeStruct((bsz, seq, ATT_WIDTH), BF16)]
    for _ in range(2):
        for keep in keeps:
            out_specs.append(pl.BlockSpec((1, keep, HEAD_DIM), lambda b, h: (b, 0, h)))
            out_shape.append(jax.ShapeDtypeStruct((bsz, keep, ATT_WIDTH), F32))
    outs = pl.pallas_call(
        functools.partial(_attn_prompt_kernel, seq=seq),
        grid=(bsz, ATT_HEADS),
        in_specs=in_specs,
        out_specs=out_specs,
        out_shape=out_shape,
        scratch_shapes=[pltpu.VMEM((seq, HEAD_DIM), F32) for _ in range(3 + 2 * N_ATT_GROUPS)],
        compiler_params=_cparams(("parallel", "arbitrary")),
        name=name,
    )(cos_t, sin_t, perm, *([u3] * 9))
    return outs[0], outs[1:4], outs[4:7]


def _attn_sample_kernel(cos_ref, sin_ref, q_ref, k_ref, v_ref, c0_ref, c1_ref, c2_ref,
                        att_ref, kn_ref, *, n_new):
    cache_refs = (c0_ref, c1_ref, c2_ref)
    cs = cos_ref[...]
    sn = sin_ref[...]
    rows_pad = q_ref.shape[1]
    half = ROPE_DIM // 2
    lane = lax.broadcasted_iota(jnp.int32, (rows_pad, HEAD_DIM), 1)

    def rot_half(x):
        return jnp.where(lane < half, -pltpu.roll(x, HEAD_DIM - half, axis=1),
                         jnp.where(lane < ROPE_DIM, pltpu.roll(x, half, axis=1), 0.0))

    nk = ATT_GROUPS[0][0] // ATT_GROUPS[0][1]
    kv_rows = 2 * ATT_HEADS
    w_iota = lax.broadcasted_iota(jnp.int32, (nk, 1), 0)
    u_iota = lax.broadcasted_iota(jnp.int32, (rows_pad, 1), 0)
    neg = jnp.float32(-jnp.inf)
    att_ref[...] = jnp.zeros(att_ref.shape, att_ref.dtype)

    for h in range(ATT_HEADS):
        parts = [[None] * N_ATT_GROUPS for _ in range(n_new)]
        for g, (win, dil) in enumerate(ATT_GROUPS):
            hs = (g * ATT_HEADS + h) * HEAD_DIM
            qh = q_ref[0, :, hs:hs + HEAD_DIM]
            kh = k_ref[0, :, hs:hs + HEAD_DIM]
            vh = v_ref[0, :, hs:hs + HEAD_DIM]
            qr = _rb(qh * cs + rot_half(qh) * sn)
            kr = kh * cs + rot_half(kh) * sn
            kn_ref[0, :, hs:hs + HEAD_DIM] = kr
            kr = _rb(kr)
            vh = _rb(vh)
            cref = cache_refs[g]
            for t in range(n_new):
                q_t = qr[t:t + 1, :]
                if dil == 1:
                    p0 = 0
                else:
                    p0 = t
                k_c = _rb(cref[pl.ds(p0 * kv_rows + h, nk, stride=kv_rows * dil), :])
                v_c = _rb(cref[pl.ds(p0 * kv_rows + ATT_HEADS + h, nk, stride=kv_rows * dil), :])
                s_c = jnp.sum(k_c * q_t, axis=-1, keepdims=True) * ATT_SCALE
                s_n = jnp.sum(kr * q_t, axis=-1, keepdims=True) * ATT_SCALE
                if dil == 1:
                    s_c = jnp.where(w_iota >= t, s_c, neg)
                    s_n = jnp.where(u_iota <= t, s_n, neg)
                else:
                    s_n = jnp.where(u_iota == t, s_n, neg)
                m = jnp.maximum(jnp.max(s_c, axis=0, keepdims=True), jnp.max(s_n, axis=0, keepdims=True))
                p_c = jnp.exp(s_c - m)
                p_n = jnp.exp(s_n - m)
                l = jnp.sum(p_c, axis=0, keepdims=True) + jnp.sum(p_n, axis=0, keepdims=True)
                acc = jnp.sum(_rb(p_c) * v_c, axis=0, keepdims=True) + jnp.sum(_rb(p_n) * vh, axis=0, keepdims=True)
                parts[t][g] = (m, l, acc)
        for t in range(n_new):
            m_all = functools.reduce(jnp.maximum, [p[0] for p in parts[t]])
            num = sum(jnp.exp(p[0] - m_all) * p[2] for p in parts[t])
            den = sum(jnp.exp(p[0] - m_all) * p[1] for p in parts[t])
            att_ref[0, t:t + 1, h * HEAD_DIM:(h + 1) * HEAD_DIM] = num / den


def _attn_sample(u3, caches2d, cos_t, sin_t, *, li, n_new, name):
    bsz, rows_pad, _ = u3.shape
    qkv_w = N_ATT_GROUPS * ATT_WIDTH
    in_specs = [
        pl.BlockSpec((rows_pad, HEAD_DIM), lambda b: (0, 0)),
        pl.BlockSpec((rows_pad, HEAD_DIM), lambda b: (0, 0)),
        pl.BlockSpec((1, rows_pad, qkv_w), lambda b: (b, 0, O_AQ // qkv_w)),
        pl.BlockSpec((1, rows_pad, qkv_w), lambda b: (b, 0, O_AK // qkv_w)),
        pl.BlockSpec((1, rows_pad, qkv_w), lambda b: (b, 0, O_AV // qkv_w)),
    ]
    for (win, _), c in zip(ATT_GROUPS, caches2d):
        rows = c.shape[0] // (DEPTH * bsz)
        in_specs.append(pl.BlockSpec((rows, HEAD_DIM), lambda b, bsz=bsz: (li * bsz + b, 0)))
    return pl.pallas_call(
        functools.partial(_attn_sample_kernel, n_new=n_new),
        grid=(bsz,),
        in_specs=in_specs,
        out_specs=[
            pl.BlockSpec((1, rows_pad, ATT_WIDTH), lambda b: (b, 0, 0)),
            pl.BlockSpec((1, rows_pad, qkv_w), lambda b: (b, 0, 0)),
        ],
        out_shape=[
            jax.ShapeDtypeStruct((bsz, rows_pad, ATT_WIDTH), F32),
            jax.ShapeDtypeStruct((bsz, rows_pad, qkv_w), F32),
        ],
        compiler_params=_cparams(("parallel",)),
        name=name,
    )(cos_t, sin_t, u3, u3, u3, *caches2d)


ML_HPB = 2
ML_VPB = 2


def _mlstm_kernel(q_ref, k_ref, *refs, chunk, precise):
    nvb = ML_HPB // ML_VPB
    v_refs, mo_refs = refs[:nvb], refs[nvb:2 * nvb]
    gc_ref, gr_ref, mlg_ref, c0_ref, n0_ref, m0_ref, h_ref, c_out, n_out, m_out, ct_s, n_s, m_s = refs[2 * nvb:]
    s_idx = pl.program_id(2)
    n_s_blocks = pl.num_programs(2)
    sb = q_ref.shape[1]
    L = chunk

    @pl.when(s_idx == 0)
    def _():
        for hh in range(ML_HPB):
            ct_s[hh] = c0_ref[0, hh].T
            n_s[hh] = n0_ref[0, 0, hh:hh + 1, :]
            m_s[hh] = m0_ref[0, 0, hh:hh + 1, :]

    row = lax.broadcasted_iota(jnp.int32, (L, L), 0)
    col = lax.broadcasted_iota(jnp.int32, (L, L), 1)
    causal = row >= col
    neg = jnp.float32(-jnp.inf)

    def chunk_body(c, carry):
        r0 = pl.multiple_of(c * L, L)
        gc = gc_ref[0, 0, pl.ds(r0, L), :]
        gr = gr_ref[0, 0, :, pl.ds(r0, L)]
        for hh in range(ML_HPB):
            ig_col = gc[:, hh:hh + 1]
            lf_col = gc[:, ML_HPB + hh:ML_HPB + hh + 1]
            ig_row = gr[hh:hh + 1, :]
            lf_row = gr[ML_HPB + hh:ML_HPB + hh + 1, :]
            b_col = jnp.sum(jnp.where(causal, lf_row, 0.0), axis=1, keepdims=True)
            b_row = jnp.sum(jnp.where(row <= col, lf_col, 0.0), axis=0, keepdims=True)
            qf = q_ref[0, pl.ds(r0, L), hh * ML_QK:(hh + 1) * ML_QK]
            kf = k_ref[0, pl.ds(r0, L), hh * ML_QK:(hh + 1) * ML_QK]
            vcols = slice((hh % ML_VPB) * ML_V, (hh % ML_VPB + 1) * ML_V)
            vf = v_refs[hh // ML_VPB][0, pl.ds(r0, L), vcols]
            mo = mo_refs[hh // ML_VPB][0, pl.ds(r0, L), vcols].astype(F32)
            ct = ct_s[hh]
            n_row = n_s[hh]
            m_prev = m_s[hh][:, 0:1]
            inter = b_col + m_prev
            dm = jnp.where(causal, b_col - b_row + ig_row, neg)
            m_t = jnp.maximum(inter, jnp.max(dm, axis=1, keepdims=True))
            w_intra = jnp.exp(dm - m_t)
            w_inter = jnp.exp(inter - m_t)
            ks = kf.astype(F32) * ML_K_SCALE
            sw = _mm_nt(qf, ks, precise) * w_intra
            num = w_inter * _mm(qf, ct, precise) + _mm(sw, vf, precise)
            den = w_inter * jnp.sum(_rb(qf, precise) * _rb(n_row, precise), axis=1, keepdims=True) + jnp.sum(
                sw, axis=1, keepdims=True)
            hv = num / jnp.maximum(jnp.abs(den), jnp.exp(-m_t))
            hn = hv * lax.rsqrt(jnp.mean(hv * hv, axis=-1, keepdims=True) + EPS)
            hn = hn * mlg_ref[:, hh * ML_V:(hh + 1) * ML_V]
            h_ref[0, pl.ds(r0, L), hh * ML_V:(hh + 1) * ML_V] = (hn * _sigmoid(mo)).astype(h_ref.dtype)
            b_last = b_col[L - 1:L, :]
            m_new = m_t[L - 1:L, :]
            decay = jnp.exp(b_last + m_prev - m_new)
            wk_col = jnp.exp(b_last - b_col + ig_col - m_new)
            ct_s[hh] = decay * ct + _mm_tn(ks * wk_col, vf, precise)
            n_s[hh] = decay * n_row + jnp.sum(_rb(ks, precise) * _rb(wk_col, precise), axis=0, keepdims=True)
            m_s[hh] = jnp.broadcast_to(m_new, (1, LANES))
        return carry

    lax.fori_loop(0, sb // L, chunk_body, 0)

    @pl.when(s_idx == n_s_blocks - 1)
    def _():
        for hh in range(ML_HPB):
            c_out[0, hh] = ct_s[hh].T
            n_out[0, 0, hh:hh + 1, :] = n_s[hh]
            m_out[0, 0, hh:hh + 1, :] = m_s[hh]


def _mlstm(u3, gc, gr, mlg, c0, n0, m0, *, chunk, sb, out_dtype, precise, name):
    bsz, seq, _ = u3.shape
    hg_n = ML_HEADS // ML_HPB
    qk_w, v_w, vb_w = ML_HPB * ML_QK, ML_HPB * ML_V, ML_VPB * ML_V
    nvb = ML_HPB // ML_VPB
    v_specs = [pl.BlockSpec((1, sb, vb_w), lambda b, hg, s, j=j, o=off // vb_w: (b, s, o + hg * nvb + j))
               for off in (O_MV, O_MO) for j in range(nvb)]
    n0r = n0.reshape(bsz, hg_n, ML_HPB, ML_QK)
    m0r = jnp.broadcast_to(m0.reshape(bsz, hg_n, ML_HPB, 1), (bsz, hg_n, ML_HPB, LANES))
    state_spec = pl.BlockSpec((1, 1, ML_HPB, LANES), lambda b, hg, s: (b, hg, 0, 0))
    c_spec = pl.BlockSpec((1, ML_HPB, ML_V, ML_QK), lambda b, hg, s: (b, hg, 0, 0))
    h, c_new, n_new, m_new = pl.pallas_call(
        functools.partial(_mlstm_kernel, chunk=chunk, precise=precise),
        grid=(bsz, hg_n, seq // sb),
        in_specs=[
            pl.BlockSpec((1, sb, qk_w), lambda b, hg, s: (b, s, O_MQ // qk_w + hg)),
            pl.BlockSpec((1, sb, qk_w), lambda b, hg, s: (b, s, O_MK // qk_w + hg)),
            *v_specs,
            pl.BlockSpec((1, 1, sb, 2 * ML_HPB), lambda b, hg, s: (b, hg, s, 0)),
            pl.BlockSpec((1, 1, 2 * ML_HPB, sb), lambda b, hg, s: (b, hg, 0, s)),
            pl.BlockSpec((1, v_w), lambda b, hg, s: (0, hg)),
            c_spec, state_spec, state_spec,
        ],
        out_specs=[
            pl.BlockSpec((1, sb, v_w), lambda b, hg, s: (b, s, hg)),
            c_spec, state_spec, state_spec,
        ],
        out_shape=[
            jax.ShapeDtypeStruct((bsz, seq, ML_HEADS * ML_V), out_dtype),
            jax.ShapeDtypeStruct((bsz, ML_HEADS, ML_V, ML_QK), F32),
            jax.ShapeDtypeStruct((bsz, hg_n, ML_HPB, LANES), F32),
            jax.ShapeDtypeStruct((bsz, hg_n, ML_HPB, LANES), F32),
        ],
        scratch_shapes=[
            pltpu.VMEM((ML_HPB, ML_QK, ML_V), F32),
            pltpu.VMEM((ML_HPB, 1, ML_QK), F32),
            pltpu.VMEM((ML_HPB, 1, LANES), F32),
        ],
        compiler_params=_cparams(("parallel", "parallel", "arbitrary")),
        name=name,
    )(u3, u3, *([u3] * (2 * nvb)), gc, gr, mlg, c0, n0r, m0r)
    return h, c_new, n_new.reshape(bsz, ML_HEADS, ML_QK), m_new[..., 0].reshape(bsz, ML_HEADS)


def _gate_layouts(gates, bsz, seq, seq_pad):
    hg_n = ML_HEADS // ML_HPB
    ig = gates[:, :ML_HEADS].reshape(bsz, seq, ML_HEADS)
    lf = gates[:, ML_HEADS:2 * ML_HEADS].reshape(bsz, seq, ML_HEADS)
    if seq_pad > seq:
        ig = jnp.pad(ig, ((0, 0), (0, seq_pad - seq), (0, 0)), constant_values=-jnp.inf)
        lf = jnp.pad(lf, ((0, 0), (0, seq_pad - seq), (0, 0)))
    ig = ig.reshape(bsz, seq_pad, hg_n, ML_HPB).transpose(0, 2, 1, 3)
    lf = lf.reshape(bsz, seq_pad, hg_n, ML_HPB).transpose(0, 2, 1, 3)
    gc = jnp.concatenate([ig, lf], axis=-1)
    return gc, gc.transpose(0, 1, 3, 2)


def _router_kernel(x_ref, g_ref, w_ref, b_ref, xn_ref, route_ref, *, precise):
    xf = x_ref[...]
    r = lax.rsqrt(jnp.mean(xf * xf, axis=-1, keepdims=True) + EPS)
    xn = (xf * r) * g_ref[...]
    xn_ref[...] = xn.astype(xn_ref.dtype)
    logits = _mm(xn, w_ref[...], precise) + b_ref[...]
    lane = lax.broadcasted_iota(jnp.int32, logits.shape, 1).astype(F32)
    neg = jnp.float32(-jnp.inf)
    big = jnp.float32(LANES)
    gl = jnp.where(lane < N_GROUPS, logits, neg)
    g_max = jnp.max(gl, axis=-1, keepdims=True)
    g_val = 1.0 / jnp.sum(jnp.exp(gl - g_max), axis=-1, keepdims=True)
    g_idx = jnp.min(jnp.where(gl == g_max, lane, big), axis=-1, keepdims=True)
    lo = N_GROUPS + EXP_PER_GROUP * g_idx
    es = jnp.where((lane >= lo) & (lane < lo + EXP_PER_GROUP), logits, neg)
    t0 = jnp.max(es, axis=-1, keepdims=True)
    i0 = jnp.min(jnp.where(es == t0, lane, big), axis=-1, keepdims=True)
    es1 = jnp.where(lane == i0, neg, es)
    t1 = jnp.max(es1, axis=-1, keepdims=True)
    i1 = jnp.min(jnp.where(es1 == t1, lane, big), axis=-1, keepdims=True)
    e1 = jnp.exp(t1 - t0)
    w0 = g_val / (1.0 + e1)
    w1 = g_val * e1 / (1.0 + e1)
    route = jnp.where(lane == 0, i0 - N_GROUPS,
                      jnp.where(lane == 1, i1 - N_GROUPS,
                                jnp.where(lane == 2, w0, jnp.where(lane == 3, w1, 0.0))))
    route_ref[...] = route


def _router(x, g, w_r, b_r, *, tm, precise, name):
    m, d = x.shape
    return pl.pallas_call(
        functools.partial(_router_kernel, precise=precise),
        grid=(m // tm,),
        in_specs=[
            pl.BlockSpec((tm, d), lambda i: (i, 0)),
            pl.BlockSpec((1, d), lambda i: (0, 0)),
            pl.BlockSpec((d, LANES), lambda i: (0, 0)),
            pl.BlockSpec((1, LANES), lambda i: (0, 0)),
        ],
        out_specs=[pl.BlockSpec((tm, d), lambda i: (i, 0)), pl.BlockSpec((tm, LANES), lambda i: (i, 0))],
        out_shape=[jax.ShapeDtypeStruct((m, d), F32), jax.ShapeDtypeStruct((m, LANES), F32)],
        compiler_params=_cparams(("parallel",)),
        name=name,
    )(x, g, w_r, b_r)


def _ffn_kernel(te_ref, nv_ref, xs_ref, wr_ref, wg_ref, wu_ref, wd_ref, y_ref, wg_s, wu_s, wd_s, *, precise):
    i = pl.program_id(0)
    valid = i < nv_ref[0]
    prev = te_ref[jnp.maximum(i - 1, 0)]

    @pl.when(valid & ((i == 0) | (te_ref[i] != prev)))
    def _():
        wg_s[...] = wg_ref[...].astype(wg_s.dtype)
        wu_s[...] = wu_ref[...].astype(wu_s.dtype)
        wd_s[...] = wd_ref[...].astype(wd_s.dtype)

    @pl.when(valid)
    def _():
        xs = xs_ref[...]
        gt = _mm(xs, wg_s[...], precise)
        up = _mm(xs, wu_s[...], precise)
        y_ref[...] = _mm((gt * _sigmoid(gt) * up) * wr_ref[...], wd_s[...], precise)


def _ffn(xs, wrow, te, nv, w_gate, w_up, w_down, *, li, tm, precise, name):
    rows, d = xs.shape
    n_tiles = rows // tm
    f = EXPERT_FF
    wdt = F32 if precise else BF16

    def row_blk(i, te, nv):
        return (jnp.minimum(i, nv[0] - 1), 0)

    return pl.pallas_call(
        functools.partial(_ffn_kernel, precise=precise),
        grid_spec=pltpu.PrefetchScalarGridSpec(
            num_scalar_prefetch=2,
            grid=(n_tiles,),
            in_specs=[
                pl.BlockSpec((tm, d), row_blk),
                pl.BlockSpec((tm, 1), row_blk),
                pl.BlockSpec((None, d, f), lambda i, te, nv: (li, 0, te[i])),
                pl.BlockSpec((None, d, f), lambda i, te, nv: (li, 0, te[i])),
                pl.BlockSpec((None, f, d), lambda i, te, nv: (li, te[i], 0)),
            ],
            out_specs=pl.BlockSpec((tm, d), row_blk),
            scratch_shapes=[pltpu.VMEM((d, f), wdt), pltpu.VMEM((d, f), wdt), pltpu.VMEM((f, d), wdt)],
        ),
        out_shape=jax.ShapeDtypeStruct((rows, d), F32),
        compiler_params=_cparams(("arbitrary",)),
        name=name,
    )(te, nv, xs, wrow, w_gate, w_up, w_down)


def _rank_kernel(route_ref, offs_ref, dest_ref, run_s):
    @pl.when(pl.program_id(0) == 0)
    def _():
        run_s[...] = jnp.zeros(run_s.shape, run_s.dtype)

    route = route_ref[...]
    tm = route.shape[0]
    lane = lax.broadcasted_iota(jnp.int32, route.shape, 1).astype(F32)
    is0 = lane == route[:, 0:1]
    is1 = lane == route[:, 1:2]
    onehot = jnp.where(is0, 1.0, 0.0) + jnp.where(is1, 1.0, 0.0)
    r = lax.broadcasted_iota(jnp.int32, (tm, tm), 0)
    c = lax.broadcasted_iota(jnp.int32, (tm, tm), 1)
    earlier = jnp.where(c < r, 1.0, 0.0)
    before = _mm(earlier, onehot)
    base = offs_ref[...] + run_s[...] + before
    d0 = jnp.sum(jnp.where(is0, base, 0.0), axis=-1, keepdims=True)
    d1 = jnp.sum(jnp.where(is1, base, 0.0), axis=-1, keepdims=True)
    run_s[...] = run_s[...] + jnp.sum(onehot, axis=0, keepdims=True)
    dest_ref[...] = jnp.where(lane == 0.0, d0, jnp.where(lane == 1.0, d1, 0.0)).astype(jnp.int32)


def _rank(route, offs_row, *, tm, name):
    m = route.shape[0]
    return pl.pallas_call(
        _rank_kernel,
        grid=(m // tm,),
        in_specs=[pl.BlockSpec((tm, LANES), lambda i: (i, 0)), pl.BlockSpec((1, LANES), lambda i: (0, 0))],
        out_specs=pl.BlockSpec((tm, LANES), lambda i: (i, 0)),
        out_shape=jax.ShapeDtypeStruct((m, LANES), jnp.int32),
        scratch_shapes=[pltpu.VMEM((1, LANES), F32)],
        compiler_params=_cparams(("arbitrary",)),
        name=name,
    )(route, offs_row)


ROW_DMA_UNROLL = 8


def _row_copy(src, src_row, dst, dst_row, sem):
    return pltpu.make_async_copy(src.at[pl.ds(src_row, 1)], dst.at[pl.ds(dst_row, 1)], sem)


def _scatter_kernel(dest_ref, last_ref, xn_ref, xs_hbm, zero_s, sem, zsem, *, tile):
    i = pl.program_id(0)
    tm = xn_ref.shape[0]

    def zero_copy(e):
        return pltpu.make_async_copy(zero_s, xs_hbm.at[pl.ds(pl.multiple_of(last_ref[e], tile), tile)], zsem)

    @pl.when(i == 0)
    def _():
        zero_s[...] = jnp.zeros(zero_s.shape, zero_s.dtype)

        def z_start(e, carry):
            @pl.when(last_ref[e] >= 0)
            def _():
                zero_copy(e).start()
            return carry

        def z_wait(e, carry):
            @pl.when(last_ref[e] >= 0)
            def _():
                zero_copy(e).wait()
            return carry

        lax.fori_loop(0, N_EXPERTS, z_start, 0)
        lax.fori_loop(0, N_EXPERTS, z_wait, 0)

    base = i * (2 * tm)

    def start(r, carry):
        _row_copy(xn_ref, r, xs_hbm, dest_ref[base + 2 * r], sem).start()
        _row_copy(xn_ref, r, xs_hbm, dest_ref[base + 2 * r + 1], sem).start()
        return carry

    def wait(r, carry):
        _row_copy(xn_ref, r, xs_hbm, dest_ref[base + 2 * r], sem).wait()
        _row_copy(xn_ref, r, xs_hbm, dest_ref[base + 2 * r + 1], sem).wait()
        return carry

    lax.fori_loop(0, tm, start, 0, unroll=ROW_DMA_UNROLL)
    lax.fori_loop(0, tm, wait, 0, unroll=ROW_DMA_UNROLL)


def _scatter(dest, last_tile, xn, *, rows, tm, tile, name):
    m, d = xn.shape
    return pl.pallas_call(
        functools.partial(_scatter_kernel, tile=tile),
        grid_spec=pltpu.PrefetchScalarGridSpec(
            num_scalar_prefetch=2,
            grid=(m // tm,),
            in_specs=[pl.BlockSpec((tm, d), lambda i, dest, last: (i, 0))],
            out_specs=pl.BlockSpec(memory_space=pl.ANY),
            scratch_shapes=[pltpu.VMEM((tile, d), F32), pltpu.SemaphoreType.DMA(()), pltpu.SemaphoreType.DMA(())],
        ),
        out_shape=jax.ShapeDtypeStruct((rows, d), F32),
        compiler_params=_cparams(("arbitrary",)),
        name=name,
    )(dest, last_tile, xn)


def _combine_kernel(dest_ref, x_ref, y_hbm, o_ref, ya, yb, sem):
    i = pl.program_id(0)
    tm = x_ref.shape[0]
    base = i * (2 * tm)

    def start(r, carry):
        _row_copy(y_hbm, dest_ref[base + 2 * r], ya, r, sem).start()
        _row_copy(y_hbm, dest_ref[base + 2 * r + 1], yb, r, sem).start()
        return carry

    def wait(r, carry):
        _row_copy(y_hbm, dest_ref[base + 2 * r], ya, r, sem).wait()
        _row_copy(y_hbm, dest_ref[base + 2 * r + 1], yb, r, sem).wait()
        return carry

    lax.fori_loop(0, tm, start, 0, unroll=ROW_DMA_UNROLL)
    lax.fori_loop(0, tm, wait, 0, unroll=ROW_DMA_UNROLL)
    o_ref[...] = x_ref[...] + (ya[...] + yb[...])


def _combine(dest, x, y, *, tm, name):
    m, d = x.shape
    return pl.pallas_call(
        _combine_kernel,
        grid_spec=pltpu.PrefetchScalarGridSpec(
            num_scalar_prefetch=1,
            grid=(m // tm,),
            in_specs=[
                pl.BlockSpec((tm, d), lambda i, dest: (i, 0)),
                pl.BlockSpec(memory_space=pl.ANY),
            ],
            out_specs=pl.BlockSpec((tm, d), lambda i, dest: (i, 0)),
            scratch_shapes=[pltpu.VMEM((tm, d), F32), pltpu.VMEM((tm, d), F32), pltpu.SemaphoreType.DMA(())],
        ),
        out_shape=jax.ShapeDtypeStruct((m, d), F32),
        compiler_params=_cparams(("arbitrary",)),
        name=name,
    )(dest, x, y)


def _moe(x, g, w_r, b_r, w_gate, w_up, w_down, *, li, tm_tok, tm_ffn, precise, name):
    m = x.shape[0]
    xn, route = _router(x, g, w_r, b_r, tm=tm_tok, precise=precise, name=name + "_router")
    eid = route[:, 0:2].astype(jnp.int32)
    counts = jnp.sum((eid[:, :, None] == jnp.arange(N_EXPERTS, dtype=jnp.int32)).astype(jnp.int32), axis=(0, 1))
    padded = ((counts + tm_ffn - 1) // tm_ffn) * tm_ffn
    ends = jnp.cumsum(padded)
    offs = ends - padded
    rows = 2 * m + N_EXPERTS * tm_ffn
    tile_start = jnp.arange(rows // tm_ffn, dtype=jnp.int32) * tm_ffn
    te = jnp.minimum(jnp.sum((ends[None, :] <= tile_start[:, None]).astype(jnp.int32), axis=1), N_EXPERTS - 1)
    nv = (ends[-1] // tm_ffn).reshape(1)
    last_tile = jnp.where(padded > 0, ends - tm_ffn, -1)
    offs_row = jnp.pad(offs.astype(F32), (0, LANES - N_EXPERTS)).reshape(1, LANES)
    dest = _rank(route, offs_row, tm=tm_tok, name=name + "_rank")[:, 0:2].reshape(-1)
    xs = _scatter(dest, last_tile, xn, rows=rows, tm=tm_tok, tile=tm_ffn, name=name + "_scatter")
    wrow = jnp.zeros((rows, 1), F32).at[dest, 0].set(route[:, 2:4].reshape(-1))
    y = _ffn(xs, wrow, te, nv, w_gate, w_up, w_down, li=li, tm=tm_ffn, precise=precise, name=name + "_ffn")
    return _combine(dest, x, y, tm=tm_tok, name=name + "_combine")


def _rmsnorm_kernel(x_ref, g_ref, o_ref):
    xf = x_ref[...]
    r = lax.rsqrt(jnp.mean(xf * xf, axis=-1, keepdims=True) + EPS)
    o_ref[...] = (xf * r) * g_ref[...]


def _rmsnorm(x, g, *, tm, name):
    m, d = x.shape
    return pl.pallas_call(
        _rmsnorm_kernel,
        grid=(m // tm,),
        in_specs=[pl.BlockSpec((tm, d), lambda i: (i, 0)), pl.BlockSpec((1, d), lambda i: (0, 0))],
        out_specs=pl.BlockSpec((tm, d), lambda i: (i, 0)),
        out_shape=jax.ShapeDtypeStruct((m, d), F32),
        compiler_params=_cparams(("parallel",)),
        name=name,
    )(x, g)


def _layer_weights(p, li):
    b_if = jnp.pad(p["b_if"][li], (0, 2 * LANES - 2 * ML_HEADS)).reshape(1, 2 * LANES)
    w_r = jnp.pad(jnp.concatenate([p["w_rg"][li], p["w_re"][li]], axis=1),
                  ((0, 0), (0, LANES - N_GROUPS - N_EXPERTS)))
    b_r = jnp.pad(jnp.concatenate([p["b_rg"][li], p["b_re"][li]]), (0, LANES - N_GROUPS - N_EXPERTS)).reshape(1, LANES)
    return b_if, w_r, b_r


def _run_trunk(x3, p, layer_w, *, caches, c0s, n0s, m0s, pos0):
    bsz, seq, d = x3.shape
    m = bsz * seq
    x = x3.reshape(m, d)
    precise = False
    if caches is None:
        act_dtype, tag = BF16, "p"
        tm, tn, tm_tok, tm_ffn, seq_pad = 1024, 512, 256, 256, seq
    else:
        act_dtype, tag = F32, "s"
        tm, tn, tm_tok, tm_ffn, seq_pad = m, 512, m, 16, 16
    cos_t, sin_t = _rope_tables(pos0 + jnp.arange(seq_pad, dtype=jnp.int32))
    zeros_bias = jnp.zeros((1, N_MAIN), F32)
    new_kv = [[] for _ in ATT_GROUPS]
    c_all, n_all, m_all = [], [], []
    for li in range(DEPTH):
        b_if, w_r, b_r = layer_w[li]
        g_attn = p["attn_norm_g"][li].reshape(1, d)
        common = dict(li=li, tm=tm, precise=precise)
        u = _norm_matmul(x, g_attn, p["w_in"], zeros_bias, col_blk0=0, n_out=N_MAIN, tn=TN_MAIN,
                         out_dtype=act_dtype, act="none", name=f"{tag}{li}_inproj", **common)
        n_sg = 2 * D_MODEL
        sg_main = _norm_matmul(x, g_attn, p["w_in"], zeros_bias, col_blk0=N_MAIN // tn, n_out=n_sg, tn=tn,
                               out_dtype=act_dtype, act="sigmoid", name=f"{tag}{li}_gateproj", **common)
        tail_blk = (N_MAIN + n_sg) // LANES
        gt = _norm_matmul(x, g_attn, p["w_in"], b_if, col_blk0=N_MAIN // LANES, col_stride=tail_blk - N_MAIN // LANES,
                          n_out=2 * LANES, tn=LANES, out_dtype=F32, act="gates_tail", name=f"{tag}{li}_ifproj",
                          **common)
        gates = gt[:, :LANES]
        n_gate_cols = 2 * ML_HEADS
        sg = jnp.concatenate([sg_main[:, n_gate_cols:], gt[:, LANES:LANES + n_gate_cols].astype(act_dtype)], axis=1)
        u3 = u.reshape(bsz, seq, N_MAIN)
        if seq_pad > seq:
            u3 = jnp.pad(u3, ((0, 0), (0, seq_pad - seq), (0, 0)))
        if caches is None:
            att3, kcs, vcs = _attn_prompt(u3, cos_t, sin_t, name=f"{tag}{li}_attn")
            for gi in range(N_ATT_GROUPS):
                keep = kcs[gi].shape[1]
                new_kv[gi].append(jnp.stack([kcs[gi].reshape(bsz, keep, ATT_HEADS, HEAD_DIM),
                                             vcs[gi].reshape(bsz, keep, ATT_HEADS, HEAD_DIM)], axis=2))
            att = att3.reshape(m, ATT_WIDTH)
        else:
            caches2d = [c.reshape(-1, HEAD_DIM) for c in caches]
            att3, kn3 = _attn_sample(u3, caches2d, cos_t, sin_t, li=li, n_new=seq, name=f"{tag}{li}_attn")
            att = att3[:, :seq].reshape(m, ATT_WIDTH)
            nh = N_ATT_GROUPS * ATT_HEADS
            k_new = kn3[:, :seq].reshape(bsz, seq, nh, HEAD_DIM)
            v_new = u3[:, :seq, O_AV:O_AV + nh * HEAD_DIM].reshape(bsz, seq, nh, HEAD_DIM)
            for gi in range(N_ATT_GROUPS):
                sl = slice(gi * ATT_HEADS, (gi + 1) * ATT_HEADS)
                new_kv[gi].append(jnp.stack([k_new[:, :, sl], v_new[:, :, sl]], axis=2))
        gc, gr = _gate_layouts(gates, bsz, seq, seq_pad)
        mlg = p["ml_norm_g"][li].reshape(1, ML_HEADS * ML_V)
        hg3, c_new, n_new, m_new = _mlstm(
            u3, gc, gr, mlg, c0s[li], n0s[li], m0s[li], chunk=min(128, seq_pad), sb=min(512, seq_pad),
            out_dtype=act_dtype, precise=precise, name=f"{tag}{li}_mlstm")
        hg = hg3[:, :seq].reshape(m, ML_HEADS * ML_V)
        c_all.append(c_new)
        n_all.append(n_new)
        m_all.append(m_new)
        merged = _merge(att, hg, sg, p["w_pa"], p["w_pm"], li=li, tm=tm, tn=tn, out_dtype=act_dtype,
                        precise=precise, name=f"{tag}{li}_merge")
        x = _outproj(x, merged, p["w_out"], li=li, tm=tm, tn=tn, precise=precise, name=f"{tag}{li}_outproj")
        x = _moe(x, p["ffn_norm_g"][li].reshape(1, d), w_r, b_r, p["w_gate"], p["w_up"], p["w_down"],
                 li=li, tm_tok=tm_tok, tm_ffn=tm_ffn, precise=precise, name=f"{tag}{li}_moe")
    y = _rmsnorm(x, p["final_norm_g"].reshape(1, d), tm=min(m, 512), name=f"{tag}_final_norm")
    new_kv = [jnp.stack(a) for a in new_kv]
    if caches is not None:
        new_kv = [jnp.concatenate([c[:, :, seq:], fresh], axis=2) for c, fresh in zip(caches, new_kv)]
    return (y.reshape(bsz, seq, d), new_kv, jnp.stack(c_all), jnp.stack(n_all), jnp.stack(m_all))


def kernel(x_prompt, x_sample, cache_kv_w128, cache_kv_w512, cache_kv_w2048, state_C, state_n, state_m,
           attn_norm_g, w_in, b_if, ml_norm_g, w_pa, w_pm, w_out, ffn_norm_g, w_rg, b_rg, w_re, b_re,
           w_gate, w_up, w_down, final_norm_g):
    p = dict(attn_norm_g=attn_norm_g, w_in=jnp.swapaxes(w_in, 1, 2), b_if=b_if, ml_norm_g=ml_norm_g, w_pa=w_pa, w_pm=w_pm,
             w_out=w_out, ffn_norm_g=ffn_norm_g, w_rg=w_rg, b_rg=b_rg, w_re=w_re, b_re=b_re,
             w_gate=w_gate, w_up=w_up, w_down=w_down, final_norm_g=final_norm_g)
    layer_w = [_layer_weights(p, li) for li in range(DEPTH)]
    bp = x_prompt.shape[0]
    c0 = jnp.zeros((DEPTH, bp, ML_HEADS, ML_V, ML_QK), F32)
    n0 = jnp.zeros((DEPTH, bp, ML_HEADS, ML_QK), F32)
    m0 = jnp.zeros((DEPTH, bp, ML_HEADS), F32)
    y_p, p_kv, p_c, p_n, p_m = _run_trunk(x_prompt, p, layer_w, caches=None,
                                          c0s=c0, n0s=n0, m0s=m0, pos0=0)
    y_s, s_kv, s_c, s_n, s_m = _run_trunk(x_sample, p, layer_w,
                                          caches=[cache_kv_w128, cache_kv_w512, cache_kv_w2048],
                                          c0s=state_C, n0s=state_n, m0s=state_m, pos0=PAST_LEN)
    return (y_p, y_s, p_kv[0], p_kv[1], p_kv[2], p_c, p_n, p_m,
            s_kv[0], s_kv[1], s_kv[2], s_c, s_n, s_m)
```

```python
import functools

import jax
import jax.numpy as jnp
from jax import lax
from jax.experimental import pallas as pl
from jax.experimental.pallas import tpu as pltpu

F32 = jnp.float32
BF16 = jnp.bfloat16

D_MODEL = 2048
DEPTH = 2
PAST_LEN = 16384
ATT_GROUPS = ((128, 1), (512, 4), (2048, 16))
N_ATT_GROUPS = 3
ATT_HEADS = 4
HEAD_DIM = 128
ATT_WIDTH = ATT_HEADS * HEAD_DIM
ATT_SCALE = HEAD_DIM ** -0.5
ROPE_DIM = HEAD_DIM // 4
ROPE_THETA = 500000.0
ML_HEADS = 8
ML_QK = 128
ML_V = 256
ML_K_SCALE = ML_QK ** -0.5
N_GROUPS = 4
EXP_PER_GROUP = 8
N_EXPERTS = N_GROUPS * EXP_PER_GROUP
EXPERT_FF = 256
EPS = 1e-6

O_AQ, O_AK, O_AV = 0, 1536, 3072
O_MQ, O_MK, O_MV, O_MO = 4608, 5632, 6656, 8704
O_MI, O_MF, O_GA, O_GB = 10752, 10760, 10768, 12816
N_MAIN = O_MI
LANES = 128
TN_MAIN = 768
VMEM_LIMIT_MB = 56


def _cparams(sem):
    return pltpu.CompilerParams(dimension_semantics=sem, vmem_limit_bytes=VMEM_LIMIT_MB * 1024 * 1024)


def _dot(a, b, dims, precise):
    dn = (dims, ((), ()))
    if precise:
        return lax.dot_general(a.astype(F32), b.astype(F32), dn, preferred_element_type=F32,
                               precision=lax.Precision.HIGHEST)
    return lax.dot_general(a.astype(BF16), b.astype(BF16), dn, preferred_element_type=F32)


def _mm(a, b, precise=False):
    return _dot(a, b, ((1,), (0,)), precise)


def _mm_nt(a, b, precise=False):
    return _dot(a, b, ((1,), (1,)), precise)


def _mm_tn(a, b, precise=False):
    return _dot(a, b, ((0,), (0,)), precise)


def _rb(x, precise=False):
    return x.astype(F32) if precise else x.astype(BF16).astype(F32)


def _sigmoid(z):
    return 1.0 / (1.0 + jnp.exp(-z))


def _log_sigmoid(z):
    return jnp.minimum(z, 0.0) - jnp.log1p(jnp.exp(-jnp.abs(z)))


def _norm_matmul_kernel(x_ref, g_ref, w_ref, b_ref, o_ref, xn_ref, *, act, precise):
    @pl.when(pl.program_id(1) == 0)
    def _():
        xf = x_ref[...]
        r = lax.rsqrt(jnp.mean(xf * xf, axis=-1, keepdims=True) + EPS)
        xn_ref[...] = ((xf * r) * g_ref[...]).astype(xn_ref.dtype)

    acc = _mm_nt(xn_ref[...], w_ref[...], precise)
    if act == "sigmoid":
        acc = _sigmoid(acc)
    elif act == "gates_tail":
        @pl.when(pl.program_id(1) == 0)
        def _():
            z = acc + b_ref[...]
            lane = lax.broadcasted_iota(jnp.int32, z.shape, 1)
            o_ref[...] = jnp.where(lane < ML_HEADS, z, _log_sigmoid(z)).astype(o_ref.dtype)

        @pl.when(pl.program_id(1) != 0)
        def _():
            o_ref[...] = _sigmoid(acc).astype(o_ref.dtype)
        return
    o_ref[...] = acc.astype(o_ref.dtype)


def _norm_matmul(x, g, w, bias, *, li, col_blk0, col_stride=1, n_out, tm, tn, out_dtype, act, precise, name):
    m, d = x.shape
    w_spec = pl.BlockSpec((None, tn, d), lambda i, j: (li, col_blk0 + j * col_stride, 0))
    return pl.pallas_call(
        functools.partial(_norm_matmul_kernel, act=act, precise=precise),
        grid=(m // tm, n_out // tn),
        in_specs=[
            pl.BlockSpec((tm, d), lambda i, j: (i, 0)),
            pl.BlockSpec((1, d), lambda i, j: (0, 0)),
            w_spec,
            pl.BlockSpec((1, tn), lambda i, j: (0, j)),
        ],
        out_specs=pl.BlockSpec((tm, tn), lambda i, j: (i, j)),
        out_shape=jax.ShapeDtypeStruct((m, n_out), out_dtype),
        scratch_shapes=[pltpu.VMEM((tm, d), F32 if precise else BF16)],
        compiler_params=_cparams(("parallel", "arbitrary")),
        name=name,
    )(x, g, w, bias)


def _merge_kernel(att_ref, hg_ref, sa_ref, sb_ref, wpa_ref, wpm_ref, o_ref, *, precise):
    a = _mm(att_ref[...], wpa_ref[...], precise)
    m = _mm(hg_ref[...], wpm_ref[...], precise)
    o_ref[...] = (sa_ref[...].astype(F32) * a + sb_ref[...].astype(F32) * m).astype(o_ref.dtype)


def _merge(att, hg, sg, w_pa, w_pm, *, li, tm, tn, out_dtype, precise, name):
    m = att.shape[0]
    nb = D_MODEL // tn
    return pl.pallas_call(
        functools.partial(_merge_kernel, precise=precise),
        grid=(m // tm, nb),
        in_specs=[
            pl.BlockSpec((tm, ATT_WIDTH), lambda i, j: (i, 0)),
            pl.BlockSpec((tm, ML_HEADS * ML_V), lambda i, j: (i, 0)),
            pl.BlockSpec((tm, tn), lambda i, j: (i, j)),
            pl.BlockSpec((tm, tn), lambda i, j: (i, j + nb)),
            pl.BlockSpec((None, ATT_WIDTH, tn), lambda i, j: (li, 0, j)),
            pl.BlockSpec((None, ML_HEADS * ML_V, tn), lambda i, j: (li, 0, j)),
        ],
        out_specs=pl.BlockSpec((tm, tn), lambda i, j: (i, j)),
        out_shape=jax.ShapeDtypeStruct((m, D_MODEL), out_dtype),
        compiler_params=_cparams(("parallel", "arbitrary")),
        name=name,
    )(att, hg, sg, sg, w_pa, w_pm)


def _outproj_kernel(x_ref, mg_ref, w_ref, o_ref, *, precise):
    o_ref[...] = x_ref[...] + _mm(mg_ref[...], w_ref[...], precise)


def _outproj(x, mg, w_out, *, li, tm, tn, precise, name):
    m = x.shape[0]
    return pl.pallas_call(
        functools.partial(_outproj_kernel, precise=precise),
        grid=(m // tm, D_MODEL // tn),
        in_specs=[
            pl.BlockSpec((tm, tn), lambda i, j: (i, j)),
            pl.BlockSpec((tm, D_MODEL), lambda i, j: (i, 0)),
            pl.BlockSpec((None, D_MODEL, tn), lambda i, j: (li, 0, j)),
        ],
        out_specs=pl.BlockSpec((tm, tn), lambda i, j: (i, j)),
        out_shape=jax.ShapeDtypeStruct((m, D_MODEL), F32),
        compiler_params=_cparams(("parallel", "arbitrary")),
        name=name,
    )(x, mg, w_out)


def _rope_tables(pos):
    half = ROPE_DIM // 2
    inv = jnp.power(jnp.float32(ROPE_THETA), -jnp.arange(half, dtype=F32) / half)
    ang = pos.astype(F32)[:, None] * inv[None, :]
    n = pos.shape[0]
    cos_t = jnp.concatenate([jnp.cos(ang), jnp.cos(ang), jnp.ones((n, HEAD_DIM - ROPE_DIM), F32)], axis=-1)
    sin_t = jnp.concatenate([jnp.sin(ang), jnp.sin(ang), jnp.zeros((n, HEAD_DIM - ROPE_DIM), F32)], axis=-1)
    return cos_t, sin_t


def _rope_perm():
    half = ROPE_DIM // 2
    r = jnp.arange(HEAD_DIM)[:, None]
    c = jnp.arange(HEAD_DIM)[None, :]
    p = jnp.where((c < half) & (r == c + half), -1.0, 0.0) + jnp.where(
        (c >= half) & (c < ROPE_DIM) & (r == c - half), 1.0, 0.0)
    return p.astype(BF16)


ROPE_ROWS = 512


def _attn_prompt_kernel(cos_ref, sin_ref, perm_ref, *refs, seq):
    q_refs, k_refs, v_refs = refs[0:3], refs[3:6], refs[6:9]
    att_ref = refs[9]
    kc_refs, vc_refs = refs[10:13], refs[13:16]
    qs, ks, vs = refs[16:19]
    o_s, lse_s = refs[19:22], refs[22:25]
    perm = perm_ref[...]
    nk = ATT_GROUPS[0][0] // ATT_GROUPS[0][1]
    row = lax.broadcasted_iota(jnp.int32, (nk, nk), 0)
    col = lax.broadcasted_iota(jnp.int32, (nk, nk), 1)
    cur_ok = col <= row
    prev_ok = col >= row
    neg = jnp.float32(-jnp.inf)

    for g, (win, dil) in enumerate(ATT_GROUPS):
        assert win // dil == nk
        span = nk * dil
        shift = dil.bit_length() - 1

        def rope_body(c, carry, g=g):
            r0 = pl.multiple_of(c * ROPE_ROWS, ROPE_ROWS)
            cs = cos_ref[pl.ds(r0, ROPE_ROWS), :]
            sn = sin_ref[pl.ds(r0, ROPE_ROWS), :]
            qb = q_refs[g][0, pl.ds(r0, ROPE_ROWS), :]
            kb = k_refs[g][0, pl.ds(r0, ROPE_ROWS), :]
            qr = qb.astype(F32) * cs + jnp.dot(qb, perm, preferred_element_type=F32) * sn
            kr = kb.astype(F32) * cs + jnp.dot(kb, perm, preferred_element_type=F32) * sn
            qs[pl.ds(r0, ROPE_ROWS), :] = qr
            ks[pl.ds(r0, ROPE_ROWS), :] = kr
            vs[pl.ds(r0, ROPE_ROWS), :] = v_refs[g][0, pl.ds(r0, ROPE_ROWS), :].astype(F32)
            return carry

        lax.fori_loop(0, seq // ROPE_ROWS, rope_body, 0)
        keep = min(win, seq)
        kc_refs[g][0] = ks[seq - keep:seq, :]
        vc_refs[g][0] = vs[seq - keep:seq, :]

        def rows(start, dil=dil):
            if dil == 1:
                return pl.ds(pl.multiple_of(start, nk), nk)
            return pl.ds(start, nk, stride=dil)

        def blk_body(blk, carry, g=g, dil=dil, span=span, shift=shift, rows=rows):
            n = blk >> shift
            r = blk & (dil - 1)
            start = n * span + r
            has_prev = n > 0
            start_p = jnp.where(has_prev, start - span, start)
            q = qs[rows(start), :]
            k_c = ks[rows(start), :]
            k_p = ks[rows(start_p), :]
            v_c = vs[rows(start), :]
            v_p = vs[rows(start_p), :]
            s_c = jnp.where(cur_ok, _mm_nt(q, k_c) * ATT_SCALE, neg)
            no_prev = jnp.where(has_prev, jnp.float32(0.0), neg)
            s_p = jnp.where(prev_ok, _mm_nt(q, k_p) * ATT_SCALE + no_prev, neg)
            m = jnp.max(jnp.maximum(s_c, s_p), axis=-1, keepdims=True)
            p_c = jnp.exp(s_c - m)
            p_p = jnp.exp(s_p - m)
            l = jnp.sum(p_c + p_p, axis=-1, keepdims=True)
            acc = _mm(p_c, v_c) + _mm(p_p, v_p)
            o_s[g][rows(start), :] = acc * (1.0 / l)
            lse_s[g][rows(start), :] = jnp.broadcast_to(m + jnp.log(l), (nk, HEAD_DIM))
            return carry

        lax.fori_loop(0, seq // nk, blk_body, 0, unroll=4)

    def out_body(c, carry):
        sl = pl.ds(pl.multiple_of(c * nk, nk), nk)
        lse = [lse_s[g][sl, :] for g in range(N_ATT_GROUPS)]
        top = functools.reduce(jnp.maximum, lse)
        w = [jnp.exp(x - top) for x in lse]
        num = sum(w[g] * o_s[g][sl, :] for g in range(N_ATT_GROUPS))
        att_ref[0, sl, :] = (num / sum(w)).astype(att_ref.dtype)
        return carry

    lax.fori_loop(0, seq // nk, out_body, 0)


def _attn_prompt(u3, cos_t, sin_t, *, name):
    bsz, seq, _ = u3.shape
    perm = _rope_perm()
    in_specs = [
        pl.BlockSpec((seq, HEAD_DIM), lambda b, h: (0, 0)),
        pl.BlockSpec((seq, HEAD_DIM), lambda b, h: (0, 0)),
        pl.BlockSpec((HEAD_DIM, HEAD_DIM), lambda b, h: (0, 0)),
    ]
    for off in (O_AQ, O_AK, O_AV):
        for g in range(N_ATT_GROUPS):
            blk0 = off // HEAD_DIM + g * ATT_HEADS
            in_specs.append(pl.BlockSpec((1, seq, HEAD_DIM), lambda b, h, blk0=blk0: (b, 0, blk0 + h)))
    keeps = [min(w, seq) for w, _ in ATT_GROUPS]
    out_specs = [pl.BlockSpec((1, seq, HEAD_DIM), lambda b, h: (b, 0, h))]
    out_shape = [jax.ShapeDtypeStruct((bsz, seq, ATT_WIDTH), BF16)]
    for _ in range(2):
        for keep in keeps:
            out_specs.append(pl.BlockSpec((1, keep, HEAD_DIM), lambda b, h: (b, 0, h)))
            out_shape.append(jax.ShapeDtypeStruct((bsz, keep, ATT_WIDTH), F32))
    outs = pl.pallas_call(
        functools.partial(_attn_prompt_kernel, seq=seq),
        grid=(bsz, ATT_HEADS),
        in_specs=in_specs,
        out_specs=out_specs,
        out_shape=out_shape,
        scratch_shapes=[pltpu.VMEM((seq, HEAD_DIM), F32) for _ in range(3 + 2 * N_ATT_GROUPS)],
        compiler_params=_cparams(("parallel", "arbitrary")),
        name=name,
    )(cos_t, sin_t, perm, *([u3] * 9))
    return outs[0], outs[1:4], outs[4:7]


def _attn_sample_kernel(cos_ref, sin_ref, q_ref, k_ref, v_ref, c0_ref, c1_ref, c2_ref,
                        att_ref, kn_ref, *, n_new):
    cache_refs = (c0_ref, c1_ref, c2_ref)
    cs = cos_ref[...]
    sn = sin_ref[...]
    rows_pad = q_ref.shape[1]
    half = ROPE_DIM // 2
    lane = lax.broadcasted_iota(jnp.int32, (rows_pad, HEAD_DIM), 1)

    def rot_half(x):
        return jnp.where(lane < half, -pltpu.roll(x, HEAD_DIM - half, axis=1),
                         jnp.where(lane < ROPE_DIM, pltpu.roll(x, half, axis=1), 0.0))

    nk = ATT_GROUPS[0][0] // ATT_GROUPS[0][1]
    kv_rows = 2 * ATT_HEADS
    w_iota = lax.broadcasted_iota(jnp.int32, (nk, 1), 0)
    u_iota = lax.broadcasted_iota(jnp.int32, (rows_pad, 1), 0)
    neg = jnp.float32(-jnp.inf)
    att_ref[...] = jnp.zeros(att_ref.shape, att_ref.dtype)

    for h in range(ATT_HEADS):
        parts = [[None] * N_ATT_GROUPS for _ in range(n_new)]
        for g, (win, dil) in enumerate(ATT_GROUPS):
            hs = (g * ATT_HEADS + h) * HEAD_DIM
            qh = q_ref[0, :, hs:hs + HEAD_DIM]
            kh = k_ref[0, :, hs:hs + HEAD_DIM]
            vh = v_ref[0, :, hs:hs + HEAD_DIM]
            qr = _rb(qh * cs + rot_half(qh) * sn)
            kr = kh * cs + rot_half(kh) * sn
            kn_ref[0, :, hs:hs + HEAD_DIM] = kr
            kr = _rb(kr)
            vh = _rb(vh)
            cref = cache_refs[g]
            for t in range(n_new):
                q_t = qr[t:t + 1, :]
                if dil == 1:
                    p0 = 0
                else:
                    p0 = t
                k_c = _rb(cref[pl.ds(p0 * kv_rows + h, nk, stride=kv_rows * dil), :])
                v_c = _rb(cref[pl.ds(p0 * kv_rows + ATT_HEADS + h, nk, stride=kv_rows * dil), :])
                s_c = jnp.sum(k_c * q_t, axis=-1, keepdims=True) * ATT_SCALE
                s_n = jnp.sum(kr * q_t, axis=-1, keepdims=True) * ATT_SCALE
                if dil == 1:
                    s_c = jnp.where(w_iota >= t, s_c, neg)
                    s_n = jnp.where(u_iota <= t, s_n, neg)
                else:
                    s_n = jnp.where(u_iota == t, s_n, neg)
                m = jnp.maximum(jnp.max(s_c, axis=0, keepdims=True), jnp.max(s_n, axis=0, keepdims=True))
                p_c = jnp.exp(s_c - m)
                p_n = jnp.exp(s_n - m)
                l = jnp.sum(p_c, axis=0, keepdims=True) + jnp.sum(p_n, axis=0, keepdims=True)
                acc = jnp.sum(_rb(p_c) * v_c, axis=0, keepdims=True) + jnp.sum(_rb(p_n) * vh, axis=0, keepdims=True)
                parts[t][g] = (m, l, acc)
        for t in range(n_new):
            m_all = functools.reduce(jnp.maximum, [p[0] for p in parts[t]])
            num = sum(jnp.exp(p[0] - m_all) * p[2] for p in parts[t])
            den = sum(jnp.exp(p[0] - m_all) * p[1] for p in parts[t])
            att_ref[0, t:t + 1, h * HEAD_DIM:(h + 1) * HEAD_DIM] = num / den


def _attn_sample(u3, caches2d, cos_t, sin_t, *, li, n_new, name):
    bsz, rows_pad, _ = u3.shape
    qkv_w = N_ATT_GROUPS * ATT_WIDTH
    in_specs = [
        pl.BlockSpec((rows_pad, HEAD_DIM), lambda b: (0, 0)),
        pl.BlockSpec((rows_pad, HEAD_DIM), lambda b: (0, 0)),
        pl.BlockSpec((1, rows_pad, qkv_w), lambda b: (b, 0, O_AQ // qkv_w)),
        pl.BlockSpec((1, rows_pad, qkv_w), lambda b: (b, 0, O_AK // qkv_w)),
        pl.BlockSpec((1, rows_pad, qkv_w), lambda b: (b, 0, O_AV // qkv_w)),
    ]
    for (win, _), c in zip(ATT_GROUPS, caches2d):
        rows = c.shape[0] // (DEPTH * bsz)
        in_specs.append(pl.BlockSpec((rows, HEAD_DIM), lambda b, bsz=bsz: (li * bsz + b, 0)))
    return pl.pallas_call(
        functools.partial(_attn_sample_kernel, n_new=n_new),
        grid=(bsz,),
        in_specs=in_specs,
        out_specs=[
            pl.BlockSpec((1, rows_pad, ATT_WIDTH), lambda b: (b, 0, 0)),
            pl.BlockSpec((1, rows_pad, qkv_w), lambda b: (b, 0, 0)),
        ],
        out_shape=[
            jax.ShapeDtypeStruct((bsz, rows_pad, ATT_WIDTH), F32),
            jax.ShapeDtypeStruct((bsz, rows_pad, qkv_w), F32),
        ],
        compiler_params=_cparams(("parallel",)),
        name=name,
    )(cos_t, sin_t, u3, u3, u3, *caches2d)


ML_HPB = 2
ML_VPB = 2


def _mlstm_kernel(q_ref, k_ref, *refs, chunk, precise):
    nvb = ML_HPB // ML_VPB
    v_refs, mo_refs = refs[:nvb], refs[nvb:2 * nvb]
    gc_ref, gr_ref, mlg_ref, c0_ref, n0_ref, m0_ref, h_ref, c_out, n_out, m_out, ct_s, n_s, m_s = refs[2 * nvb:]
    s_idx = pl.program_id(2)
    n_s_blocks = pl.num_programs(2)
    sb = q_ref.shape[1]
    L = chunk

    @pl.when(s_idx == 0)
    def _():
        for hh in range(ML_HPB):
            ct_s[hh] = c0_ref[0, hh].T
            n_s[hh] = n0_ref[0, 0, hh:hh + 1, :]
            m_s[hh] = m0_ref[0, 0, hh:hh + 1, :]

    row = lax.broadcasted_iota(jnp.int32, (L, L), 0)
    col = lax.broadcasted_iota(jnp.int32, (L, L), 1)
    causal = row >= col
    neg = jnp.float32(-jnp.inf)

    def chunk_body(c, carry):
        r0 = pl.multiple_of(c * L, L)
        gc = gc_ref[0, 0, pl.ds(r0, L), :]
        gr = gr_ref[0, 0, :, pl.ds(r0, L)]
        for hh in range(ML_HPB):
            ig_col = gc[:, hh:hh + 1]
            lf_col = gc[:, ML_HPB + hh:ML_HPB + hh + 1]
            ig_row = gr[hh:hh + 1, :]
            lf_row = gr[ML_HPB + hh:ML_HPB + hh + 1, :]
            b_col = jnp.sum(jnp.where(causal, lf_row, 0.0), axis=1, keepdims=True)
            b_row = jnp.sum(jnp.where(row <= col, lf_col, 0.0), axis=0, keepdims=True)
            qf = q_ref[0, pl.ds(r0, L), hh * ML_QK:(hh + 1) * ML_QK]
            kf = k_ref[0, pl.ds(r0, L), hh * ML_QK:(hh + 1) * ML_QK]
            vcols = slice((hh % ML_VPB) * ML_V, (hh % ML_VPB + 1) * ML_V)
            vf = v_refs[hh // ML_VPB][0, pl.ds(r0, L), vcols]
            mo = mo_refs[hh // ML_VPB][0, pl.ds(r0, L), vcols].astype(F32)
            ct = ct_s[hh]
            n_row = n_s[hh]
            m_prev = m_s[hh][:, 0:1]
            inter = b_col + m_prev
            dm = jnp.where(causal, b_col - b_row + ig_row, neg)
            m_t = jnp.maximum(inter, jnp.max(dm, axis=1, keepdims=True))
            w_intra = jnp.exp(dm - m_t)
            w_inter = jnp.exp(inter - m_t)
            ks = kf.astype(F32) * ML_K_SCALE
            sw = _mm_nt(qf, ks, precise) * w_intra
            num = w_inter * _mm(qf, ct, precise) + _mm(sw, vf, precise)
            den = w_inter * jnp.sum(_rb(qf, precise) * _rb(n_row, precise), axis=1, keepdims=True) + jnp.sum(
                sw, axis=1, keepdims=True)
            hv = num / jnp.maximum(jnp.abs(den), jnp.exp(-m_t))
            hn = hv * lax.rsqrt(jnp.mean(hv * hv, axis=-1, keepdims=True) + EPS)
            hn = hn * mlg_ref[:, hh * ML_V:(hh + 1) * ML_V]
            h_ref[0, pl.ds(r0, L), hh * ML_V:(hh + 1) * ML_V] = (hn * _sigmoid(mo)).astype(h_ref.dtype)
            b_last = b_col[L - 1:L, :]
            m_new = m_t[L - 1:L, :]
            decay = jnp.exp(b_last + m_prev - m_new)
            wk_col = jnp.exp(b_last - b_col + ig_col - m_new)
            ct_s[hh] = decay * ct + _mm_tn(ks * wk_col, vf, precise)
            n_s[hh] = decay * n_row + jnp.sum(_rb(ks, precise) * _rb(wk_col, precise), axis=0, keepdims=True)
            m_s[hh] = jnp.broadcast_to(m_new, (1, LANES))
        return carry

    lax.fori_loop(0, sb // L, chunk_body, 0)

    @pl.when(s_idx == n_s_blocks - 1)
    def _():
        for hh in range(ML_HPB):
            c_out[0, hh] = ct_s[hh].T
            n_out[0, 0, hh:hh + 1, :] = n_s[hh]
            m_out[0, 0, hh:hh + 1, :] = m_s[hh]


def _mlstm(u3, gc, gr, mlg, c0, n0, m0, *, chunk, sb, out_dtype, precise, name):
    bsz, seq, _ = u3.shape
    hg_n = ML_HEADS // ML_HPB
    qk_w, v_w, vb_w = ML_HPB * ML_QK, ML_HPB * ML_V, ML_VPB * ML_V
    nvb = ML_HPB // ML_VPB
    v_specs = [pl.BlockSpec((1, sb, vb_w), lambda b, hg, s, j=j, o=off // vb_w: (b, s, o + hg * nvb + j))
               for off in (O_MV, O_MO) for j in range(nvb)]
    n0r = n0.reshape(bsz, hg_n, ML_HPB, ML_QK)
    m0r = jnp.broadcast_to(m0.reshape(bsz, hg_n, ML_HPB, 1), (bsz, hg_n, ML_HPB, LANES))
    state_spec = pl.BlockSpec((1, 1, ML_HPB, LANES), lambda b, hg, s: (b, hg, 0, 0))
    c_spec = pl.BlockSpec((1, ML_HPB, ML_V, ML_QK), lambda b, hg, s: (b, hg, 0, 0))
    h, c_new, n_new, m_new = pl.pallas_call(
        functools.partial(_mlstm_kernel, chunk=chunk, precise=precise),
        grid=(bsz, hg_n, seq // sb),
        in_specs=[
            pl.BlockSpec((1, sb, qk_w), lambda b, hg, s: (b, s, O_MQ // qk_w + hg)),
            pl.BlockSpec((1, sb, qk_w), lambda b, hg, s: (b, s, O_MK // qk_w + hg)),
            *v_specs,
            pl.BlockSpec((1, 1, sb, 2 * ML_HPB), lambda b, hg, s: (b, hg, s, 0)),
            pl.BlockSpec((1, 1, 2 * ML_HPB, sb), lambda b, hg, s: (b, hg, 0, s)),
            pl.BlockSpec((1, v_w), lambda b, hg, s: (0, hg)),
            c_spec, state_spec, state_spec,
        ],
        out_specs=[
            pl.BlockSpec((1, sb, v_w), lambda b, hg, s: (b, s, hg)),
            c_spec, state_spec, state_spec,
        ],
        out_shape=[
            jax.ShapeDtypeStruct((bsz, seq, ML_HEADS * ML_V), out_dtype),
            jax.ShapeDtypeStruct((bsz, ML_HEADS, ML_V, ML_QK), F32),
            jax.ShapeDtypeStruct((bsz, hg_n, ML_HPB, LANES), F32),
            jax.ShapeDtypeStruct((bsz, hg_n, ML_HPB, LANES), F32),
        ],
        scratch_shapes=[
            pltpu.VMEM((ML_HPB, ML_QK, ML_V), F32),
            pltpu.VMEM((ML_HPB, 1, ML_QK), F32),
            pltpu.VMEM((ML_HPB, 1, LANES), F32),
        ],
        compiler_params=_cparams(("parallel", "parallel", "arbitrary")),
        name=name,
    )(u3, u3, *([u3] * (2 * nvb)), gc, gr, mlg, c0, n0r, m0r)
    return h, c_new, n_new.reshape(bsz, ML_HEADS, ML_QK), m_new[..., 0].reshape(bsz, ML_HEADS)


def _gate_layouts(gates, bsz, seq, seq_pad):
    hg_n = ML_HEADS // ML_HPB
    ig = gates[:, :ML_HEADS].reshape(bsz, seq, ML_HEADS)
    lf = gates[:, ML_HEADS:2 * ML_HEADS].reshape(bsz, seq, ML_HEADS)
    if seq_pad > seq:
        ig = jnp.pad(ig, ((0, 0), (0, seq_pad - seq), (0, 0)), constant_values=-jnp.inf)
        lf = jnp.pad(lf, ((0, 0), (0, seq_pad - seq), (0, 0)))
    ig = ig.reshape(bsz, seq_pad, hg_n, ML_HPB).transpose(0, 2, 1, 3)
    lf = lf.reshape(bsz, seq_pad, hg_n, ML_HPB).transpose(0, 2, 1, 3)
    gc = jnp.concatenate([ig, lf], axis=-1)
    return gc, gc.transpose(0, 1, 3, 2)


def _router_kernel(x_ref, g_ref, w_ref, b_ref, xn_ref, route_ref, *, precise):
    xf = x_ref[...]
    r = lax.rsqrt(jnp.mean(xf * xf, axis=-1, keepdims=True) + EPS)
    xn = (xf * r) * g_ref[...]
    xn_ref[...] = xn.astype(xn_ref.dtype)
    logits = _mm(xn, w_ref[...], precise) + b_ref[...]
    lane = lax.broadcasted_iota(jnp.int32, logits.shape, 1).astype(F32)
    neg = jnp.float32(-jnp.inf)
    big = jnp.float32(LANES)
    gl = jnp.where(lane < N_GROUPS, logits, neg)
    g_max = jnp.max(gl, axis=-1, keepdims=True)
    g_val = 1.0 / jnp.sum(jnp.exp(gl - g_max), axis=-1, keepdims=True)
    g_idx = jnp.min(jnp.where(gl == g_max, lane, big), axis=-1, keepdims=True)
    lo = N_GROUPS + EXP_PER_GROUP * g_idx
    es = jnp.where((lane >= lo) & (lane < lo + EXP_PER_GROUP), logits, neg)
    t0 = jnp.max(es, axis=-1, keepdims=True)
    i0 = jnp.min(jnp.where(es == t0, lane, big), axis=-1, keepdims=True)
    es1 = jnp.where(lane == i0, neg, es)
    t1 = jnp.max(es1, axis=-1, keepdims=True)
    i1 = jnp.min(jnp.where(es1 == t1, lane, big), axis=-1, keepdims=True)
    e1 = jnp.exp(t1 - t0)
    w0 = g_val / (1.0 + e1)
    w1 = g_val * e1 / (1.0 + e1)
    route = jnp.where(lane == 0, i0 - N_GROUPS,
                      jnp.where(lane == 1, i1 - N_GROUPS,
                                jnp.where(lane == 2, w0, jnp.where(lane == 3, w1, 0.0))))
    route_ref[...] = route


def _router(x, g, w_r, b_r, *, tm, precise, name):
    m, d = x.shape
    return pl.pallas_call(
        functools.partial(_router_kernel, precise=precise),
        grid=(m // tm,),
        in_specs=[
            pl.BlockSpec((tm, d), lambda i: (i, 0)),
            pl.BlockSpec((1, d), lambda i: (0, 0)),
            pl.BlockSpec((d, LANES), lambda i: (0, 0)),
            pl.BlockSpec((1, LANES), lambda i: (0, 0)),
        ],
        out_specs=[pl.BlockSpec((tm, d), lambda i: (i, 0)), pl.BlockSpec((tm, LANES), lambda i: (i, 0))],
        out_shape=[jax.ShapeDtypeStruct((m, d), F32), jax.ShapeDtypeStruct((m, LANES), F32)],
        compiler_params=_cparams(("parallel",)),
        name=name,
    )(x, g, w_r, b_r)


def _ffn_kernel(te_ref, nv_ref, xs_ref, wr_ref, wg_ref, wu_ref, wd_ref, y_ref, wg_s, wu_s, wd_s, *, precise):
    i = pl.program_id(0)
    valid = i < nv_ref[0]
    prev = te_ref[jnp.maximum(i - 1, 0)]

    @pl.when(valid & ((i == 0) | (te_ref[i] != prev)))
    def _():
        wg_s[...] = wg_ref[...].astype(wg_s.dtype)
        wu_s[...] = wu_ref[...].astype(wu_s.dtype)
        wd_s[...] = wd_ref[...].astype(wd_s.dtype)

    @pl.when(valid)
    def _():
        xs = xs_ref[...]
        gt = _mm(xs, wg_s[...], precise)
        up = _mm(xs, wu_s[...], precise)
        y_ref[...] = _mm((gt * _sigmoid(gt) * up) * wr_ref[...], wd_s[...], precise)


def _ffn(xs, wrow, te, nv, w_gate, w_up, w_down, *, li, tm, precise, name):
    rows, d = xs.shape
    n_tiles = rows // tm
    f = EXPERT_FF
    wdt = F32 if precise else BF16

    def row_blk(i, te, nv):
        return (jnp.minimum(i, nv[0] - 1), 0)

    return pl.pallas_call(
        functools.partial(_ffn_kernel, precise=precise),
        grid_spec=pltpu.PrefetchScalarGridSpec(
            num_scalar_prefetch=2,
            grid=(n_tiles,),
            in_specs=[
                pl.BlockSpec((tm, d), row_blk),
                pl.BlockSpec((tm, 1), row_blk),
                pl.BlockSpec((None, d, f), lambda i, te, nv: (li, 0, te[i])),
                pl.BlockSpec((None, d, f), lambda i, te, nv: (li, 0, te[i])),
                pl.BlockSpec((None, f, d), lambda i, te, nv: (li, te[i], 0)),
            ],
            out_specs=pl.BlockSpec((tm, d), row_blk),
            scratch_shapes=[pltpu.VMEM((d, f), wdt), pltpu.VMEM((d, f), wdt), pltpu.VMEM((f, d), wdt)],
        ),
        out_shape=jax.ShapeDtypeStruct((rows, d), F32),
        compiler_params=_cparams(("arbitrary",)),
        name=name,
    )(te, nv, xs, wrow, w_gate, w_up, w_down)


def _rank_kernel(route_ref, offs_ref, dest_ref, run_s):
    @pl.when(pl.program_id(0) == 0)
    def _():
        run_s[...] = jnp.zeros(run_s.shape, run_s.dtype)

    route = route_ref[...]
    tm = route.shape[0]
    lane = lax.broadcasted_iota(jnp.int32, route.shape, 1).astype(F32)
    is0 = lane == route[:, 0:1]
    is1 = lane == route[:, 1:2]
    onehot = jnp.where(is0, 1.0, 0.0) + jnp.where(is1, 1.0, 0.0)
    r = lax.broadcasted_iota(jnp.int32, (tm, tm), 0)
    c = lax.broadcasted_iota(jnp.int32, (tm, tm), 1)
    earlier = jnp.where(c < r, 1.0, 0.0)
    before = _mm(earlier, onehot)
    base = offs_ref[...] + run_s[...] + before
    d0 = jnp.sum(jnp.where(is0, base, 0.0), axis=-1, keepdims=True)
    d1 = jnp.sum(jnp.where(is1, base, 0.0), axis=-1, keepdims=True)
    run_s[...] = run_s[...] + jnp.sum(onehot, axis=0, keepdims=True)
    dest_ref[...] = jnp.where(lane == 0.0, d0, jnp.where(lane == 1.0, d1, 0.0)).astype(jnp.int32)


def _rank(route, offs_row, *, tm, name):
    m = route.shape[0]
    return pl.pallas_call(
        _rank_kernel,
        grid=(m // tm,),
        in_specs=[pl.BlockSpec((tm, LANES), lambda i: (i, 0)), pl.BlockSpec((1, LANES), lambda i: (0, 0))],
        out_specs=pl.BlockSpec((tm, LANES), lambda i: (i, 0)),
        out_shape=jax.ShapeDtypeStruct((m, LANES), jnp.int32),
        scratch_shapes=[pltpu.VMEM((1, LANES), F32)],
        compiler_params=_cparams(("arbitrary",)),
        name=name,
    )(route, offs_row)


ROW_DMA_UNROLL = 8


def _row_copy(src, src_row, dst, dst_row, sem):
    return pltpu.make_async_copy(src.at[pl.ds(src_row, 1)], dst.at[pl.ds(dst_row, 1)], sem)


def _scatter_kernel(dest_ref, last_ref, xn_ref, xs_hbm, zero_s, sem, zsem, *, tile):
    i = pl.program_id(0)
    tm = xn_ref.shape[0]

    def zero_copy(e):
        return pltpu.make_async_copy(zero_s, xs_hbm.at[pl.ds(pl.multiple_of(last_ref[e], tile), tile)], zsem)

    @pl.when(i == 0)
    def _():
        zero_s[...] = jnp.zeros(zero_s.shape, zero_s.dtype)

        def z_start(e, carry):
            @pl.when(last_ref[e] >= 0)
            def _():
                zero_copy(e).start()
            return carry

        def z_wait(e, carry):
            @pl.when(last_ref[e] >= 0)
            def _():
                zero_copy(e).wait()
            return carry

        lax.fori_loop(0, N_EXPERTS, z_start, 0)
        lax.fori_loop(0, N_EXPERTS, z_wait, 0)

    base = i * (2 * tm)

    def start(r, carry):
        _row_copy(xn_ref, r, xs_hbm, dest_ref[base + 2 * r], sem).start()
        _row_copy(xn_ref, r, xs_hbm, dest_ref[base + 2 * r + 1], sem).start()
        return carry

    def wait(r, carry):
        _row_copy(xn_ref, r, xs_hbm, dest_ref[base + 2 * r], sem).wait()
        _row_copy(xn_ref, r, xs_hbm, dest_ref[base + 2 * r + 1], sem).wait()
        return carry

    lax.fori_loop(0, tm, start, 0, unroll=ROW_DMA_UNROLL)
    lax.fori_loop(0, tm, wait, 0, unroll=ROW_DMA_UNROLL)


def _scatter(dest, last_tile, xn, *, rows, tm, tile, name):
    m, d = xn.shape
    return pl.pallas_call(
        functools.partial(_scatter_kernel, tile=tile),
        grid_spec=pltpu.PrefetchScalarGridSpec(
            num_scalar_prefetch=2,
            grid=(m // tm,),
            in_specs=[pl.BlockSpec((tm, d), lambda i, dest, last: (i, 0))],
            out_specs=pl.BlockSpec(memory_space=pl.ANY),
            scratch_shapes=[pltpu.VMEM((tile, d), F32), pltpu.SemaphoreType.DMA(()), pltpu.SemaphoreType.DMA(())],
        ),
        out_shape=jax.ShapeDtypeStruct((rows, d), F32),
        compiler_params=_cparams(("arbitrary",)),
        name=name,
    )(dest, last_tile, xn)


def _combine_kernel(dest_ref, x_ref, wt_ref, y_hbm, o_ref, ya, yb, sem):
    i = pl.program_id(0)
    tm = x_ref.shape[0]
    base = i * (2 * tm)

    def start(r, carry):
        _row_copy(y_hbm, dest_ref[base + 2 * r], ya, r, sem).start()
        _row_copy(y_hbm, dest_ref[base + 2 * r + 1], yb, r, sem).start()
        return carry

    def wait(r, carry):
        _row_copy(y_hbm, dest_ref[base + 2 * r], ya, r, sem).wait()
        _row_copy(y_hbm, dest_ref[base + 2 * r + 1], yb, r, sem).wait()
        return carry

    lax.fori_loop(0, tm, start, 0, unroll=ROW_DMA_UNROLL)
    lax.fori_loop(0, tm, wait, 0, unroll=ROW_DMA_UNROLL)
    wt = wt_ref[...]
    o_ref[...] = x_ref[...] + (wt[:, 2:3] * ya[...] + wt[:, 3:4] * yb[...])


def _combine(dest, x, wts, y, *, tm, name):
    m, d = x.shape
    return pl.pallas_call(
        _combine_kernel,
        grid_spec=pltpu.PrefetchScalarGridSpec(
            num_scalar_prefetch=1,
            grid=(m // tm,),
            in_specs=[
                pl.BlockSpec((tm, d), lambda i, dest: (i, 0)),
                pl.BlockSpec((tm, LANES), lambda i, dest: (i, 0)),
                pl.BlockSpec(memory_space=pl.ANY),
            ],
            out_specs=pl.BlockSpec((tm, d), lambda i, dest: (i, 0)),
            scratch_shapes=[pltpu.VMEM((tm, d), F32), pltpu.VMEM((tm, d), F32), pltpu.SemaphoreType.DMA(())],
        ),
        out_shape=jax.ShapeDtypeStruct((m, d), F32),
        compiler_params=_cparams(("arbitrary",)),
        name=name,
    )(dest, x, wts, y)


def _moe(x, g, w_r, b_r, w_gate, w_up, w_down, *, li, tm_tok, tm_ffn, gate_before_down, precise, name):
    m = x.shape[0]
    xn, route = _router(x, g, w_r, b_r, tm=tm_tok, precise=precise, name=name + "_router")
    eid = route[:, 0:2].astype(jnp.int32)
    counts = jnp.sum((eid[:, :, None] == jnp.arange(N_EXPERTS, dtype=jnp.int32)).astype(jnp.int32), axis=(0, 1))
    padded = ((counts + tm_ffn - 1) // tm_ffn) * tm_ffn
    ends = jnp.cumsum(padded)
    offs = ends - padded
    rows = 2 * m + N_EXPERTS * tm_ffn
    tile_start = jnp.arange(rows // tm_ffn, dtype=jnp.int32) * tm_ffn
    te = jnp.minimum(jnp.sum((ends[None, :] <= tile_start[:, None]).astype(jnp.int32), axis=1), N_EXPERTS - 1)
    nv = (ends[-1] // tm_ffn).reshape(1)
    last_tile = jnp.where(padded > 0, ends - tm_ffn, -1)
    offs_row = jnp.pad(offs.astype(F32), (0, LANES - N_EXPERTS)).reshape(1, LANES)
    dest = _rank(route, offs_row, tm=tm_tok, name=name + "_rank")[:, 0:2].reshape(-1)
    xs = _scatter(dest, last_tile, xn, rows=rows, tm=tm_tok, tile=tm_ffn, name=name + "_scatter")
    if gate_before_down:
        wrow = jnp.zeros((rows, 1), F32).at[dest, 0].set(route[:, 2:4].reshape(-1))
        wts = jnp.ones_like(route)
    else:
        wrow = jnp.ones((rows, 1), F32)
        wts = route
    y = _ffn(xs, wrow, te, nv, w_gate, w_up, w_down, li=li, tm=tm_ffn, precise=precise, name=name + "_ffn")
    return _combine(dest, x, wts, y, tm=tm_tok, name=name + "_combine")


def _rmsnorm_kernel(x_ref, g_ref, o_ref):
    xf = x_ref[...]
    r = lax.rsqrt(jnp.mean(xf * xf, axis=-1, keepdims=True) + EPS)
    o_ref[...] = (xf * r) * g_ref[...]


def _rmsnorm(x, g, *, tm, name):
    m, d = x.shape
    return pl.pallas_call(
        _rmsnorm_kernel,
        grid=(m // tm,),
        in_specs=[pl.BlockSpec((tm, d), lambda i: (i, 0)), pl.BlockSpec((1, d), lambda i: (0, 0))],
        out_specs=pl.BlockSpec((tm, d), lambda i: (i, 0)),
        out_shape=jax.ShapeDtypeStruct((m, d), F32),
        compiler_params=_cparams(("parallel",)),
        name=name,
    )(x, g)


def _layer_weights(p, li):
    b_if = jnp.pad(p["b_if"][li], (0, 2 * LANES - 2 * ML_HEADS)).reshape(1, 2 * LANES)
    w_r = jnp.pad(jnp.concatenate([p["w_rg"][li], p["w_re"][li]], axis=1),
                  ((0, 0), (0, LANES - N_GROUPS - N_EXPERTS)))
    b_r = jnp.pad(jnp.concatenate([p["b_rg"][li], p["b_re"][li]]), (0, LANES - N_GROUPS - N_EXPERTS)).reshape(1, LANES)
    return b_if, w_r, b_r


def _run_trunk(x3, p, layer_w, *, caches, c0s, n0s, m0s, pos0):
    bsz, seq, d = x3.shape
    m = bsz * seq
    x = x3.reshape(m, d)
    precise = False
    if caches is None:
        act_dtype, tag = BF16, "p"
        tm, tn, tm_tok, tm_ffn, seq_pad = 1024, 512, 256, 256, seq
    else:
        act_dtype, tag = F32, "s"
        tm, tn, tm_tok, tm_ffn, seq_pad = m, 512, m, 16, 16
    cos_t, sin_t = _rope_tables(pos0 + jnp.arange(seq_pad, dtype=jnp.int32))
    zeros_bias = jnp.zeros((1, N_MAIN), F32)
    new_kv = [[] for _ in ATT_GROUPS]
    c_all, n_all, m_all = [], [], []
    for li in range(DEPTH):
        b_if, w_r, b_r = layer_w[li]
        g_attn = p["attn_norm_g"][li].reshape(1, d)
        common = dict(li=li, tm=tm, precise=precise)
        u = _norm_matmul(x, g_attn, p["w_in"], zeros_bias, col_blk0=0, n_out=N_MAIN, tn=TN_MAIN,
                         out_dtype=act_dtype, act="none", name=f"{tag}{li}_inproj", **common)
        n_sg = 2 * D_MODEL
        sg_main = _norm_matmul(x, g_attn, p["w_in"], zeros_bias, col_blk0=N_MAIN // tn, n_out=n_sg, tn=tn,
                               out_dtype=act_dtype, act="sigmoid", name=f"{tag}{li}_gateproj", **common)
        tail_blk = (N_MAIN + n_sg) // LANES
        gt = _norm_matmul(x, g_attn, p["w_in"], b_if, col_blk0=N_MAIN // LANES, col_stride=tail_blk - N_MAIN // LANES,
                          n_out=2 * LANES, tn=LANES, out_dtype=F32, act="gates_tail", name=f"{tag}{li}_ifproj",
                          **common)
        gates = gt[:, :LANES]
        n_gate_cols = 2 * ML_HEADS
        sg = jnp.concatenate([sg_main[:, n_gate_cols:], gt[:, LANES:LANES + n_gate_cols].astype(act_dtype)], axis=1)
        u3 = u.reshape(bsz, seq, N_MAIN)
        if seq_pad > seq:
            u3 = jnp.pad(u3, ((0, 0), (0, seq_pad - seq), (0, 0)))
        if caches is None:
            att3, kcs, vcs = _attn_prompt(u3, cos_t, sin_t, name=f"{tag}{li}_attn")
            for gi in range(N_ATT_GROUPS):
                keep = kcs[gi].shape[1]
                new_kv[gi].append(jnp.stack([kcs[gi].reshape(bsz, keep, ATT_HEADS, HEAD_DIM),
                                             vcs[gi].reshape(bsz, keep, ATT_HEADS, HEAD_DIM)], axis=2))
            att = att3.reshape(m, ATT_WIDTH)
        else:
            caches2d = [c.reshape(-1, HEAD_DIM) for c in caches]
            att3, kn3 = _attn_sample(u3, caches2d, cos_t, sin_t, li=li, n_new=seq, name=f"{tag}{li}_attn")
            att = att3[:, :seq].reshape(m, ATT_WIDTH)
            nh = N_ATT_GROUPS * ATT_HEADS
            k_new = kn3[:, :seq].reshape(bsz, seq, nh, HEAD_DIM)
            v_new = u3[:, :seq, O_AV:O_AV + nh * HEAD_DIM].reshape(bsz, seq, nh, HEAD_DIM)
            for gi in range(N_ATT_GROUPS):
                sl = slice(gi * ATT_HEADS, (gi + 1) * ATT_HEADS)
                new_kv[gi].append(jnp.stack([k_new[:, :, sl], v_new[:, :, sl]], axis=2))
        gc, gr = _gate_layouts(gates, bsz, seq, seq_pad)
        mlg = p["ml_norm_g"][li].reshape(1, ML_HEADS * ML_V)
        hg3, c_new, n_new, m_new = _mlstm(
            u3, gc, gr, mlg, c0s[li], n0s[li], m0s[li], chunk=min(128, seq_pad), sb=min(512, seq_pad),
            out_dtype=act_dtype, precise=precise, name=f"{tag}{li}_mlstm")
        hg = hg3[:, :seq].reshape(m, ML_HEADS * ML_V)
        c_all.append(c_new)
        n_all.append(n_new)
        m_all.append(m_new)
        merged = _merge(att, hg, sg, p["w_pa"], p["w_pm"], li=li, tm=tm, tn=tn, out_dtype=act_dtype,
                        precise=precise, name=f"{tag}{li}_merge")
        x = _outproj(x, merged, p["w_out"], li=li, tm=tm, tn=tn, precise=precise, name=f"{tag}{li}_outproj")
        x = _moe(x, p["ffn_norm_g"][li].reshape(1, d), w_r, b_r, p["w_gate"], p["w_up"], p["w_down"],
                 li=li, tm_tok=tm_tok, tm_ffn=tm_ffn, gate_before_down=caches is not None, precise=precise,
                 name=f"{tag}{li}_moe")
    y = _rmsnorm(x, p["final_norm_g"].reshape(1, d), tm=min(m, 512), name=f"{tag}_final_norm")
    new_kv = [jnp.stack(a) for a in new_kv]
    if caches is not None:
        new_kv = [jnp.concatenate([c[:, :, seq:], fresh], axis=2) for c, fresh in zip(caches, new_kv)]
    return (y.reshape(bsz, seq, d), new_kv, jnp.stack(c_all), jnp.stack(n_all), jnp.stack(m_all))


def kernel(x_prompt, x_sample, cache_kv_w128, cache_kv_w512, cache_kv_w2048, state_C, state_n, state_m,
           attn_norm_g, w_in, b_if, ml_norm_g, w_pa, w_pm, w_out, ffn_norm_g, w_rg, b_rg, w_re, b_re,
           w_gate, w_up, w_down, final_norm_g):
    p = dict(attn_norm_g=attn_norm_g, w_in=jnp.swapaxes(w_in, 1, 2), b_if=b_if, ml_norm_g=ml_norm_g, w_pa=w_pa, w_pm=w_pm,
             w_out=w_out, ffn_norm_g=ffn_norm_g, w_rg=w_rg, b_rg=b_rg, w_re=w_re, b_re=b_re,
             w_gate=w_gate, w_up=w_up, w_down=w_down, final_norm_g=final_norm_g)
    layer_w = [_layer_weights(p, li) for li in range(DEPTH)]
    bp = x_prompt.shape[0]
    c0 = jnp.zeros((DEPTH, bp, ML_HEADS, ML_V, ML_QK), F32)
    n0 = jnp.zeros((DEPTH, bp, ML_HEADS, ML_QK), F32)
    m0 = jnp.zeros((DEPTH, bp, ML_HEADS), F32)
    y_p, p_kv, p_c, p_n, p_m = _run_trunk(x_prompt, p, layer_w, caches=None,
                                          c0s=c0, n0s=n0, m0s=m0, pos0=0)
    y_s, s_kv, s_c, s_n, s_m = _run_trunk(x_sample, p, layer_w,
                                          caches=[cache_kv_w128, cache_kv_w512, cache_kv_w2048],
                                          c0s=state_C, n0s=state_n, m0s=state_m, pos0=PAST_LEN)
    return (y_p, y_s, p_kv[0], p_kv[1], p_kv[2], p_c, p_n, p_m,
            s_kv[0], s_kv[1], s_kv[2], s_c, s_n, s_m)
```

```python
import functools

import jax
import jax.numpy as jnp
from jax import lax
from jax.experimental import pallas as pl
from jax.experimental.pallas import tpu as pltpu

F32 = jnp.float32
BF16 = jnp.bfloat16

D_MODEL = 2048
DEPTH = 2
PAST_LEN = 16384
ATT_GROUPS = ((128, 1), (512, 4), (2048, 16))
N_ATT_GROUPS = 3
ATT_HEADS = 4
HEAD_DIM = 128
ATT_WIDTH = ATT_HEADS * HEAD_DIM
ATT_SCALE = HEAD_DIM ** -0.5
ROPE_DIM = HEAD_DIM // 4
ROPE_THETA = 500000.0
ML_HEADS = 8
ML_QK = 128
ML_V = 256
ML_K_SCALE = ML_QK ** -0.5
N_GROUPS = 4
EXP_PER_GROUP = 8
N_EXPERTS = N_GROUPS * EXP_PER_GROUP
EXPERT_FF = 256
EPS = 1e-6

O_AQ, O_AK, O_AV = 0, 1536, 3072
O_MQ, O_MK, O_MV, O_MO = 4608, 5632, 6656, 8704
O_MI, O_MF, O_GA, O_GB = 10752, 10760, 10768, 12816
N_MAIN = O_MI
LANES = 128
TN_MAIN = 768
VMEM_LIMIT_MB = 56


def _cparams(sem):
    return pltpu.CompilerParams(dimension_semantics=sem, vmem_limit_bytes=VMEM_LIMIT_MB * 1024 * 1024)


def _dot(a, b, dims, precise):
    dn = (dims, ((), ()))
    if precise:
        return lax.dot_general(a.astype(F32), b.astype(F32), dn, preferred_element_type=F32,
                               precision=lax.Precision.HIGHEST)
    return lax.dot_general(a.astype(BF16), b.astype(BF16), dn, preferred_element_type=F32)


def _mm(a, b, precise=False):
    return _dot(a, b, ((1,), (0,)), precise)


def _mm_nt(a, b, precise=False):
    return _dot(a, b, ((1,), (1,)), precise)


def _mm_tn(a, b, precise=False):
    return _dot(a, b, ((0,), (0,)), precise)


def _rb(x, precise=False):
    return x.astype(F32) if precise else x.astype(BF16).astype(F32)


def _sigmoid(z):
    return 1.0 / (1.0 + jnp.exp(-z))


def _log_sigmoid(z):
    return jnp.minimum(z, 0.0) - jnp.log1p(jnp.exp(-jnp.abs(z)))


def _norm_matmul_kernel(x_ref, g_ref, w_ref, b_ref, o_ref, xn_ref, *, act, precise):
    @pl.when(pl.program_id(1) == 0)
    def _():
        xf = x_ref[...]
        r = lax.rsqrt(jnp.mean(xf * xf, axis=-1, keepdims=True) + EPS)
        xn_ref[...] = ((xf * r) * g_ref[...]).astype(xn_ref.dtype)

    acc = _mm_nt(xn_ref[...], w_ref[...], precise)
    if act == "sigmoid":
        acc = _sigmoid(acc)
    elif act == "gates_tail":
        @pl.when(pl.program_id(1) == 0)
        def _():
            z = acc + b_ref[...]
            lane = lax.broadcasted_iota(jnp.int32, z.shape, 1)
            o_ref[...] = jnp.where(lane < ML_HEADS, z, _log_sigmoid(z)).astype(o_ref.dtype)

        @pl.when(pl.program_id(1) != 0)
        def _():
            o_ref[...] = _sigmoid(acc).astype(o_ref.dtype)
        return
    o_ref[...] = acc.astype(o_ref.dtype)


def _norm_matmul(x, g, w, bias, *, li, col_blk0, col_stride=1, n_out, tm, tn, out_dtype, act, precise, name):
    m, d = x.shape
    w_spec = pl.BlockSpec((None, tn, d), lambda i, j: (li, col_blk0 + j * col_stride, 0))
    return pl.pallas_call(
        functools.partial(_norm_matmul_kernel, act=act, precise=precise),
        grid=(m // tm, n_out // tn),
        in_specs=[
            pl.BlockSpec((tm, d), lambda i, j: (i, 0)),
            pl.BlockSpec((1, d), lambda i, j: (0, 0)),
            w_spec,
            pl.BlockSpec((1, tn), lambda i, j: (0, j)),
        ],
        out_specs=pl.BlockSpec((tm, tn), lambda i, j: (i, j)),
        out_shape=jax.ShapeDtypeStruct((m, n_out), out_dtype),
        scratch_shapes=[pltpu.VMEM((tm, d), F32 if precise else BF16)],
        compiler_params=_cparams(("parallel", "arbitrary")),
        name=name,
    )(x, g, w, bias)


def _merge_kernel(att_ref, hg_ref, sa_ref, sb_ref, wpa_ref, wpm_ref, o_ref, *, precise):
    a = _mm(att_ref[...], wpa_ref[...], precise)
    m = _mm(hg_ref[...], wpm_ref[...], precise)
    o_ref[...] = (sa_ref[...].astype(F32) * a + sb_ref[...].astype(F32) * m).astype(o_ref.dtype)


def _merge(att, hg, sg, w_pa, w_pm, *, li, tm, tn, out_dtype, precise, name):
    m = att.shape[0]
    nb = D_MODEL // tn
    return pl.pallas_call(
        functools.partial(_merge_kernel, precise=precise),
        grid=(m // tm, nb),
        in_specs=[
            pl.BlockSpec((tm, ATT_WIDTH), lambda i, j: (i, 0)),
            pl.BlockSpec((tm, ML_HEADS * ML_V), lambda i, j: (i, 0)),
            pl.BlockSpec((tm, tn), lambda i, j: (i, j)),
            pl.BlockSpec((tm, tn), lambda i, j: (i, j + nb)),
            pl.BlockSpec((None, ATT_WIDTH, tn), lambda i, j: (li, 0, j)),
            pl.BlockSpec((None, ML_HEADS * ML_V, tn), lambda i, j: (li, 0, j)),
        ],
        out_specs=pl.BlockSpec((tm, tn), lambda i, j: (i, j)),
        out_shape=jax.ShapeDtypeStruct((m, D_MODEL), out_dtype),
        compiler_params=_cparams(("parallel", "arbitrary")),
        name=name,
    )(att, hg, sg, sg, w_pa, w_pm)


def _outproj_kernel(x_ref, mg_ref, w_ref, o_ref, *, precise):
    o_ref[...] = x_ref[...] + _mm(mg_ref[...], w_ref[...], precise)


def _outproj(x, mg, w_out, *, li, tm, tn, precise, name):
    m = x.shape[0]
    return pl.pallas_call(
        functools.partial(_outproj_kernel, precise=precise),
        grid=(m // tm, D_MODEL // tn),
        in_specs=[
            pl.BlockSpec((tm, tn), lambda i, j: (i, j)),
            pl.BlockSpec((tm, D_MODEL), lambda i, j: (i, 0)),
            pl.BlockSpec((None, D_MODEL, tn), lambda i, j: (li, 0, j)),
        ],
        out_specs=pl.BlockSpec((tm, tn), lambda i, j: (i, j)),
        out_shape=jax.ShapeDtypeStruct((m, D_MODEL), F32),
        compiler_params=_cparams(("parallel", "arbitrary")),
        name=name,
    )(x, mg, w_out)


def _rope_tables(pos):
    half = ROPE_DIM // 2
    inv = jnp.power(jnp.float32(ROPE_THETA), -jnp.arange(half, dtype=F32) / half)
    ang = pos.astype(F32)[:, None] * inv[None, :]
    n = pos.shape[0]
    cos_t = jnp.concatenate([jnp.cos(ang), jnp.cos(ang), jnp.ones((n, HEAD_DIM - ROPE_DIM), F32)], axis=-1)
    sin_t = jnp.concatenate([jnp.sin(ang), jnp.sin(ang), jnp.zeros((n, HEAD_DIM - ROPE_DIM), F32)], axis=-1)
    return cos_t, sin_t


def _rope_perm():
    half = ROPE_DIM // 2
    r = jnp.arange(HEAD_DIM)[:, None]
    c = jnp.arange(HEAD_DIM)[None, :]
    p = jnp.where((c < half) & (r == c + half), -1.0, 0.0) + jnp.where(
        (c >= half) & (c < ROPE_DIM) & (r == c - half), 1.0, 0.0)
    return p.astype(BF16)


ROPE_ROWS = 512
ATT_BLOCKS_PER_TRIP = 4


def _attn_prompt_kernel(cos_ref, sin_ref, perm_ref, *refs, seq):
    q_refs, k_refs, v_refs = refs[0:3], refs[3:6], refs[6:9]
    att_ref = refs[9]
    kc_refs, vc_refs = refs[10:13], refs[13:16]
    qs, ks, vs = refs[16:19]
    o_s, lse_s = refs[19:22], refs[22:25]
    perm = perm_ref[...]
    nk = ATT_GROUPS[0][0] // ATT_GROUPS[0][1]
    row = lax.broadcasted_iota(jnp.int32, (nk, nk), 0)
    col = lax.broadcasted_iota(jnp.int32, (nk, nk), 1)
    cur_ok = col <= row
    prev_ok = col >= row
    neg = jnp.float32(-jnp.inf)

    for g, (win, dil) in enumerate(ATT_GROUPS):
        assert win // dil == nk
        span = nk * dil
        shift = dil.bit_length() - 1

        def rope_body(c, carry, g=g):
            r0 = pl.multiple_of(c * ROPE_ROWS, ROPE_ROWS)
            cs = cos_ref[pl.ds(r0, ROPE_ROWS), :]
            sn = sin_ref[pl.ds(r0, ROPE_ROWS), :]
            qb = q_refs[g][0, pl.ds(r0, ROPE_ROWS), :]
            kb = k_refs[g][0, pl.ds(r0, ROPE_ROWS), :]
            qr = qb.astype(F32) * cs + jnp.dot(qb, perm, preferred_element_type=F32) * sn
            kr = kb.astype(F32) * cs + jnp.dot(kb, perm, preferred_element_type=F32) * sn
            qs[pl.ds(r0, ROPE_ROWS), :] = qr
            ks[pl.ds(r0, ROPE_ROWS), :] = kr
            vs[pl.ds(r0, ROPE_ROWS), :] = v_refs[g][0, pl.ds(r0, ROPE_ROWS), :].astype(F32)
            return carry

        lax.fori_loop(0, seq // ROPE_ROWS, rope_body, 0)
        keep = min(win, seq)
        kc_refs[g][0] = ks[seq - keep:seq, :]
        vc_refs[g][0] = vs[seq - keep:seq, :]

        def rows(start, dil=dil):
            if dil == 1:
                return pl.ds(pl.multiple_of(start, nk), nk)
            return pl.ds(start, nk, stride=dil)

        def blk_body(it, carry, g=g, dil=dil, span=span, shift=shift, rows=rows):
            blks = [it * ATT_BLOCKS_PER_TRIP + u for u in range(ATT_BLOCKS_PER_TRIP)]
            n = [blk >> shift for blk in blks]
            start = [n_u * span + (blk & (dil - 1)) for n_u, blk in zip(n, blks)]
            has_prev = [n_u > 0 for n_u in n]
            start_p = [jnp.where(hp, st - span, st) for hp, st in zip(has_prev, start)]
            q = [qs[rows(st), :] for st in start]
            s_c = [jnp.where(cur_ok, _mm_nt(q_u, ks[rows(st), :]) * ATT_SCALE, neg) for q_u, st in zip(q, start)]
            no_prev = [jnp.where(hp, jnp.float32(0.0), neg) for hp in has_prev]
            s_p = [jnp.where(prev_ok, _mm_nt(q_u, ks[rows(sp), :]) * ATT_SCALE + np_u, neg)
                   for q_u, sp, np_u in zip(q, start_p, no_prev)]
            m = [jnp.max(jnp.maximum(c_u, p_u), axis=-1, keepdims=True) for c_u, p_u in zip(s_c, s_p)]
            p_c = [jnp.exp(c_u - m_u) for c_u, m_u in zip(s_c, m)]
            p_p = [jnp.exp(p_u - m_u) for p_u, m_u in zip(s_p, m)]
            l = [jnp.sum(c_u + p_u, axis=-1, keepdims=True) for c_u, p_u in zip(p_c, p_p)]
            acc = [_mm(c_u, vs[rows(st), :]) + _mm(p_u, vs[rows(sp), :])
                   for c_u, p_u, st, sp in zip(p_c, p_p, start, start_p)]
            for u, st in enumerate(start):
                o_s[g][rows(st), :] = acc[u] * (1.0 / l[u])
                lse_s[g][rows(st), :] = jnp.broadcast_to(m[u] + jnp.log(l[u]), (nk, HEAD_DIM))
            return carry

        lax.fori_loop(0, seq // nk // ATT_BLOCKS_PER_TRIP, blk_body, 0)

    def out_body(c, carry):
        sl = pl.ds(pl.multiple_of(c * nk, nk), nk)
        lse = [lse_s[g][sl, :] for g in range(N_ATT_GROUPS)]
        top = functools.reduce(jnp.maximum, lse)
        w = [jnp.exp(x - top) for x in lse]
        num = sum(w[g] * o_s[g][sl, :] for g in range(N_ATT_GROUPS))
        att_ref[0, sl, :] = (num / sum(w)).astype(att_ref.dtype)
        return carry

    lax.fori_loop(0, seq // nk, out_body, 0)


def _attn_prompt(u3, cos_t, sin_t, *, name):
    bsz, seq, _ = u3.shape
    perm = _rope_perm()
    in_specs = [
        pl.BlockSpec((seq, HEAD_DIM), lambda b, h: (0, 0)),
        pl.BlockSpec((seq, HEAD_DIM), lambda b, h: (0, 0)),
        pl.BlockSpec((HEAD_DIM, HEAD_DIM), lambda b, h: (0, 0)),
    ]
    for off in (O_AQ, O_AK, O_AV):
        for g in range(N_ATT_GROUPS):
            blk0 = off // HEAD_DIM + g * ATT_HEADS
            in_specs.append(pl.BlockSpec((1, seq, HEAD_DIM), lambda b, h, blk0=blk0: (b, 0, blk0 + h)))
    keeps = [min(w, seq) for w, _ in ATT_GROUPS]
    out_specs = [pl.BlockSpec((1, seq, HEAD_DIM), lambda b, h: (b, 0, h))]
    out_shape = [jax.ShapeDtypeStruct((bsz, seq, ATT_WIDTH), BF16)]
    for _ in range(2):
        for keep in keeps:
            out_specs.append(pl.BlockSpec((1, keep, HEAD_DIM), lambda b, h: (b, 0, h)))
            out_shape.append(jax.ShapeDtypeStruct((bsz, keep, ATT_WIDTH), F32))
    outs = pl.pallas_call(
        functools.partial(_attn_prompt_kernel, seq=seq),
        grid=(bsz, ATT_HEADS),
        in_specs=in_specs,
        out_specs=out_specs,
        out_shape=out_shape,
        scratch_shapes=[pltpu.VMEM((seq, HEAD_DIM), F32) for _ in range(3 + 2 * N_ATT_GROUPS)],
        compiler_params=_cparams(("parallel", "arbitrary")),
        name=name,
    )(cos_t, sin_t, perm, *([u3] * 9))
    return outs[0], outs[1:4], outs[4:7]


def _attn_sample_kernel(cos_ref, sin_ref, q_ref, k_ref, v_ref, c0_ref, c1_ref, c2_ref,
                        att_ref, kn_ref, *, n_new):
    cache_refs = (c0_ref, c1_ref, c2_ref)
    cs = cos_ref[...]
    sn = sin_ref[...]
    rows_pad = q_ref.shape[1]
    half = ROPE_DIM // 2
    lane = lax.broadcasted_iota(jnp.int32, (rows_pad, HEAD_DIM), 1)

    def rot_half(x):
        return jnp.where(lane < half, -pltpu.roll(x, HEAD_DIM - half, axis=1),
                         jnp.where(lane < ROPE_DIM, pltpu.roll(x, half, axis=1), 0.0))

    nk = ATT_GROUPS[0][0] // ATT_GROUPS[0][1]
    kv_rows = 2 * ATT_HEADS
    w_iota = lax.broadcasted_iota(jnp.int32, (nk, 1), 0)
    u_iota = lax.broadcasted_iota(jnp.int32, (rows_pad, 1), 0)
    neg = jnp.float32(-jnp.inf)
    att_ref[...] = jnp.zeros(att_ref.shape, att_ref.dtype)

    for h in range(ATT_HEADS):
        parts = [[None] * N_ATT_GROUPS for _ in range(n_new)]
        for g, (win, dil) in enumerate(ATT_GROUPS):
            hs = (g * ATT_HEADS + h) * HEAD_DIM
            qh = q_ref[0, :, hs:hs + HEAD_DIM]
            kh = k_ref[0, :, hs:hs + HEAD_DIM]
            vh = v_ref[0, :, hs:hs + HEAD_DIM]
            qr = _rb(qh * cs + rot_half(qh) * sn)
            kr = kh * cs + rot_half(kh) * sn
            kn_ref[0, :, hs:hs + HEAD_DIM] = kr
            kr = _rb(kr)
            vh = _rb(vh)
            cref = cache_refs[g]
            for t in range(n_new):
                q_t = qr[t:t + 1, :]
                if dil == 1:
                    p0 = 0
                else:
                    p0 = t
                k_c = _rb(cref[pl.ds(p0 * kv_rows + h, nk, stride=kv_rows * dil), :])
                v_c = _rb(cref[pl.ds(p0 * kv_rows + ATT_HEADS + h, nk, stride=kv_rows * dil), :])
                s_c = jnp.sum(k_c * q_t, axis=-1, keepdims=True) * ATT_SCALE
                s_n = jnp.sum(kr * q_t, axis=-1, keepdims=True) * ATT_SCALE
                if dil == 1:
                    s_c = jnp.where(w_iota >= t, s_c, neg)
                    s_n = jnp.where(u_iota <= t, s_n, neg)
                else:
                    s_n = jnp.where(u_iota == t, s_n, neg)
                m = jnp.maximum(jnp.max(s_c, axis=0, keepdims=True), jnp.max(s_n, axis=0, keepdims=True))
                p_c = jnp.exp(s_c - m)
                p_n = jnp.exp(s_n - m)
                l = jnp.sum(p_c, axis=0, keepdims=True) + jnp.sum(p_n, axis=0, keepdims=True)
                acc = jnp.sum(_rb(p_c) * v_c, axis=0, keepdims=True) + jnp.sum(_rb(p_n) * vh, axis=0, keepdims=True)
                parts[t][g] = (m, l, acc)
        for t in range(n_new):
            m_all = functools.reduce(jnp.maximum, [p[0] for p in parts[t]])
            num = sum(jnp.exp(p[0] - m_all) * p[2] for p in parts[t])
            den = sum(jnp.exp(p[0] - m_all) * p[1] for p in parts[t])
            att_ref[0, t:t + 1, h * HEAD_DIM:(h + 1) * HEAD_DIM] = num / den


def _attn_sample(u3, caches2d, cos_t, sin_t, *, li, n_new, name):
    bsz, rows_pad, _ = u3.shape
    qkv_w = N_ATT_GROUPS * ATT_WIDTH
    in_specs = [
        pl.BlockSpec((rows_pad, HEAD_DIM), lambda b: (0, 0)),
        pl.BlockSpec((rows_pad, HEAD_DIM), lambda b: (0, 0)),
        pl.BlockSpec((1, rows_pad, qkv_w), lambda b: (b, 0, O_AQ // qkv_w)),
        pl.BlockSpec((1, rows_pad, qkv_w), lambda b: (b, 0, O_AK // qkv_w)),
        pl.BlockSpec((1, rows_pad, qkv_w), lambda b: (b, 0, O_AV // qkv_w)),
    ]
    for (win, _), c in zip(ATT_GROUPS, caches2d):
        rows = c.shape[0] // (DEPTH * bsz)
        in_specs.append(pl.BlockSpec((rows, HEAD_DIM), lambda b, bsz=bsz: (li * bsz + b, 0)))
    return pl.pallas_call(
        functools.partial(_attn_sample_kernel, n_new=n_new),
        grid=(bsz,),
        in_specs=in_specs,
        out_specs=[
            pl.BlockSpec((1, rows_pad, ATT_WIDTH), lambda b: (b, 0, 0)),
            pl.BlockSpec((1, rows_pad, qkv_w), lambda b: (b, 0, 0)),
        ],
        out_shape=[
            jax.ShapeDtypeStruct((bsz, rows_pad, ATT_WIDTH), F32),
            jax.ShapeDtypeStruct((bsz, rows_pad, qkv_w), F32),
        ],
        compiler_params=_cparams(("parallel",)),
        name=name,
    )(cos_t, sin_t, u3, u3, u3, *caches2d)


ML_HPB = 4
ML_VPB = 2


def _mlstm_kernel(q_ref, k_ref, *refs, chunk, precise):
    nvb = ML_HPB // ML_VPB
    v_refs, mo_refs = refs[:nvb], refs[nvb:2 * nvb]
    gc_ref, gr_ref, mlg_ref, c0_ref, n0_ref, m0_ref, h_ref, c_out, n_out, m_out, ct_s, n_s, m_s = refs[2 * nvb:]
    s_idx = pl.program_id(2)
    n_s_blocks = pl.num_programs(2)
    sb = q_ref.shape[1]
    L = chunk

    @pl.when(s_idx == 0)
    def _():
        for hh in range(ML_HPB):
            ct_s[hh] = c0_ref[0, hh].T
            n_s[hh] = n0_ref[0, 0, hh:hh + 1, :]
            m_s[hh] = m0_ref[0, 0, hh:hh + 1, :]

    row = lax.broadcasted_iota(jnp.int32, (L, L), 0)
    col = lax.broadcasted_iota(jnp.int32, (L, L), 1)
    causal = row >= col
    neg = jnp.float32(-jnp.inf)

    def chunk_body(c, carry):
        r0 = pl.multiple_of(c * L, L)
        gc = gc_ref[0, 0, pl.ds(r0, L), :]
        gr = gr_ref[0, 0, :, pl.ds(r0, L)]
        heads = range(ML_HPB)
        ig_col = [gc[:, hh:hh + 1] for hh in heads]
        lf_col = [gc[:, ML_HPB + hh:ML_HPB + hh + 1] for hh in heads]
        ig_row = [gr[hh:hh + 1, :] for hh in heads]
        lf_row = [gr[ML_HPB + hh:ML_HPB + hh + 1, :] for hh in heads]
        vcols = [slice((hh % ML_VPB) * ML_V, (hh % ML_VPB + 1) * ML_V) for hh in heads]
        qf = [q_ref[0, pl.ds(r0, L), hh * ML_QK:(hh + 1) * ML_QK] for hh in heads]
        ks = [k_ref[0, pl.ds(r0, L), hh * ML_QK:(hh + 1) * ML_QK].astype(F32) * ML_K_SCALE for hh in heads]
        vf = [v_refs[hh // ML_VPB][0, pl.ds(r0, L), vcols[hh]] for hh in heads]
        ct = [ct_s[hh] for hh in heads]
        n_row = [n_s[hh] for hh in heads]
        m_prev = [m_s[hh][:, 0:1] for hh in heads]
        qk = [_mm_nt(qf[hh], ks[hh], precise) for hh in heads]
        q_ct = [_mm(qf[hh], ct[hh], precise) for hh in heads]
        b_col = [jnp.sum(jnp.where(causal, lf_row[hh], 0.0), axis=1, keepdims=True) for hh in heads]
        b_row = [jnp.sum(jnp.where(row <= col, lf_col[hh], 0.0), axis=0, keepdims=True) for hh in heads]
        inter = [b_col[hh] + m_prev[hh] for hh in heads]
        dm = [jnp.where(causal, b_col[hh] - b_row[hh] + ig_row[hh], neg) for hh in heads]
        m_t = [jnp.maximum(inter[hh], jnp.max(dm[hh], axis=1, keepdims=True)) for hh in heads]
        w_inter = [jnp.exp(inter[hh] - m_t[hh]) for hh in heads]
        sw = [qk[hh] * jnp.exp(dm[hh] - m_t[hh]) for hh in heads]
        num = [w_inter[hh] * q_ct[hh] + _mm(sw[hh], vf[hh], precise) for hh in heads]
        den = [w_inter[hh] * jnp.sum(_rb(qf[hh], precise) * _rb(n_row[hh], precise), axis=1, keepdims=True)
               + jnp.sum(sw[hh], axis=1, keepdims=True) for hh in heads]
        hv = [num[hh] / jnp.maximum(jnp.abs(den[hh]), jnp.exp(-m_t[hh])) for hh in heads]
        hn = [hv[hh] * lax.rsqrt(jnp.mean(hv[hh] * hv[hh], axis=-1, keepdims=True) + EPS) for hh in heads]
        for hh in heads:
            mo = mo_refs[hh // ML_VPB][0, pl.ds(r0, L), vcols[hh]].astype(F32)
            gain = mlg_ref[:, hh * ML_V:(hh + 1) * ML_V]
            h_ref[0, pl.ds(r0, L), hh * ML_V:(hh + 1) * ML_V] = (hn[hh] * gain * _sigmoid(mo)).astype(h_ref.dtype)
        b_last = [b_col[hh][L - 1:L, :] for hh in heads]
        m_new = [m_t[hh][L - 1:L, :] for hh in heads]
        decay = [jnp.exp(b_last[hh] + m_prev[hh] - m_new[hh]) for hh in heads]
        wk_col = [jnp.exp(b_last[hh] - b_col[hh] + ig_col[hh] - m_new[hh]) for hh in heads]
        upd = [_mm_tn(ks[hh] * wk_col[hh], vf[hh], precise) for hh in heads]
        for hh in heads:
            ct_s[hh] = decay[hh] * ct[hh] + upd[hh]
            n_s[hh] = decay[hh] * n_row[hh] + jnp.sum(_rb(ks[hh], precise) * _rb(wk_col[hh], precise), axis=0,
                                                      keepdims=True)
            m_s[hh] = jnp.broadcast_to(m_new[hh], (1, LANES))
        return carry

    lax.fori_loop(0, sb // L, chunk_body, 0)

    @pl.when(s_idx == n_s_blocks - 1)
    def _():
        for hh in range(ML_HPB):
            c_out[0, hh] = ct_s[hh].T
            n_out[0, 0, hh:hh + 1, :] = n_s[hh]
            m_out[0, 0, hh:hh + 1, :] = m_s[hh]


def _mlstm(u3, gc, gr, mlg, c0, n0, m0, *, chunk, sb, out_dtype, precise, name):
    bsz, seq, _ = u3.shape
    hg_n = ML_HEADS // ML_HPB
    qk_w, v_w, vb_w = ML_HPB * ML_QK, ML_HPB * ML_V, ML_VPB * ML_V
    nvb = ML_HPB // ML_VPB
    v_specs = [pl.BlockSpec((1, sb, vb_w), lambda b, hg, s, j=j, o=off // vb_w: (b, s, o + hg * nvb + j))
               for off in (O_MV, O_MO) for j in range(nvb)]
    n0r = n0.reshape(bsz, hg_n, ML_HPB, ML_QK)
    m0r = jnp.broadcast_to(m0.reshape(bsz, hg_n, ML_HPB, 1), (bsz, hg_n, ML_HPB, LANES))
    state_spec = pl.BlockSpec((1, 1, ML_HPB, LANES), lambda b, hg, s: (b, hg, 0, 0))
    c_spec = pl.BlockSpec((1, ML_HPB, ML_V, ML_QK), lambda b, hg, s: (b, hg, 0, 0))
    h, c_new, n_new, m_new = pl.pallas_call(
        functools.partial(_mlstm_kernel, chunk=chunk, precise=precise),
        grid=(bsz, hg_n, seq // sb),
        in_specs=[
            pl.BlockSpec((1, sb, qk_w), lambda b, hg, s: (b, s, O_MQ // qk_w + hg)),
            pl.BlockSpec((1, sb, qk_w), lambda b, hg, s: (b, s, O_MK // qk_w + hg)),
            *v_specs,
            pl.BlockSpec((1, 1, sb, 2 * ML_HPB), lambda b, hg, s: (b, hg, s, 0)),
            pl.BlockSpec((1, 1, 2 * ML_HPB, sb), lambda b, hg, s: (b, hg, 0, s)),
            pl.BlockSpec((1, v_w), lambda b, hg, s: (0, hg)),
            c_spec, state_spec, state_spec,
        ],
        out_specs=[
            pl.BlockSpec((1, sb, v_w), lambda b, hg, s: (b, s, hg)),
            c_spec, state_spec, state_spec,
        ],
        out_shape=[
            jax.ShapeDtypeStruct((bsz, seq, ML_HEADS * ML_V), out_dtype),
            jax.ShapeDtypeStruct((bsz, ML_HEADS, ML_V, ML_QK), F32),
            jax.ShapeDtypeStruct((bsz, hg_n, ML_HPB, LANES), F32),
            jax.ShapeDtypeStruct((bsz, hg_n, ML_HPB, LANES), F32),
        ],
        scratch_shapes=[
            pltpu.VMEM((ML_HPB, ML_QK, ML_V), F32),
            pltpu.VMEM((ML_HPB, 1, ML_QK), F32),
            pltpu.VMEM((ML_HPB, 1, LANES), F32),
        ],
        compiler_params=_cparams(("parallel", "parallel", "arbitrary")),
        name=name,
    )(u3, u3, *([u3] * (2 * nvb)), gc, gr, mlg, c0, n0r, m0r)
    return h, c_new, n_new.reshape(bsz, ML_HEADS, ML_QK), m_new[..., 0].reshape(bsz, ML_HEADS)


def _gate_layouts(gates, bsz, seq, seq_pad):
    hg_n = ML_HEADS // ML_HPB
    ig = gates[:, :ML_HEADS].reshape(bsz, seq, ML_HEADS)
    lf = gates[:, ML_HEADS:2 * ML_HEADS].reshape(bsz, seq, ML_HEADS)
    if seq_pad > seq:
        ig = jnp.pad(ig, ((0, 0), (0, seq_pad - seq), (0, 0)), constant_values=-jnp.inf)
        lf = jnp.pad(lf, ((0, 0), (0, seq_pad - seq), (0, 0)))
    ig = ig.reshape(bsz, seq_pad, hg_n, ML_HPB).transpose(0, 2, 1, 3)
    lf = lf.reshape(bsz, seq_pad, hg_n, ML_HPB).transpose(0, 2, 1, 3)
    gc = jnp.concatenate([ig, lf], axis=-1)
    return gc, gc.transpose(0, 1, 3, 2)


def _router_kernel(x_ref, g_ref, w_ref, b_ref, xn_ref, route_ref, *, precise):
    xf = x_ref[...]
    r = lax.rsqrt(jnp.mean(xf * xf, axis=-1, keepdims=True) + EPS)
    xn = (xf * r) * g_ref[...]
    xn_ref[...] = xn.astype(xn_ref.dtype)
    logits = _mm(xn, w_ref[...], precise) + b_ref[...]
    lane = lax.broadcasted_iota(jnp.int32, logits.shape, 1).astype(F32)
    neg = jnp.float32(-jnp.inf)
    big = jnp.float32(LANES)
    gl = jnp.where(lane < N_GROUPS, logits, neg)
    g_max = jnp.max(gl, axis=-1, keepdims=True)
    g_val = 1.0 / jnp.sum(jnp.exp(gl - g_max), axis=-1, keepdims=True)
    g_idx = jnp.min(jnp.where(gl == g_max, lane, big), axis=-1, keepdims=True)
    lo = N_GROUPS + EXP_PER_GROUP * g_idx
    es = jnp.where((lane >= lo) & (lane < lo + EXP_PER_GROUP), logits, neg)
    t0 = jnp.max(es, axis=-1, keepdims=True)
    i0 = jnp.min(jnp.where(es == t0, lane, big), axis=-1, keepdims=True)
    es1 = jnp.where(lane == i0, neg, es)
    t1 = jnp.max(es1, axis=-1, keepdims=True)
    i1 = jnp.min(jnp.where(es1 == t1, lane, big), axis=-1, keepdims=True)
    e1 = jnp.exp(t1 - t0)
    w0 = g_val / (1.0 + e1)
    w1 = g_val * e1 / (1.0 + e1)
    route = jnp.where(lane == 0, i0 - N_GROUPS,
                      jnp.where(lane == 1, i1 - N_GROUPS,
                                jnp.where(lane == 2, w0, jnp.where(lane == 3, w1, 0.0))))
    route_ref[...] = route


def _router(x, g, w_r, b_r, *, tm, precise, name):
    m, d = x.shape
    return pl.pallas_call(
        functools.partial(_router_kernel, precise=precise),
        grid=(m // tm,),
        in_specs=[
            pl.BlockSpec((tm, d), lambda i: (i, 0)),
            pl.BlockSpec((1, d), lambda i: (0, 0)),
            pl.BlockSpec((d, LANES), lambda i: (0, 0)),
            pl.BlockSpec((1, LANES), lambda i: (0, 0)),
        ],
        out_specs=[pl.BlockSpec((tm, d), lambda i: (i, 0)), pl.BlockSpec((tm, LANES), lambda i: (i, 0))],
        out_shape=[jax.ShapeDtypeStruct((m, d), F32), jax.ShapeDtypeStruct((m, LANES), F32)],
        compiler_params=_cparams(("parallel",)),
        name=name,
    )(x, g, w_r, b_r)


def _ffn_kernel(te_ref, nv_ref, xs_ref, wr_ref, wg_ref, wu_ref, wd_ref, y_ref, wg_s, wu_s, wd_s, *, precise):
    i = pl.program_id(0)
    valid = i < nv_ref[0]
    prev = te_ref[jnp.maximum(i - 1, 0)]

    @pl.when(valid & ((i == 0) | (te_ref[i] != prev)))
    def _():
        wg_s[...] = wg_ref[...].astype(wg_s.dtype)
        wu_s[...] = wu_ref[...].astype(wu_s.dtype)
        wd_s[...] = wd_ref[...].astype(wd_s.dtype)

    @pl.when(valid)
    def _():
        xs = xs_ref[...]
        gt = _mm(xs, wg_s[...], precise)
        up = _mm(xs, wu_s[...], precise)
        y_ref[...] = _mm((gt * _sigmoid(gt) * up) * wr_ref[...], wd_s[...], precise)


def _ffn(xs, wrow, te, nv, w_gate, w_up, w_down, *, li, tm, precise, name):
    rows, d = xs.shape
    n_tiles = rows // tm
    f = EXPERT_FF
    wdt = F32 if precise else BF16

    def row_blk(i, te, nv):
        return (jnp.minimum(i, nv[0] - 1), 0)

    return pl.pallas_call(
        functools.partial(_ffn_kernel, precise=precise),
        grid_spec=pltpu.PrefetchScalarGridSpec(
            num_scalar_prefetch=2,
            grid=(n_tiles,),
            in_specs=[
                pl.BlockSpec((tm, d), row_blk),
                pl.BlockSpec((tm, 1), row_blk),
                pl.BlockSpec((None, d, f), lambda i, te, nv: (li, 0, te[i])),
                pl.BlockSpec((None, d, f), lambda i, te, nv: (li, 0, te[i])),
                pl.BlockSpec((None, f, d), lambda i, te, nv: (li, te[i], 0)),
            ],
            out_specs=pl.BlockSpec((tm, d), row_blk),
            scratch_shapes=[pltpu.VMEM((d, f), wdt), pltpu.VMEM((d, f), wdt), pltpu.VMEM((f, d), wdt)],
        ),
        out_shape=jax.ShapeDtypeStruct((rows, d), F32),
        compiler_params=_cparams(("arbitrary",)),
        name=name,
    )(te, nv, xs, wrow, w_gate, w_up, w_down)


def _rank_kernel(route_ref, offs_ref, dest_ref, run_s):
    @pl.when(pl.program_id(0) == 0)
    def _():
        run_s[...] = jnp.zeros(run_s.shape, run_s.dtype)

    route = route_ref[...]
    tm = route.shape[0]
    lane = lax.broadcasted_iota(jnp.int32, route.shape, 1).astype(F32)
    is0 = lane == route[:, 0:1]
    is1 = lane == route[:, 1:2]
    onehot = jnp.where(is0, 1.0, 0.0) + jnp.where(is1, 1.0, 0.0)
    r = lax.broadcasted_iota(jnp.int32, (tm, tm), 0)
    c = lax.broadcasted_iota(jnp.int32, (tm, tm), 1)
    earlier = jnp.where(c < r, 1.0, 0.0)
    before = _mm(earlier, onehot)
    base = offs_ref[...] + run_s[...] + before
    d0 = jnp.sum(jnp.where(is0, base, 0.0), axis=-1, keepdims=True)
    d1 = jnp.sum(jnp.where(is1, base, 0.0), axis=-1, keepdims=True)
    run_s[...] = run_s[...] + jnp.sum(onehot, axis=0, keepdims=True)
    dest_ref[...] = jnp.where(lane == 0.0, d0, jnp.where(lane == 1.0, d1, 0.0)).astype(jnp.int32)


def _rank(route, offs_row, *, tm, name):
    m = route.shape[0]
    return pl.pallas_call(
        _rank_kernel,
        grid=(m // tm,),
        in_specs=[pl.BlockSpec((tm, LANES), lambda i: (i, 0)), pl.BlockSpec((1, LANES), lambda i: (0, 0))],
        out_specs=pl.BlockSpec((tm, LANES), lambda i: (i, 0)),
        out_shape=jax.ShapeDtypeStruct((m, LANES), jnp.int32),
        scratch_shapes=[pltpu.VMEM((1, LANES), F32)],
        compiler_params=_cparams(("arbitrary",)),
        name=name,
    )(route, offs_row)


ROW_DMA_UNROLL = 8


def _row_copy(src, src_row, dst, dst_row, sem):
    return pltpu.make_async_copy(src.at[pl.ds(src_row, 1)], dst.at[pl.ds(dst_row, 1)], sem)


def _scatter_kernel(dest_ref, last_ref, xn_ref, xs_hbm, zero_s, sem, zsem, *, tile):
    i = pl.program_id(0)
    tm = xn_ref.shape[0]

    def zero_copy(e):
        return pltpu.make_async_copy(zero_s, xs_hbm.at[pl.ds(pl.multiple_of(last_ref[e], tile), tile)], zsem)

    @pl.when(i == 0)
    def _():
        zero_s[...] = jnp.zeros(zero_s.shape, zero_s.dtype)

        def z_start(e, carry):
            @pl.when(last_ref[e] >= 0)
            def _():
                zero_copy(e).start()
            return carry

        def z_wait(e, carry):
            @pl.when(last_ref[e] >= 0)
            def _():
                zero_copy(e).wait()
            return carry

        lax.fori_loop(0, N_EXPERTS, z_start, 0)
        lax.fori_loop(0, N_EXPERTS, z_wait, 0)

    base = i * (2 * tm)

    def start(r, carry):
        _row_copy(xn_ref, r, xs_hbm, dest_ref[base + 2 * r], sem).start()
        _row_copy(xn_ref, r, xs_hbm, dest_ref[base + 2 * r + 1], sem).start()
        return carry

    def wait(r, carry):
        _row_copy(xn_ref, r, xs_hbm, dest_ref[base + 2 * r], sem).wait()
        _row_copy(xn_ref, r, xs_hbm, dest_ref[base + 2 * r + 1], sem).wait()
        return carry

    lax.fori_loop(0, tm, start, 0, unroll=ROW_DMA_UNROLL)
    lax.fori_loop(0, tm, wait, 0, unroll=ROW_DMA_UNROLL)


def _scatter(dest, last_tile, xn, *, rows, tm, tile, name):
    m, d = xn.shape
    return pl.pallas_call(
        functools.partial(_scatter_kernel, tile=tile),
        grid_spec=pltpu.PrefetchScalarGridSpec(
            num_scalar_prefetch=2,
            grid=(m // tm,),
            in_specs=[pl.BlockSpec((tm, d), lambda i, dest, last: (i, 0))],
            out_specs=pl.BlockSpec(memory_space=pl.ANY),
            scratch_shapes=[pltpu.VMEM((tile, d), F32), pltpu.SemaphoreType.DMA(()), pltpu.SemaphoreType.DMA(())],
        ),
        out_shape=jax.ShapeDtypeStruct((rows, d), F32),
        compiler_params=_cparams(("arbitrary",)),
        name=name,
    )(dest, last_tile, xn)


def _combine_kernel(dest_ref, x_ref, wt_ref, y_hbm, o_ref, ya, yb, sem):
    i = pl.program_id(0)
    tm = x_ref.shape[0]
    base = i * (2 * tm)

    def start(r, carry):
        _row_copy(y_hbm, dest_ref[base + 2 * r], ya, r, sem).start()
        _row_copy(y_hbm, dest_ref[base + 2 * r + 1], yb, r, sem).start()
        return carry

    def wait(r, carry):
        _row_copy(y_hbm, dest_ref[base + 2 * r], ya, r, sem).wait()
        _row_copy(y_hbm, dest_ref[base + 2 * r + 1], yb, r, sem).wait()
        return carry

    lax.fori_loop(0, tm, start, 0, unroll=ROW_DMA_UNROLL)
    lax.fori_loop(0, tm, wait, 0, unroll=ROW_DMA_UNROLL)
    wt = wt_ref[...]
    o_ref[...] = x_ref[...] + (wt[:, 2:3] * ya[...] + wt[:, 3:4] * yb[...])


def _combine(dest, x, wts, y, *, tm, name):
    m, d = x.shape
    return pl.pallas_call(
        _combine_kernel,
        grid_spec=pltpu.PrefetchScalarGridSpec(
            num_scalar_prefetch=1,
            grid=(m // tm,),
            in_specs=[
                pl.BlockSpec((tm, d), lambda i, dest: (i, 0)),
                pl.BlockSpec((tm, LANES), lambda i, dest: (i, 0)),
                pl.BlockSpec(memory_space=pl.ANY),
            ],
            out_specs=pl.BlockSpec((tm, d), lambda i, dest: (i, 0)),
            scratch_shapes=[pltpu.VMEM((tm, d), F32), pltpu.VMEM((tm, d), F32), pltpu.SemaphoreType.DMA(())],
        ),
        out_shape=jax.ShapeDtypeStruct((m, d), F32),
        compiler_params=_cparams(("arbitrary",)),
        name=name,
    )(dest, x, wts, y)


def _moe(x, g, w_r, b_r, w_gate, w_up, w_down, *, li, tm_tok, tm_ffn, gate_before_down, precise, name):
    m = x.shape[0]
    xn, route = _router(x, g, w_r, b_r, tm=tm_tok, precise=precise, name=name + "_router")
    eid = route[:, 0:2].astype(jnp.int32)
    counts = jnp.sum((eid[:, :, None] == jnp.arange(N_EXPERTS, dtype=jnp.int32)).astype(jnp.int32), axis=(0, 1))
    padded = ((counts + tm_ffn - 1) // tm_ffn) * tm_ffn
    ends = jnp.cumsum(padded)
    offs = ends - padded
    rows = 2 * m + N_EXPERTS * tm_ffn
    tile_start = jnp.arange(rows // tm_ffn, dtype=jnp.int32) * tm_ffn
    te = jnp.minimum(jnp.sum((ends[None, :] <= tile_start[:, None]).astype(jnp.int32), axis=1), N_EXPERTS - 1)
    nv = (ends[-1] // tm_ffn).reshape(1)
    last_tile = jnp.where(padded > 0, ends - tm_ffn, -1)
    offs_row = jnp.pad(offs.astype(F32), (0, LANES - N_EXPERTS)).reshape(1, LANES)
    dest = _rank(route, offs_row, tm=tm_tok, name=name + "_rank")[:, 0:2].reshape(-1)
    xs = _scatter(dest, last_tile, xn, rows=rows, tm=tm_tok, tile=tm_ffn, name=name + "_scatter")
    if gate_before_down:
        wrow = jnp.zeros((rows, 1), F32).at[dest, 0].set(route[:, 2:4].reshape(-1))
        wts = jnp.ones_like(route)
    else:
        wrow = jnp.ones((rows, 1), F32)
        wts = route
    y = _ffn(xs, wrow, te, nv, w_gate, w_up, w_down, li=li, tm=tm_ffn, precise=precise, name=name + "_ffn")
    return _combine(dest, x, wts, y, tm=tm_tok, name=name + "_combine")


def _rmsnorm_kernel(x_ref, g_ref, o_ref):
    xf = x_ref[...]
    r = lax.rsqrt(jnp.mean(xf * xf, axis=-1, keepdims=True) + EPS)
    o_ref[...] = (xf * r) * g_ref[...]


def _rmsnorm(x, g, *, tm, name):
    m, d = x.shape
    return pl.pallas_call(
        _rmsnorm_kernel,
        grid=(m // tm,),
        in_specs=[pl.BlockSpec((tm, d), lambda i: (i, 0)), pl.BlockSpec((1, d), lambda i: (0, 0))],
        out_specs=pl.BlockSpec((tm, d), lambda i: (i, 0)),
        out_shape=jax.ShapeDtypeStruct((m, d), F32),
        compiler_params=_cparams(("parallel",)),
        name=name,
    )(x, g)


def _layer_weights(p, li):
    b_if = jnp.pad(p["b_if"][li], (0, 2 * LANES - 2 * ML_HEADS)).reshape(1, 2 * LANES)
    w_r = jnp.pad(jnp.concatenate([p["w_rg"][li], p["w_re"][li]], axis=1),
                  ((0, 0), (0, LANES - N_GROUPS - N_EXPERTS)))
    b_r = jnp.pad(jnp.concatenate([p["b_rg"][li], p["b_re"][li]]), (0, LANES - N_GROUPS - N_EXPERTS)).reshape(1, LANES)
    return b_if, w_r, b_r


def _run_trunk(x3, p, layer_w, *, caches, c0s, n0s, m0s, pos0):
    bsz, seq, d = x3.shape
    m = bsz * seq
    x = x3.reshape(m, d)
    precise = False
    if caches is None:
        act_dtype, tag = BF16, "p"
        tm, tn, tm_tok, tm_ffn, seq_pad = 1024, 512, 256, 256, seq
    else:
        act_dtype, tag = F32, "s"
        tm, tn, tm_tok, tm_ffn, seq_pad = m, 512, m, 16, 16
    cos_t, sin_t = _rope_tables(pos0 + jnp.arange(seq_pad, dtype=jnp.int32))
    zeros_bias = jnp.zeros((1, N_MAIN), F32)
    new_kv = [[] for _ in ATT_GROUPS]
    c_all, n_all, m_all = [], [], []
    for li in range(DEPTH):
        b_if, w_r, b_r = layer_w[li]
        g_attn = p["attn_norm_g"][li].reshape(1, d)
        common = dict(li=li, tm=tm, precise=precise)
        u = _norm_matmul(x, g_attn, p["w_in"], zeros_bias, col_blk0=0, n_out=N_MAIN, tn=TN_MAIN,
                         out_dtype=act_dtype, act="none", name=f"{tag}{li}_inproj", **common)
        n_sg = 2 * D_MODEL
        sg_main = _norm_matmul(x, g_attn, p["w_in"], zeros_bias, col_blk0=N_MAIN // tn, n_out=n_sg, tn=tn,
                               out_dtype=act_dtype, act="sigmoid", name=f"{tag}{li}_gateproj", **common)
        tail_blk = (N_MAIN + n_sg) // LANES
        gt = _norm_matmul(x, g_attn, p["w_in"], b_if, col_blk0=N_MAIN // LANES, col_stride=tail_blk - N_MAIN // LANES,
                          n_out=2 * LANES, tn=LANES, out_dtype=F32, act="gates_tail", name=f"{tag}{li}_ifproj",
                          **common)
        gates = gt[:, :LANES]
        n_gate_cols = 2 * ML_HEADS
        sg = jnp.concatenate([sg_main[:, n_gate_cols:], gt[:, LANES:LANES + n_gate_cols].astype(act_dtype)], axis=1)
        u3 = u.reshape(bsz, seq, N_MAIN)
        if seq_pad > seq:
            u3 = jnp.pad(u3, ((0, 0), (0, seq_pad - seq), (0, 0)))
        if caches is None:
            att3, kcs, vcs = _attn_prompt(u3, cos_t, sin_t, name=f"{tag}{li}_attn")
            for gi in range(N_ATT_GROUPS):
                keep = kcs[gi].shape[1]
                new_kv[gi].append(jnp.stack([kcs[gi].reshape(bsz, keep, ATT_HEADS, HEAD_DIM),
                                             vcs[gi].reshape(bsz, keep, ATT_HEADS, HEAD_DIM)], axis=2))
            att = att3.reshape(m, ATT_WIDTH)
        else:
            caches2d = [c.reshape(-1, HEAD_DIM) for c in caches]
            att3, kn3 = _attn_sample(u3, caches2d, cos_t, sin_t, li=li, n_new=seq, name=f"{tag}{li}_attn")
            att = att3[:, :seq].reshape(m, ATT_WIDTH)
            nh = N_ATT_GROUPS * ATT_HEADS
            k_new = kn3[:, :seq].reshape(bsz, seq, nh, HEAD_DIM)
            v_new = u3[:, :seq, O_AV:O_AV + nh * HEAD_DIM].reshape(bsz, seq, nh, HEAD_DIM)
            for gi in range(N_ATT_GROUPS):
                sl = slice(gi * ATT_HEADS, (gi + 1) * ATT_HEADS)
                new_kv[gi].append(jnp.stack([k_new[:, :, sl], v_new[:, :, sl]], axis=2))
        gc, gr = _gate_layouts(gates, bsz, seq, seq_pad)
        mlg = p["ml_norm_g"][li].reshape(1, ML_HEADS * ML_V)
        hg3, c_new, n_new, m_new = _mlstm(
            u3, gc, gr, mlg, c0s[li], n0s[li], m0s[li], chunk=min(128, seq_pad), sb=min(512, seq_pad),
            out_dtype=act_dtype, precise=precise, name=f"{tag}{li}_mlstm")
        hg = hg3[:, :seq].reshape(m, ML_HEADS * ML_V)
        c_all.append(c_new)
        n_all.append(n_new)
        m_all.append(m_new)
        merged = _merge(att, hg, sg, p["w_pa"], p["w_pm"], li=li, tm=tm, tn=tn, out_dtype=act_dtype,
                        precise=precise, name=f"{tag}{li}_merge")
        x = _outproj(x, merged, p["w_out"], li=li, tm=tm, tn=tn, precise=precise, name=f"{tag}{li}_outproj")
        x = _moe(x, p["ffn_norm_g"][li].reshape(1, d), w_r, b_r, p["w_gate"], p["w_up"], p["w_down"],
                 li=li, tm_tok=tm_tok, tm_ffn=tm_ffn, gate_before_down=caches is not None, precise=precise,
                 name=f"{tag}{li}_moe")
    y = _rmsnorm(x, p["final_norm_g"].reshape(1, d), tm=min(m, 512), name=f"{tag}_final_norm")
    new_kv = [jnp.stack(a) for a in new_kv]
    if caches is not None:
        shift = [(0, 0, 0)] * 2 + [(-seq, seq, 0)] + [(0, 0, 0)] * 3
        new_kv = [lax.dynamic_update_slice_in_dim(lax.pad(c, jnp.float32(0), shift), fresh, c.shape[2] - seq, axis=2)
                  for c, fresh in zip(caches, new_kv)]
    return (y.reshape(bsz, seq, d), new_kv, jnp.stack(c_all), jnp.stack(n_all), jnp.stack(m_all))


def kernel(x_prompt, x_sample, cache_kv_w128, cache_kv_w512, cache_kv_w2048, state_C, state_n, state_m,
           attn_norm_g, w_in, b_if, ml_norm_g, w_pa, w_pm, w_out, ffn_norm_g, w_rg, b_rg, w_re, b_re,
           w_gate, w_up, w_down, final_norm_g):
    p = dict(attn_norm_g=attn_norm_g, w_in=jnp.swapaxes(w_in, 1, 2), b_if=b_if, ml_norm_g=ml_norm_g, w_pa=w_pa, w_pm=w_pm,
             w_out=w_out, ffn_norm_g=ffn_norm_g, w_rg=w_rg, b_rg=b_rg, w_re=w_re, b_re=b_re,
             w_gate=w_gate, w_up=w_up, w_down=w_down, final_norm_g=final_norm_g)
    layer_w = [_layer_weights(p, li) for li in range(DEPTH)]
    bp = x_prompt.shape[0]
    c0 = jnp.zeros((DEPTH, bp, ML_HEADS, ML_V, ML_QK), F32)
    n0 = jnp.zeros((DEPTH, bp, ML_HEADS, ML_QK), F32)
    m0 = jnp.zeros((DEPTH, bp, ML_HEADS), F32)
    y_p, p_kv, p_c, p_n, p_m = _run_trunk(x_prompt, p, layer_w, caches=None,
                                          c0s=c0, n0s=n0, m0s=m0, pos0=0)
    y_s, s_kv, s_c, s_n, s_m = _run_trunk(x_sample, p, layer_w,
                                          caches=[cache_kv_w128, cache_kv_w512, cache_kv_w2048],
                                          c0s=state_C, n0s=state_n, m0s=state_m, pos0=PAST_LEN)
    return (y_p, y_s, p_kv[0], p_kv[1], p_kv[2], p_c, p_n, p_m,
            s_kv[0], s_kv[1], s_kv[2], s_c, s_n, s_m)
```

```python
import functools

import jax
import jax.numpy as jnp
from jax import lax
from jax.experimental import pallas as pl
from jax.experimental.pallas import tpu as pltpu

F32 = jnp.float32
BF16 = jnp.bfloat16

D_MODEL = 2048
DEPTH = 2
PAST_LEN = 16384
ATT_GROUPS = ((128, 1), (512, 4), (2048, 16))
N_ATT_GROUPS = 3
ATT_HEADS = 4
HEAD_DIM = 128
ATT_WIDTH = ATT_HEADS * HEAD_DIM
ATT_SCALE = HEAD_DIM ** -0.5
ROPE_DIM = HEAD_DIM // 4
ROPE_THETA = 500000.0
ML_HEADS = 8
ML_QK = 128
ML_V = 256
ML_K_SCALE = ML_QK ** -0.5
N_GROUPS = 4
EXP_PER_GROUP = 8
N_EXPERTS = N_GROUPS * EXP_PER_GROUP
EXPERT_FF = 256
EPS = 1e-6

O_AQ, O_AK, O_AV = 0, 1536, 3072
O_MQ, O_MK, O_MV, O_MO = 4608, 5632, 6656, 8704
O_MI, O_MF, O_GA, O_GB = 10752, 10760, 10768, 12816
N_MAIN = O_MI
LANES = 128
SUBLANES = 8
TN_MAIN = 768
VMEM_LIMIT_MB = 56


def _cparams(sem):
    return pltpu.CompilerParams(dimension_semantics=sem, vmem_limit_bytes=VMEM_LIMIT_MB * 1024 * 1024)


def _dot(a, b, dims, precise):
    dn = (dims, ((), ()))
    if precise:
        return lax.dot_general(a.astype(F32), b.astype(F32), dn, preferred_element_type=F32,
                               precision=lax.Precision.HIGHEST)
    return lax.dot_general(a.astype(BF16), b.astype(BF16), dn, preferred_element_type=F32)


def _mm(a, b, precise=False):
    return _dot(a, b, ((1,), (0,)), precise)


def _mm_nt(a, b, precise=False):
    return _dot(a, b, ((1,), (1,)), precise)


def _mm_tn(a, b, precise=False):
    return _dot(a, b, ((0,), (0,)), precise)


def _rb(x, precise=False):
    return x.astype(F32) if precise else x.astype(BF16).astype(F32)


def _sigmoid(z):
    return 1.0 / (1.0 + jnp.exp(-z))


def _log_sigmoid(z):
    return jnp.minimum(z, 0.0) - jnp.log1p(jnp.exp(-jnp.abs(z)))


def _norm_matmul_kernel(x_ref, g_ref, w_ref, b_ref, o_ref, xn_ref, *, act, precise):
    @pl.when(pl.program_id(1) == 0)
    def _():
        xf = x_ref[...]
        r = lax.rsqrt(jnp.mean(xf * xf, axis=-1, keepdims=True) + EPS)
        xn_ref[...] = ((xf * r) * g_ref[...]).astype(xn_ref.dtype)

    acc = _mm_nt(xn_ref[...], w_ref[0], precise)
    if act == "sigmoid":
        acc = _sigmoid(acc)
    elif act == "gates":
        z = acc + b_ref[...]
        lane = lax.broadcasted_iota(jnp.int32, z.shape, 1)
        acc = jnp.where(lane < ML_HEADS, z, _log_sigmoid(z))
    o_ref[...] = acc.astype(o_ref.dtype)


def _norm_matmul(x, g, wt, bias, *, li, row0, n_out, tm, tn, out_dtype, act, precise, name):
    m, d = x.shape
    assert row0 % SUBLANES == 0 and tn % SUBLANES == 0
    w_spec = pl.BlockSpec((pl.Element(1), pl.Element(tn), pl.Element(d)),
                          lambda i, j: (li, pl.multiple_of(row0 + j * tn, SUBLANES), 0))
    return pl.pallas_call(
        functools.partial(_norm_matmul_kernel, act=act, precise=precise),
        grid=(m // tm, n_out // tn),
        in_specs=[
            pl.BlockSpec((tm, d), lambda i, j: (i, 0)),
            pl.BlockSpec((1, d), lambda i, j: (0, 0)),
            w_spec,
            pl.BlockSpec((1, tn), lambda i, j: (0, j)),
        ],
        out_specs=pl.BlockSpec((tm, tn), lambda i, j: (i, j)),
        out_shape=jax.ShapeDtypeStruct((m, n_out), out_dtype),
        scratch_shapes=[pltpu.VMEM((tm, d), F32 if precise else BF16)],
        compiler_params=_cparams(("parallel", "arbitrary")),
        name=name,
    )(x, g, wt, bias)


def _merge_kernel(att_ref, hg_ref, sa_ref, sb_ref, wpa_ref, wpm_ref, o_ref, *, precise):
    a = _mm(att_ref[...], wpa_ref[...], precise)
    m = _mm(hg_ref[...], wpm_ref[...], precise)
    o_ref[...] = (sa_ref[...].astype(F32) * a + sb_ref[...].astype(F32) * m).astype(o_ref.dtype)


def _merge(att, hg, sg, w_pa, w_pm, *, li, tm, tn, out_dtype, precise, name):
    m = att.shape[0]
    nb = D_MODEL // tn
    return pl.pallas_call(
        functools.partial(_merge_kernel, precise=precise),
        grid=(m // tm, nb),
        in_specs=[
            pl.BlockSpec((tm, ATT_WIDTH), lambda i, j: (i, 0)),
            pl.BlockSpec((tm, ML_HEADS * ML_V), lambda i, j: (i, 0)),
            pl.BlockSpec((tm, tn), lambda i, j: (i, j)),
            pl.BlockSpec((tm, tn), lambda i, j: (i, j + nb)),
            pl.BlockSpec((None, ATT_WIDTH, tn), lambda i, j: (li, 0, j)),
            pl.BlockSpec((None, ML_HEADS * ML_V, tn), lambda i, j: (li, 0, j)),
        ],
        out_specs=pl.BlockSpec((tm, tn), lambda i, j: (i, j)),
        out_shape=jax.ShapeDtypeStruct((m, D_MODEL), out_dtype),
        compiler_params=_cparams(("parallel", "arbitrary")),
        name=name,
    )(att, hg, sg, sg, w_pa, w_pm)


def _outproj_kernel(x_ref, mg_ref, w_ref, o_ref, *, precise):
    o_ref[...] = x_ref[...] + _mm(mg_ref[...], w_ref[...], precise)


def _outproj(x, mg, w_out, *, li, tm, tn, precise, name):
    m = x.shape[0]
    return pl.pallas_call(
        functools.partial(_outproj_kernel, precise=precise),
        grid=(m // tm, D_MODEL // tn),
        in_specs=[
            pl.BlockSpec((tm, tn), lambda i, j: (i, j)),
            pl.BlockSpec((tm, D_MODEL), lambda i, j: (i, 0)),
            pl.BlockSpec((None, D_MODEL, tn), lambda i, j: (li, 0, j)),
        ],
        out_specs=pl.BlockSpec((tm, tn), lambda i, j: (i, j)),
        out_shape=jax.ShapeDtypeStruct((m, D_MODEL), F32),
        compiler_params=_cparams(("parallel", "arbitrary")),
        name=name,
    )(x, mg, w_out)


def _rope_tables(pos):
    half = ROPE_DIM // 2
    inv = jnp.power(jnp.float32(ROPE_THETA), -jnp.arange(half, dtype=F32) / half)
    ang = pos.astype(F32)[:, None] * inv[None, :]
    n = pos.shape[0]
    cos_t = jnp.concatenate([jnp.cos(ang), jnp.cos(ang), jnp.ones((n, HEAD_DIM - ROPE_DIM), F32)], axis=-1)
    sin_t = jnp.concatenate([jnp.sin(ang), jnp.sin(ang), jnp.zeros((n, HEAD_DIM - ROPE_DIM), F32)], axis=-1)
    return cos_t, sin_t


def _rope_perm():
    half = ROPE_DIM // 2
    r = jnp.arange(HEAD_DIM)[:, None]
    c = jnp.arange(HEAD_DIM)[None, :]
    p = jnp.where((c < half) & (r == c + half), -1.0, 0.0) + jnp.where(
        (c >= half) & (c < ROPE_DIM) & (r == c - half), 1.0, 0.0)
    return p.astype(BF16)


ROPE_ROWS = 512
ATT_BLOCKS_PER_TRIP = 4


def _attn_prompt_kernel(cos_ref, sin_ref, perm_ref, *refs, seq):
    q_refs, k_refs, v_refs = refs[0:3], refs[3:6], refs[6:9]
    att_ref = refs[9]
    kc_refs, vc_refs = refs[10:13], refs[13:16]
    qs, ks, vs = refs[16:19]
    o_s, lse_s = refs[19:22], refs[22:25]
    perm = perm_ref[...]
    nk = ATT_GROUPS[0][0] // ATT_GROUPS[0][1]
    row = lax.broadcasted_iota(jnp.int32, (nk, nk), 0)
    col = lax.broadcasted_iota(jnp.int32, (nk, nk), 1)
    cur_ok = col <= row
    prev_ok = col >= row
    neg = jnp.float32(-jnp.inf)

    for g, (win, dil) in enumerate(ATT_GROUPS):
        assert win // dil == nk
        span = nk * dil
        shift = dil.bit_length() - 1

        def rope_body(c, carry, g=g):
            r0 = pl.multiple_of(c * ROPE_ROWS, ROPE_ROWS)
            cs = cos_ref[pl.ds(r0, ROPE_ROWS), :]
            sn = sin_ref[pl.ds(r0, ROPE_ROWS), :]
            qb = q_refs[g][0, pl.ds(r0, ROPE_ROWS), :]
            kb = k_refs[g][0, pl.ds(r0, ROPE_ROWS), :]
            qr = qb.astype(F32) * cs + jnp.dot(qb, perm, preferred_element_type=F32) * sn
            kr = kb.astype(F32) * cs + jnp.dot(kb, perm, preferred_element_type=F32) * sn
            qs[pl.ds(r0, ROPE_ROWS), :] = qr
            ks[pl.ds(r0, ROPE_ROWS), :] = kr
            vs[pl.ds(r0, ROPE_ROWS), :] = v_refs[g][0, pl.ds(r0, ROPE_ROWS), :].astype(F32)
            return carry

        lax.fori_loop(0, seq // ROPE_ROWS, rope_body, 0)
        keep = min(win, seq)
        kc_refs[g][0] = ks[seq - keep:seq, :]
        vc_refs[g][0] = vs[seq - keep:seq, :]

        def rows(start, dil=dil):
            if dil == 1:
                return pl.ds(pl.multiple_of(start, nk), nk)
            return pl.ds(start, nk, stride=dil)

        def blk_body(it, carry, g=g, dil=dil, span=span, shift=shift, rows=rows):
            blks = [it * ATT_BLOCKS_PER_TRIP + u for u in range(ATT_BLOCKS_PER_TRIP)]
            n = [blk >> shift for blk in blks]
            start = [n_u * span + (blk & (dil - 1)) for n_u, blk in zip(n, blks)]
            has_prev = [n_u > 0 for n_u in n]
            start_p = [jnp.where(hp, st - span, st) for hp, st in zip(has_prev, start)]
            q = [qs[rows(st), :] for st in start]
            s_c = [jnp.where(cur_ok, _mm_nt(q_u, ks[rows(st), :]) * ATT_SCALE, neg) for q_u, st in zip(q, start)]
            no_prev = [jnp.where(hp, jnp.float32(0.0), neg) for hp in has_prev]
            s_p = [jnp.where(prev_ok, _mm_nt(q_u, ks[rows(sp), :]) * ATT_SCALE + np_u, neg)
                   for q_u, sp, np_u in zip(q, start_p, no_prev)]
            m = [jnp.max(jnp.maximum(c_u, p_u), axis=-1, keepdims=True) for c_u, p_u in zip(s_c, s_p)]
            p_c = [jnp.exp(c_u - m_u) for c_u, m_u in zip(s_c, m)]
            p_p = [jnp.exp(p_u - m_u) for p_u, m_u in zip(s_p, m)]
            l = [jnp.sum(c_u + p_u, axis=-1, keepdims=True) for c_u, p_u in zip(p_c, p_p)]
            acc = [_mm(c_u, vs[rows(st), :]) + _mm(p_u, vs[rows(sp), :])
                   for c_u, p_u, st, sp in zip(p_c, p_p, start, start_p)]
            for u, st in enumerate(start):
                o_s[g][rows(st), :] = acc[u] * (1.0 / l[u])
                lse_s[g][rows(st), :] = jnp.broadcast_to(m[u] + jnp.log(l[u]), (nk, HEAD_DIM))
            return carry

        lax.fori_loop(0, seq // nk // ATT_BLOCKS_PER_TRIP, blk_body, 0)

    def out_body(c, carry):
        sl = pl.ds(pl.multiple_of(c * nk, nk), nk)
        lse = [lse_s[g][sl, :] for g in range(N_ATT_GROUPS)]
        top = functools.reduce(jnp.maximum, lse)
        w = [jnp.exp(x - top) for x in lse]
        num = sum(w[g] * o_s[g][sl, :] for g in range(N_ATT_GROUPS))
        att_ref[0, sl, :] = (num / sum(w)).astype(att_ref.dtype)
        return carry

    lax.fori_loop(0, seq // nk, out_body, 0)


def _attn_prompt(u3, cos_t, sin_t, *, name):
    bsz, seq, _ = u3.shape
    perm = _rope_perm()
    in_specs = [
        pl.BlockSpec((seq, HEAD_DIM), lambda b, h: (0, 0)),
        pl.BlockSpec((seq, HEAD_DIM), lambda b, h: (0, 0)),
        pl.BlockSpec((HEAD_DIM, HEAD_DIM), lambda b, h: (0, 0)),
    ]
    for off in (O_AQ, O_AK, O_AV):
        for g in range(N_ATT_GROUPS):
            blk0 = off // HEAD_DIM + g * ATT_HEADS
            in_specs.append(pl.BlockSpec((1, seq, HEAD_DIM), lambda b, h, blk0=blk0: (b, 0, blk0 + h)))
    keeps = [min(w, seq) for w, _ in ATT_GROUPS]
    out_specs = [pl.BlockSpec((1, seq, HEAD_DIM), lambda b, h: (b, 0, h))]
    out_shape = [jax.ShapeDtypeStruct((bsz, seq, ATT_WIDTH), BF16)]
    for _ in range(2):
        for keep in keeps:
            out_specs.append(pl.BlockSpec((1, keep, HEAD_DIM), lambda b, h: (b, 0, h)))
            out_shape.append(jax.ShapeDtypeStruct((bsz, keep, ATT_WIDTH), F32))
    outs = pl.pallas_call(
        functools.partial(_attn_prompt_kernel, seq=seq),
        grid=(bsz, ATT_HEADS),
        in_specs=in_specs,
        out_specs=out_specs,
        out_shape=out_shape,
        scratch_shapes=[pltpu.VMEM((seq, HEAD_DIM), F32) for _ in range(3 + 2 * N_ATT_GROUPS)],
        compiler_params=_cparams(("parallel", "arbitrary")),
        name=name,
    )(cos_t, sin_t, perm, *([u3] * 9))
    return outs[0], outs[1:4], outs[4:7]


def _attn_sample_kernel(cos_ref, sin_ref, q_ref, k_ref, v_ref, c0_ref, c1_ref, c2_ref,
                        att_ref, kn_ref, *, n_new):
    cache_refs = (c0_ref, c1_ref, c2_ref)
    cs = cos_ref[...]
    sn = sin_ref[...]
    rows_pad = q_ref.shape[1]
    half = ROPE_DIM // 2
    lane = lax.broadcasted_iota(jnp.int32, (rows_pad, HEAD_DIM), 1)

    def rot_half(x):
        return jnp.where(lane < half, -pltpu.roll(x, HEAD_DIM - half, axis=1),
                         jnp.where(lane < ROPE_DIM, pltpu.roll(x, half, axis=1), 0.0))

    nk = ATT_GROUPS[0][0] // ATT_GROUPS[0][1]
    kv_rows = 2 * ATT_HEADS
    w_iota = lax.broadcasted_iota(jnp.int32, (nk, 1), 0)
    u_iota = lax.broadcasted_iota(jnp.int32, (rows_pad, 1), 0)
    neg = jnp.float32(-jnp.inf)
    att_ref[...] = jnp.zeros(att_ref.shape, att_ref.dtype)

    for h in range(ATT_HEADS):
        parts = [[None] * N_ATT_GROUPS for _ in range(n_new)]
        for g, (win, dil) in enumerate(ATT_GROUPS):
            hs = (g * ATT_HEADS + h) * HEAD_DIM
            qh = q_ref[0, :, hs:hs + HEAD_DIM]
            kh = k_ref[0, :, hs:hs + HEAD_DIM]
            vh = v_ref[0, :, hs:hs + HEAD_DIM]
            qr = _rb(qh * cs + rot_half(qh) * sn)
            kr = kh * cs + rot_half(kh) * sn
            kn_ref[0, :, hs:hs + HEAD_DIM] = kr
            kr = _rb(kr)
            vh = _rb(vh)
            cref = cache_refs[g]
            for t in range(n_new):
                q_t = qr[t:t + 1, :]
                if dil == 1:
                    p0 = 0
                else:
                    p0 = t
                k_c = _rb(cref[pl.ds(p0 * kv_rows + h, nk, stride=kv_rows * dil), :])
                v_c = _rb(cref[pl.ds(p0 * kv_rows + ATT_HEADS + h, nk, stride=kv_rows * dil), :])
                s_c = jnp.sum(k_c * q_t, axis=-1, keepdims=True) * ATT_SCALE
                s_n = jnp.sum(kr * q_t, axis=-1, keepdims=True) * ATT_SCALE
                if dil == 1:
                    s_c = jnp.where(w_iota >= t, s_c, neg)
                    s_n = jnp.where(u_iota <= t, s_n, neg)
                else:
                    s_n = jnp.where(u_iota == t, s_n, neg)
                m = jnp.maximum(jnp.max(s_c, axis=0, keepdims=True), jnp.max(s_n, axis=0, keepdims=True))
                p_c = jnp.exp(s_c - m)
                p_n = jnp.exp(s_n - m)
                l = jnp.sum(p_c, axis=0, keepdims=True) + jnp.sum(p_n, axis=0, keepdims=True)
                acc = jnp.sum(_rb(p_c) * v_c, axis=0, keepdims=True) + jnp.sum(_rb(p_n) * vh, axis=0, keepdims=True)
                parts[t][g] = (m, l, acc)
        for t in range(n_new):
            m_all = functools.reduce(jnp.maximum, [p[0] for p in parts[t]])
            num = sum(jnp.exp(p[0] - m_all) * p[2] for p in parts[t])
            den = sum(jnp.exp(p[0] - m_all) * p[1] for p in parts[t])
            att_ref[0, t:t + 1, h * HEAD_DIM:(h + 1) * HEAD_DIM] = num / den


def _attn_sample(u3, caches2d, cos_t, sin_t, *, li, n_new, name):
    bsz, rows_pad, _ = u3.shape
    qkv_w = N_ATT_GROUPS * ATT_WIDTH
    in_specs = [
        pl.BlockSpec((rows_pad, HEAD_DIM), lambda b: (0, 0)),
        pl.BlockSpec((rows_pad, HEAD_DIM), lambda b: (0, 0)),
        pl.BlockSpec((1, rows_pad, qkv_w), lambda b: (b, 0, O_AQ // qkv_w)),
        pl.BlockSpec((1, rows_pad, qkv_w), lambda b: (b, 0, O_AK // qkv_w)),
        pl.BlockSpec((1, rows_pad, qkv_w), lambda b: (b, 0, O_AV // qkv_w)),
    ]
    for (win, _), c in zip(ATT_GROUPS, caches2d):
        rows = c.shape[0] // (DEPTH * bsz)
        in_specs.append(pl.BlockSpec((rows, HEAD_DIM), lambda b, bsz=bsz: (li * bsz + b, 0)))
    return pl.pallas_call(
        functools.partial(_attn_sample_kernel, n_new=n_new),
        grid=(bsz,),
        in_specs=in_specs,
        out_specs=[
            pl.BlockSpec((1, rows_pad, ATT_WIDTH), lambda b: (b, 0, 0)),
            pl.BlockSpec((1, rows_pad, qkv_w), lambda b: (b, 0, 0)),
        ],
        out_shape=[
            jax.ShapeDtypeStruct((bsz, rows_pad, ATT_WIDTH), F32),
            jax.ShapeDtypeStruct((bsz, rows_pad, qkv_w), F32),
        ],
        compiler_params=_cparams(("parallel",)),
        name=name,
    )(cos_t, sin_t, u3, u3, u3, *caches2d)


ML_HPB = 4
ML_VPB = 2


def _mlstm_kernel(q_ref, k_ref, *refs, chunk, precise):
    nvb = ML_HPB // ML_VPB
    v_refs, mo_refs = refs[:nvb], refs[nvb:2 * nvb]
    gc_ref, gr_ref, mlg_ref, c0_ref, n0_ref, m0_ref, h_ref, c_out, n_out, m_out, ct_s, n_s, m_s = refs[2 * nvb:]
    s_idx = pl.program_id(2)
    n_s_blocks = pl.num_programs(2)
    sb = q_ref.shape[1]
    L = chunk

    @pl.when(s_idx == 0)
    def _():
        for hh in range(ML_HPB):
            ct_s[hh] = c0_ref[0, hh].T
            n_s[hh] = n0_ref[0, 0, hh:hh + 1, :]
            m_s[hh] = m0_ref[0, 0, hh:hh + 1, :]

    row = lax.broadcasted_iota(jnp.int32, (L, L), 0)
    col = lax.broadcasted_iota(jnp.int32, (L, L), 1)
    causal = row >= col
    neg = jnp.float32(-jnp.inf)

    def chunk_body(c, carry):
        r0 = pl.multiple_of(c * L, L)
        gc = gc_ref[0, 0, pl.ds(r0, L), :]
        gr = gr_ref[0, 0, :, pl.ds(r0, L)]
        heads = range(ML_HPB)
        ig_col = [gc[:, hh:hh + 1] for hh in heads]
        lf_col = [gc[:, ML_HPB + hh:ML_HPB + hh + 1] for hh in heads]
        ig_row = [gr[hh:hh + 1, :] for hh in heads]
        lf_row = [gr[ML_HPB + hh:ML_HPB + hh + 1, :] for hh in heads]
        vcols = [slice((hh % ML_VPB) * ML_V, (hh % ML_VPB + 1) * ML_V) for hh in heads]
        qf = [q_ref[0, pl.ds(r0, L), hh * ML_QK:(hh + 1) * ML_QK] for hh in heads]
        ks = [k_ref[0, pl.ds(r0, L), hh * ML_QK:(hh + 1) * ML_QK].astype(F32) * ML_K_SCALE for hh in heads]
        vf = [v_refs[hh // ML_VPB][0, pl.ds(r0, L), vcols[hh]] for hh in heads]
        ct = [ct_s[hh] for hh in heads]
        n_row = [n_s[hh] for hh in heads]
        m_prev = [m_s[hh][:, 0:1] for hh in heads]
        qk = [_mm_nt(qf[hh], ks[hh], precise) for hh in heads]
        q_ct = [_mm(qf[hh], ct[hh], precise) for hh in heads]
        b_col = [jnp.sum(jnp.where(causal, lf_row[hh], 0.0), axis=1, keepdims=True) for hh in heads]
        b_row = [jnp.sum(jnp.where(row <= col, lf_col[hh], 0.0), axis=0, keepdims=True) for hh in heads]
        inter = [b_col[hh] + m_prev[hh] for hh in heads]
        dm = [jnp.where(causal, b_col[hh] - b_row[hh] + ig_row[hh], neg) for hh in heads]
        m_t = [jnp.maximum(inter[hh], jnp.max(dm[hh], axis=1, keepdims=True)) for hh in heads]
        w_inter = [jnp.exp(inter[hh] - m_t[hh]) for hh in heads]
        sw = [qk[hh] * jnp.exp(dm[hh] - m_t[hh]) for hh in heads]
        num = [w_inter[hh] * q_ct[hh] + _mm(sw[hh], vf[hh], precise) for hh in heads]
        den = [w_inter[hh] * jnp.sum(_rb(qf[hh], precise) * _rb(n_row[hh], precise), axis=1, keepdims=True)
               + jnp.sum(sw[hh], axis=1, keepdims=True) for hh in heads]
        hv = [num[hh] / jnp.maximum(jnp.abs(den[hh]), jnp.exp(-m_t[hh])) for hh in heads]
        hn = [hv[hh] * lax.rsqrt(jnp.mean(hv[hh] * hv[hh], axis=-1, keepdims=True) + EPS) for hh in heads]
        for hh in heads:
            mo = mo_refs[hh // ML_VPB][0, pl.ds(r0, L), vcols[hh]].astype(F32)
            gain = mlg_ref[:, hh * ML_V:(hh + 1) * ML_V]
            h_ref[0, pl.ds(r0, L), hh * ML_V:(hh + 1) * ML_V] = (hn[hh] * gain * _sigmoid(mo)).astype(h_ref.dtype)
        b_last = [b_col[hh][L - 1:L, :] for hh in heads]
        m_new = [m_t[hh][L - 1:L, :] for hh in heads]
        decay = [jnp.exp(b_last[hh] + m_prev[hh] - m_new[hh]) for hh in heads]
        wk_col = [jnp.exp(b_last[hh] - b_col[hh] + ig_col[hh] - m_new[hh]) for hh in heads]
        upd = [_mm_tn(ks[hh] * wk_col[hh], vf[hh], precise) for hh in heads]
        for hh in heads:
            ct_s[hh] = decay[hh] * ct[hh] + upd[hh]
            n_s[hh] = decay[hh] * n_row[hh] + jnp.sum(_rb(ks[hh], precise) * _rb(wk_col[hh], precise), axis=0,
                                                      keepdims=True)
            m_s[hh] = jnp.broadcast_to(m_new[hh], (1, LANES))
        return carry

    lax.fori_loop(0, sb // L, chunk_body, 0)

    @pl.when(s_idx == n_s_blocks - 1)
    def _():
        for hh in range(ML_HPB):
            c_out[0, hh] = ct_s[hh].T
            n_out[0, 0, hh:hh + 1, :] = n_s[hh]
            m_out[0, 0, hh:hh + 1, :] = m_s[hh]


def _mlstm(u3, gc, gr, mlg, c0, n0, m0, *, chunk, sb, out_dtype, precise, name):
    bsz, seq, _ = u3.shape
    hg_n = ML_HEADS // ML_HPB
    qk_w, v_w, vb_w = ML_HPB * ML_QK, ML_HPB * ML_V, ML_VPB * ML_V
    nvb = ML_HPB // ML_VPB
    v_specs = [pl.BlockSpec((1, sb, vb_w), lambda b, hg, s, j=j, o=off // vb_w: (b, s, o + hg * nvb + j))
               for off in (O_MV, O_MO) for j in range(nvb)]
    n0r = n0.reshape(bsz, hg_n, ML_HPB, ML_QK)
    m0r = jnp.broadcast_to(m0.reshape(bsz, hg_n, ML_HPB, 1), (bsz, hg_n, ML_HPB, LANES))
    state_spec = pl.BlockSpec((1, 1, ML_HPB, LANES), lambda b, hg, s: (b, hg, 0, 0))
    c_spec = pl.BlockSpec((1, ML_HPB, ML_V, ML_QK), lambda b, hg, s: (b, hg, 0, 0))
    h, c_new, n_new, m_new = pl.pallas_call(
        functools.partial(_mlstm_kernel, chunk=chunk, precise=precise),
        grid=(bsz, hg_n, seq // sb),
        in_specs=[
            pl.BlockSpec((1, sb, qk_w), lambda b, hg, s: (b, s, O_MQ // qk_w + hg)),
            pl.BlockSpec((1, sb, qk_w), lambda b, hg, s: (b, s, O_MK // qk_w + hg)),
            *v_specs,
            pl.BlockSpec((1, 1, sb, 2 * ML_HPB), lambda b, hg, s: (b, hg, s, 0)),
            pl.BlockSpec((1, 1, 2 * ML_HPB, sb), lambda b, hg, s: (b, hg, 0, s)),
            pl.BlockSpec((1, v_w), lambda b, hg, s: (0, hg)),
            c_spec, state_spec, state_spec,
        ],
        out_specs=[
            pl.BlockSpec((1, sb, v_w), lambda b, hg, s: (b, s, hg)),
            c_spec, state_spec, state_spec,
        ],
        out_shape=[
            jax.ShapeDtypeStruct((bsz, seq, ML_HEADS * ML_V), out_dtype),
            jax.ShapeDtypeStruct((bsz, ML_HEADS, ML_V, ML_QK), F32),
            jax.ShapeDtypeStruct((bsz, hg_n, ML_HPB, LANES), F32),
            jax.ShapeDtypeStruct((bsz, hg_n, ML_HPB, LANES), F32),
        ],
        scratch_shapes=[
            pltpu.VMEM((ML_HPB, ML_QK, ML_V), F32),
            pltpu.VMEM((ML_HPB, 1, ML_QK), F32),
            pltpu.VMEM((ML_HPB, 1, LANES), F32),
        ],
        compiler_params=_cparams(("parallel", "parallel", "arbitrary")),
        name=name,
    )(u3, u3, *([u3] * (2 * nvb)), gc, gr, mlg, c0, n0r, m0r)
    return h, c_new, n_new.reshape(bsz, ML_HEADS, ML_QK), m_new[..., 0].reshape(bsz, ML_HEADS)


def _gate_layouts(gates, bsz, seq, seq_pad):
    hg_n = ML_HEADS // ML_HPB
    ig = gates[:, :ML_HEADS].reshape(bsz, seq, ML_HEADS)
    lf = gates[:, ML_HEADS:2 * ML_HEADS].reshape(bsz, seq, ML_HEADS)
    if seq_pad > seq:
        ig = jnp.pad(ig, ((0, 0), (0, seq_pad - seq), (0, 0)), constant_values=-jnp.inf)
        lf = jnp.pad(lf, ((0, 0), (0, seq_pad - seq), (0, 0)))
    ig = ig.reshape(bsz, seq_pad, hg_n, ML_HPB).transpose(0, 2, 1, 3)
    lf = lf.reshape(bsz, seq_pad, hg_n, ML_HPB).transpose(0, 2, 1, 3)
    gc = jnp.concatenate([ig, lf], axis=-1)
    return gc, gc.transpose(0, 1, 3, 2)


def _router_kernel(x_ref, g_ref, w_ref, b_ref, xn_ref, route_ref, *, precise):
    xf = x_ref[...]
    r = lax.rsqrt(jnp.mean(xf * xf, axis=-1, keepdims=True) + EPS)
    xn = (xf * r) * g_ref[...]
    xn_ref[...] = xn.astype(xn_ref.dtype)
    logits = _mm(xn, w_ref[...], precise) + b_ref[...]
    lane = lax.broadcasted_iota(jnp.int32, logits.shape, 1).astype(F32)
    neg = jnp.float32(-jnp.inf)
    big = jnp.float32(LANES)
    gl = jnp.where(lane < N_GROUPS, logits, neg)
    g_max = jnp.max(gl, axis=-1, keepdims=True)
    g_val = 1.0 / jnp.sum(jnp.exp(gl - g_max), axis=-1, keepdims=True)
    g_idx = jnp.min(jnp.where(gl == g_max, lane, big), axis=-1, keepdims=True)
    lo = N_GROUPS + EXP_PER_GROUP * g_idx
    es = jnp.where((lane >= lo) & (lane < lo + EXP_PER_GROUP), logits, neg)
    t0 = jnp.max(es, axis=-1, keepdims=True)
    i0 = jnp.min(jnp.where(es == t0, lane, big), axis=-1, keepdims=True)
    es1 = jnp.where(lane == i0, neg, es)
    t1 = jnp.max(es1, axis=-1, keepdims=True)
    i1 = jnp.min(jnp.where(es1 == t1, lane, big), axis=-1, keepdims=True)
    e1 = jnp.exp(t1 - t0)
    w0 = g_val / (1.0 + e1)
    w1 = g_val * e1 / (1.0 + e1)
    route = jnp.where(lane == 0, i0 - N_GROUPS,
                      jnp.where(lane == 1, i1 - N_GROUPS,
                                jnp.where(lane == 2, w0, jnp.where(lane == 3, w1, 0.0))))
    route_ref[...] = route


def _router(x, g, w_r, b_r, *, tm, precise, name):
    m, d = x.shape
    return pl.pallas_call(
        functools.partial(_router_kernel, precise=precise),
        grid=(m // tm,),
        in_specs=[
            pl.BlockSpec((tm, d), lambda i: (i, 0)),
            pl.BlockSpec((1, d), lambda i: (0, 0)),
            pl.BlockSpec((d, LANES), lambda i: (0, 0)),
            pl.BlockSpec((1, LANES), lambda i: (0, 0)),
        ],
        out_specs=[pl.BlockSpec((tm, d), lambda i: (i, 0)), pl.BlockSpec((tm, LANES), lambda i: (i, 0))],
        out_shape=[jax.ShapeDtypeStruct((m, d), F32), jax.ShapeDtypeStruct((m, LANES), F32)],
        compiler_params=_cparams(("parallel",)),
        name=name,
    )(x, g, w_r, b_r)


def _ffn_kernel(te_ref, nv_ref, xs_ref, wr_ref, wg_ref, wu_ref, wd_ref, y_ref, wg_s, wu_s, wd_s, *, precise):
    i = pl.program_id(0)
    valid = i < nv_ref[0]
    prev = te_ref[jnp.maximum(i - 1, 0)]

    @pl.when(valid & ((i == 0) | (te_ref[i] != prev)))
    def _():
        wg_s[...] = wg_ref[...].astype(wg_s.dtype)
        wu_s[...] = wu_ref[...].astype(wu_s.dtype)
        wd_s[...] = wd_ref[...].astype(wd_s.dtype)

    @pl.when(valid)
    def _():
        xs = xs_ref[...]
        gt = _mm(xs, wg_s[...], precise)
        up = _mm(xs, wu_s[...], precise)
        y_ref[...] = _mm((gt * _sigmoid(gt) * up) * wr_ref[...], wd_s[...], precise)


def _ffn(xs, wrow, te, nv, w_gate, w_up, w_down, *, li, tm, precise, name):
    rows, d = xs.shape
    n_tiles = rows // tm
    f = EXPERT_FF
    wdt = F32 if precise else BF16

    def row_blk(i, te, nv):
        return (jnp.minimum(i, nv[0] - 1), 0)

    return pl.pallas_call(
        functools.partial(_ffn_kernel, precise=precise),
        grid_spec=pltpu.PrefetchScalarGridSpec(
            num_scalar_prefetch=2,
            grid=(n_tiles,),
            in_specs=[
                pl.BlockSpec((tm, d), row_blk),
                pl.BlockSpec((tm, 1), row_blk),
                pl.BlockSpec((None, d, f), lambda i, te, nv: (li, 0, te[i])),
                pl.BlockSpec((None, d, f), lambda i, te, nv: (li, 0, te[i])),
                pl.BlockSpec((None, f, d), lambda i, te, nv: (li, te[i], 0)),
            ],
            out_specs=pl.BlockSpec((tm, d), row_blk),
            scratch_shapes=[pltpu.VMEM((d, f), wdt), pltpu.VMEM((d, f), wdt), pltpu.VMEM((f, d), wdt)],
        ),
        out_shape=jax.ShapeDtypeStruct((rows, d), F32),
        compiler_params=_cparams(("arbitrary",)),
        name=name,
    )(te, nv, xs, wrow, w_gate, w_up, w_down)


def _rank_kernel(route_ref, offs_ref, dest_ref, run_s):
    @pl.when(pl.program_id(0) == 0)
    def _():
        run_s[...] = jnp.zeros(run_s.shape, run_s.dtype)

    route = route_ref[...]
    tm = route.shape[0]
    lane = lax.broadcasted_iota(jnp.int32, route.shape, 1).astype(F32)
    is0 = lane == route[:, 0:1]
    is1 = lane == route[:, 1:2]
    onehot = jnp.where(is0, 1.0, 0.0) + jnp.where(is1, 1.0, 0.0)
    r = lax.broadcasted_iota(jnp.int32, (tm, tm), 0)
    c = lax.broadcasted_iota(jnp.int32, (tm, tm), 1)
    earlier = jnp.where(c < r, 1.0, 0.0)
    before = _mm(earlier, onehot)
    base = offs_ref[...] + run_s[...] + before
    d0 = jnp.sum(jnp.where(is0, base, 0.0), axis=-1, keepdims=True)
    d1 = jnp.sum(jnp.where(is1, base, 0.0), axis=-1, keepdims=True)
    run_s[...] = run_s[...] + jnp.sum(onehot, axis=0, keepdims=True)
    dest_ref[...] = jnp.where(lane == 0.0, d0, jnp.where(lane == 1.0, d1, 0.0)).astype(jnp.int32)


def _rank(route, offs_row, *, tm, name):
    m = route.shape[0]
    return pl.pallas_call(
        _rank_kernel,
        grid=(m // tm,),
        in_specs=[pl.BlockSpec((tm, LANES), lambda i: (i, 0)), pl.BlockSpec((1, LANES), lambda i: (0, 0))],
        out_specs=pl.BlockSpec((tm, LANES), lambda i: (i, 0)),
        out_shape=jax.ShapeDtypeStruct((m, LANES), jnp.int32),
        scratch_shapes=[pltpu.VMEM((1, LANES), F32)],
        compiler_params=_cparams(("arbitrary",)),
        name=name,
    )(route, offs_row)


ROW_DMA_UNROLL = 8


def _row_copy(src, src_row, dst, dst_row, sem):
    return pltpu.make_async_copy(src.at[pl.ds(src_row, 1)], dst.at[pl.ds(dst_row, 1)], sem)


def _scatter_kernel(dest_ref, last_ref, xn_ref, xs_hbm, zero_s, sem, zsem, *, tile):
    i = pl.program_id(0)
    tm = xn_ref.shape[0]

    def zero_copy(e):
        return pltpu.make_async_copy(zero_s, xs_hbm.at[pl.ds(pl.multiple_of(last_ref[e], tile), tile)], zsem)

    @pl.when(i == 0)
    def _():
        zero_s[...] = jnp.zeros(zero_s.shape, zero_s.dtype)

        def z_start(e, carry):
            @pl.when(last_ref[e] >= 0)
            def _():
                zero_copy(e).start()
            return carry

        def z_wait(e, carry):
            @pl.when(last_ref[e] >= 0)
            def _():
                zero_copy(e).wait()
            return carry

        lax.fori_loop(0, N_EXPERTS, z_start, 0)
        lax.fori_loop(0, N_EXPERTS, z_wait, 0)

    base = i * (2 * tm)

    def start(r, carry):
        _row_copy(xn_ref, r, xs_hbm, dest_ref[base + 2 * r], sem).start()
        _row_copy(xn_ref, r, xs_hbm, dest_ref[base + 2 * r + 1], sem).start()
        return carry

    def wait(r, carry):
        _row_copy(xn_ref, r, xs_hbm, dest_ref[base + 2 * r], sem).wait()
        _row_copy(xn_ref, r, xs_hbm, dest_ref[base + 2 * r + 1], sem).wait()
        return carry

    lax.fori_loop(0, tm, start, 0, unroll=ROW_DMA_UNROLL)
    lax.fori_loop(0, tm, wait, 0, unroll=ROW_DMA_UNROLL)


def _scatter(dest, last_tile, xn, *, rows, tm, tile, name):
    m, d = xn.shape
    return pl.pallas_call(
        functools.partial(_scatter_kernel, tile=tile),
        grid_spec=pltpu.PrefetchScalarGridSpec(
            num_scalar_prefetch=2,
            grid=(m // tm,),
            in_specs=[pl.BlockSpec((tm, d), lambda i, dest, last: (i, 0))],
            out_specs=pl.BlockSpec(memory_space=pl.ANY),
            scratch_shapes=[pltpu.VMEM((tile, d), F32), pltpu.SemaphoreType.DMA(()), pltpu.SemaphoreType.DMA(())],
        ),
        out_shape=jax.ShapeDtypeStruct((rows, d), F32),
        compiler_params=_cparams(("arbitrary",)),
        name=name,
    )(dest, last_tile, xn)


def _combine_kernel(dest_ref, x_ref, wt_ref, y_hbm, o_ref, ya, yb, sem):
    i = pl.program_id(0)
    tm = x_ref.shape[0]
    base = i * (2 * tm)

    def start(r, carry):
        _row_copy(y_hbm, dest_ref[base + 2 * r], ya, r, sem).start()
        _row_copy(y_hbm, dest_ref[base + 2 * r + 1], yb, r, sem).start()
        return carry

    def wait(r, carry):
        _row_copy(y_hbm, dest_ref[base + 2 * r], ya, r, sem).wait()
        _row_copy(y_hbm, dest_ref[base + 2 * r + 1], yb, r, sem).wait()
        return carry

    lax.fori_loop(0, tm, start, 0, unroll=ROW_DMA_UNROLL)
    lax.fori_loop(0, tm, wait, 0, unroll=ROW_DMA_UNROLL)
    wt = wt_ref[...]
    o_ref[...] = x_ref[...] + (wt[:, 2:3] * ya[...] + wt[:, 3:4] * yb[...])


def _combine(dest, x, wts, y, *, tm, name):
    m, d = x.shape
    return pl.pallas_call(
        _combine_kernel,
        grid_spec=pltpu.PrefetchScalarGridSpec(
            num_scalar_prefetch=1,
            grid=(m // tm,),
            in_specs=[
                pl.BlockSpec((tm, d), lambda i, dest: (i, 0)),
                pl.BlockSpec((tm, LANES), lambda i, dest: (i, 0)),
                pl.BlockSpec(memory_space=pl.ANY),
            ],
            out_specs=pl.BlockSpec((tm, d), lambda i, dest: (i, 0)),
            scratch_shapes=[pltpu.VMEM((tm, d), F32), pltpu.VMEM((tm, d), F32), pltpu.SemaphoreType.DMA(())],
        ),
        out_shape=jax.ShapeDtypeStruct((m, d), F32),
        compiler_params=_cparams(("arbitrary",)),
        name=name,
    )(dest, x, wts, y)


def _moe(x, g, w_r, b_r, w_gate, w_up, w_down, *, li, tm_tok, tm_ffn, gate_before_down, precise, name):
    m = x.shape[0]
    xn, route = _router(x, g, w_r, b_r, tm=tm_tok, precise=precise, name=name + "_router")
    eid = route[:, 0:2].astype(jnp.int32)
    counts = jnp.sum((eid[:, :, None] == jnp.arange(N_EXPERTS, dtype=jnp.int32)).astype(jnp.int32), axis=(0, 1))
    padded = ((counts + tm_ffn - 1) // tm_ffn) * tm_ffn
    ends = jnp.cumsum(padded)
    offs = ends - padded
    rows = 2 * m + N_EXPERTS * tm_ffn
    tile_start = jnp.arange(rows // tm_ffn, dtype=jnp.int32) * tm_ffn
    te = jnp.minimum(jnp.sum((ends[None, :] <= tile_start[:, None]).astype(jnp.int32), axis=1), N_EXPERTS - 1)
    nv = (ends[-1] // tm_ffn).reshape(1)
    last_tile = jnp.where(padded > 0, ends - tm_ffn, -1)
    offs_row = jnp.pad(offs.astype(F32), (0, LANES - N_EXPERTS)).reshape(1, LANES)
    dest = _rank(route, offs_row, tm=tm_tok, name=name + "_rank")[:, 0:2].reshape(-1)
    xs = _scatter(dest, last_tile, xn, rows=rows, tm=tm_tok, tile=tm_ffn, name=name + "_scatter")
    if gate_before_down:
        wrow = jnp.zeros((rows, 1), F32).at[dest, 0].set(route[:, 2:4].reshape(-1))
        wts = jnp.ones_like(route)
    else:
        wrow = jnp.ones((rows, 1), F32)
        wts = route
    y = _ffn(xs, wrow, te, nv, w_gate, w_up, w_down, li=li, tm=tm_ffn, precise=precise, name=name + "_ffn")
    return _combine(dest, x, wts, y, tm=tm_tok, name=name + "_combine")


def _rmsnorm_kernel(x_ref, g_ref, o_ref):
    xf = x_ref[...]
    r = lax.rsqrt(jnp.mean(xf * xf, axis=-1, keepdims=True) + EPS)
    o_ref[...] = (xf * r) * g_ref[...]


def _rmsnorm(x, g, *, tm, name):
    m, d = x.shape
    return pl.pallas_call(
        _rmsnorm_kernel,
        grid=(m // tm,),
        in_specs=[pl.BlockSpec((tm, d), lambda i: (i, 0)), pl.BlockSpec((1, d), lambda i: (0, 0))],
        out_specs=pl.BlockSpec((tm, d), lambda i: (i, 0)),
        out_shape=jax.ShapeDtypeStruct((m, d), F32),
        compiler_params=_cparams(("parallel",)),
        name=name,
    )(x, g)


def _layer_weights(p, li):
    b_if = jnp.pad(p["b_if"][li], (0, LANES - 2 * ML_HEADS)).reshape(1, LANES)
    w_r = jnp.pad(jnp.concatenate([p["w_rg"][li], p["w_re"][li]], axis=1),
                  ((0, 0), (0, LANES - N_GROUPS - N_EXPERTS)))
    b_r = jnp.pad(jnp.concatenate([p["b_rg"][li], p["b_re"][li]]), (0, LANES - N_GROUPS - N_EXPERTS)).reshape(1, LANES)
    return b_if, w_r, b_r


def _run_trunk(x3, p, layer_w, *, caches, c0s, n0s, m0s, pos0):
    bsz, seq, d = x3.shape
    m = bsz * seq
    x = x3.reshape(m, d)
    precise = False
    if caches is None:
        act_dtype, tag = BF16, "p"
        tm, tn, tm_tok, tm_ffn, seq_pad = 1024, 512, 256, 256, seq
    else:
        act_dtype, tag = F32, "s"
        tm, tn, tm_tok, tm_ffn, seq_pad = m, 512, m, 16, 16
    cos_t, sin_t = _rope_tables(pos0 + jnp.arange(seq_pad, dtype=jnp.int32))
    zeros_bias = jnp.zeros((1, N_MAIN), F32)
    new_kv = [[] for _ in ATT_GROUPS]
    c_all, n_all, m_all = [], [], []
    for li in range(DEPTH):
        b_if, w_r, b_r = layer_w[li]
        g_attn = p["attn_norm_g"][li].reshape(1, d)
        common = dict(li=li, tm=tm, precise=precise)
        u = _norm_matmul(x, g_attn, p["w_in"], zeros_bias, row0=0, n_out=N_MAIN, tn=TN_MAIN,
                         out_dtype=act_dtype, act="none", name=f"{tag}{li}_inproj", **common)
        sg = _norm_matmul(x, g_attn, p["w_in"], zeros_bias, row0=O_GA, n_out=2 * D_MODEL, tn=tn,
                          out_dtype=act_dtype, act="sigmoid", name=f"{tag}{li}_gateproj", **common)
        gates = _norm_matmul(x, g_attn, p["w_in"], b_if, row0=O_MI, n_out=LANES, tn=LANES,
                             out_dtype=F32, act="gates", name=f"{tag}{li}_ifproj", **common)
        u3 = u.reshape(bsz, seq, N_MAIN)
        if seq_pad > seq:
            u3 = jnp.pad(u3, ((0, 0), (0, seq_pad - seq), (0, 0)))
        if caches is None:
            att3, kcs, vcs = _attn_prompt(u3, cos_t, sin_t, name=f"{tag}{li}_attn")
            for gi in range(N_ATT_GROUPS):
                keep = kcs[gi].shape[1]
                new_kv[gi].append(jnp.stack([kcs[gi].reshape(bsz, keep, ATT_HEADS, HEAD_DIM),
                                             vcs[gi].reshape(bsz, keep, ATT_HEADS, HEAD_DIM)], axis=2))
            att = att3.reshape(m, ATT_WIDTH)
        else:
            caches2d = [c.reshape(-1, HEAD_DIM) for c in caches]
            att3, kn3 = _attn_sample(u3, caches2d, cos_t, sin_t, li=li, n_new=seq, name=f"{tag}{li}_attn")
            att = att3[:, :seq].reshape(m, ATT_WIDTH)
            nh = N_ATT_GROUPS * ATT_HEADS
            k_new = kn3[:, :seq].reshape(bsz, seq, nh, HEAD_DIM)
            v_new = u3[:, :seq, O_AV:O_AV + nh * HEAD_DIM].reshape(bsz, seq, nh, HEAD_DIM)
            for gi in range(N_ATT_GROUPS):
                sl = slice(gi * ATT_HEADS, (gi + 1) * ATT_HEADS)
                new_kv[gi].append(jnp.stack([k_new[:, :, sl], v_new[:, :, sl]], axis=2))
        gc, gr = _gate_layouts(gates, bsz, seq, seq_pad)
        mlg = p["ml_norm_g"][li].reshape(1, ML_HEADS * ML_V)
        hg3, c_new, n_new, m_new = _mlstm(
            u3, gc, gr, mlg, c0s[li], n0s[li], m0s[li], chunk=min(128, seq_pad), sb=min(512, seq_pad),
            out_dtype=act_dtype, precise=precise, name=f"{tag}{li}_mlstm")
        hg = hg3[:, :seq].reshape(m, ML_HEADS * ML_V)
        c_all.append(c_new)
        n_all.append(n_new)
        m_all.append(m_new)
        merged = _merge(att, hg, sg, p["w_pa"], p["w_pm"], li=li, tm=tm, tn=tn, out_dtype=act_dtype,
                        precise=precise, name=f"{tag}{li}_merge")
        x = _outproj(x, merged, p["w_out"], li=li, tm=tm, tn=tn, precise=precise, name=f"{tag}{li}_outproj")
        x = _moe(x, p["ffn_norm_g"][li].reshape(1, d), w_r, b_r, p["w_gate"], p["w_up"], p["w_down"],
                 li=li, tm_tok=tm_tok, tm_ffn=tm_ffn, gate_before_down=caches is not None, precise=precise,
                 name=f"{tag}{li}_moe")
    y = _rmsnorm(x, p["final_norm_g"].reshape(1, d), tm=min(m, 512), name=f"{tag}_final_norm")
    new_kv = [jnp.stack(a) for a in new_kv]
    if caches is not None:
        shift = [(0, 0, 0)] * 2 + [(-seq, seq, 0)] + [(0, 0, 0)] * 3
        new_kv = [lax.dynamic_update_slice_in_dim(lax.pad(c, jnp.float32(0), shift), fresh, c.shape[2] - seq, axis=2)
                  for c, fresh in zip(caches, new_kv)]
    return (y.reshape(bsz, seq, d), new_kv, jnp.stack(c_all), jnp.stack(n_all), jnp.stack(m_all))


def kernel(x_prompt, x_sample, cache_kv_w128, cache_kv_w512, cache_kv_w2048, state_C, state_n, state_m,
           attn_norm_g, w_in, b_if, ml_norm_g, w_pa, w_pm, w_out, ffn_norm_g, w_rg, b_rg, w_re, b_re,
           w_gate, w_up, w_down, final_norm_g):
    p = dict(attn_norm_g=attn_norm_g, w_in=jnp.swapaxes(w_in, 1, 2), b_if=b_if, ml_norm_g=ml_norm_g, w_pa=w_pa, w_pm=w_pm,
             w_out=w_out, ffn_norm_g=ffn_norm_g, w_rg=w_rg, b_rg=b_rg, w_re=w_re, b_re=b_re,
             w_gate=w_gate, w_up=w_up, w_down=w_down, final_norm_g=final_norm_g)
    layer_w = [_layer_weights(p, li) for li in range(DEPTH)]
    bp = x_prompt.shape[0]
    c0 = jnp.zeros((DEPTH, bp, ML_HEADS, ML_V, ML_QK), F32)
    n0 = jnp.zeros((DEPTH, bp, ML_HEADS, ML_QK), F32)
    m0 = jnp.zeros((DEPTH, bp, ML_HEADS), F32)
    y_p, p_kv, p_c, p_n, p_m = _run_trunk(x_prompt, p, layer_w, caches=None,
                                          c0s=c0, n0s=n0, m0s=m0, pos0=0)
    y_s, s_kv, s_c, s_n, s_m = _run_trunk(x_sample, p, layer_w,
                                          caches=[cache_kv_w128, cache_kv_w512, cache_kv_w2048],
                                          c0s=state_C, n0s=state_n, m0s=state_m, pos0=PAST_LEN)
    return (y_p, y_s, p_kv[0], p_kv[1], p_kv[2], p_c, p_n, p_m,
            s_kv[0], s_kv[1], s_kv[2], s_c, s_n, s_m)
```

```python
import functools

import jax
import jax.numpy as jnp
from jax import lax
from jax.experimental import pallas as pl
from jax.experimental.pallas import tpu as pltpu

F32 = jnp.float32
BF16 = jnp.bfloat16

D_MODEL = 2048
DEPTH = 2
PAST_LEN = 16384
ATT_GROUPS = ((128, 1), (512, 4), (2048, 16))
N_ATT_GROUPS = 3
ATT_HEADS = 4
HEAD_DIM = 128
ATT_WIDTH = ATT_HEADS * HEAD_DIM
ATT_SCALE = HEAD_DIM ** -0.5
ROPE_DIM = HEAD_DIM // 4
ROPE_THETA = 500000.0
ML_HEADS = 8
ML_QK = 128
ML_V = 256
ML_K_SCALE = ML_QK ** -0.5
N_GROUPS = 4
EXP_PER_GROUP = 8
N_EXPERTS = N_GROUPS * EXP_PER_GROUP
EXPERT_FF = 256
EPS = 1e-6

O_AQ, O_AK, O_AV = 0, 1536, 3072
O_MQ, O_MK, O_MV, O_MO = 4608, 5632, 6656, 8704
O_MI, O_MF, O_GA, O_GB = 10752, 10760, 10768, 12816
N_MAIN = O_MI
LANES = 128
SUBLANES = 8
TN_MAIN = 768
VMEM_LIMIT_MB = 56


def _cparams(sem):
    return pltpu.CompilerParams(dimension_semantics=sem, vmem_limit_bytes=VMEM_LIMIT_MB * 1024 * 1024)


def _dot(a, b, dims, precise):
    dn = (dims, ((), ()))
    if precise:
        return lax.dot_general(a.astype(F32), b.astype(F32), dn, preferred_element_type=F32,
                               precision=lax.Precision.HIGHEST)
    return lax.dot_general(a.astype(BF16), b.astype(BF16), dn, preferred_element_type=F32)


def _mm(a, b, precise=False):
    return _dot(a, b, ((1,), (0,)), precise)


def _mm_nt(a, b, precise=False):
    return _dot(a, b, ((1,), (1,)), precise)


def _mm_tn(a, b, precise=False):
    return _dot(a, b, ((0,), (0,)), precise)


def _rb(x, precise=False):
    return x.astype(F32) if precise else x.astype(BF16).astype(F32)


def _sigmoid(z):
    return 1.0 / (1.0 + jnp.exp(-z))


def _log_sigmoid(z):
    return jnp.minimum(z, 0.0) - jnp.log1p(jnp.exp(-jnp.abs(z)))


def _norm_matmul_kernel(x_ref, g_ref, w_ref, b_ref, o_ref, xn_ref, *, act, precise):
    @pl.when(pl.program_id(1) == 0)
    def _():
        xf = x_ref[...]
        r = lax.rsqrt(jnp.mean(xf * xf, axis=-1, keepdims=True) + EPS)
        xn_ref[...] = ((xf * r) * g_ref[...]).astype(xn_ref.dtype)

    acc = _mm_nt(xn_ref[...], w_ref[0], precise)
    if act == "sigmoid":
        acc = _sigmoid(acc)
    elif act == "gates":
        z = acc + b_ref[...]
        lane = lax.broadcasted_iota(jnp.int32, z.shape, 1)
        acc = jnp.where(lane < ML_HEADS, z, _log_sigmoid(z))
    o_ref[...] = acc.astype(o_ref.dtype)


def _norm_matmul(x, g, wt, bias, *, li, row0, n_out, tm, tn, out_dtype, act, precise, name):
    m, d = x.shape
    assert row0 % SUBLANES == 0 and tn % SUBLANES == 0
    w_spec = pl.BlockSpec((pl.Element(1), pl.Element(tn), pl.Element(d)),
                          lambda i, j: (li, pl.multiple_of(row0 + j * tn, SUBLANES), 0))
    return pl.pallas_call(
        functools.partial(_norm_matmul_kernel, act=act, precise=precise),
        grid=(m // tm, n_out // tn),
        in_specs=[
            pl.BlockSpec((tm, d), lambda i, j: (i, 0)),
            pl.BlockSpec((1, d), lambda i, j: (0, 0)),
            w_spec,
            pl.BlockSpec((1, tn), lambda i, j: (0, j)),
        ],
        out_specs=pl.BlockSpec((tm, tn), lambda i, j: (i, j)),
        out_shape=jax.ShapeDtypeStruct((m, n_out), out_dtype),
        scratch_shapes=[pltpu.VMEM((tm, d), F32 if precise else BF16)],
        compiler_params=_cparams(("parallel", "arbitrary")),
        name=name,
    )(x, g, wt, bias)


def _merge_kernel(att_ref, hg_ref, sa_ref, sb_ref, wpa_ref, wpm_ref, o_ref, *, precise):
    a = _mm(att_ref[...], wpa_ref[...], precise)
    m = _mm(hg_ref[...], wpm_ref[...], precise)
    o_ref[...] = (sa_ref[...].astype(F32) * a + sb_ref[...].astype(F32) * m).astype(o_ref.dtype)


def _merge(att, hg, sg, w_pa, w_pm, *, li, tm, tn, out_dtype, precise, name):
    m = att.shape[0]
    nb = D_MODEL // tn
    return pl.pallas_call(
        functools.partial(_merge_kernel, precise=precise),
        grid=(m // tm, nb),
        in_specs=[
            pl.BlockSpec((tm, ATT_WIDTH), lambda i, j: (i, 0)),
            pl.BlockSpec((tm, ML_HEADS * ML_V), lambda i, j: (i, 0)),
            pl.BlockSpec((tm, tn), lambda i, j: (i, j)),
            pl.BlockSpec((tm, tn), lambda i, j: (i, j + nb)),
            pl.BlockSpec((None, ATT_WIDTH, tn), lambda i, j: (li, 0, j)),
            pl.BlockSpec((None, ML_HEADS * ML_V, tn), lambda i, j: (li, 0, j)),
        ],
        out_specs=pl.BlockSpec((tm, tn), lambda i, j: (i, j)),
        out_shape=jax.ShapeDtypeStruct((m, D_MODEL), out_dtype),
        compiler_params=_cparams(("parallel", "arbitrary")),
        name=name,
    )(att, hg, sg, sg, w_pa, w_pm)


def _outproj_kernel(x_ref, mg_ref, w_ref, o_ref, *, precise):
    o_ref[...] = x_ref[...] + _mm(mg_ref[...], w_ref[...], precise)


def _outproj(x, mg, w_out, *, li, tm, tn, precise, name):
    m = x.shape[0]
    return pl.pallas_call(
        functools.partial(_outproj_kernel, precise=precise),
        grid=(m // tm, D_MODEL // tn),
        in_specs=[
            pl.BlockSpec((tm, tn), lambda i, j: (i, j)),
            pl.BlockSpec((tm, D_MODEL), lambda i, j: (i, 0)),
            pl.BlockSpec((None, D_MODEL, tn), lambda i, j: (li, 0, j)),
        ],
        out_specs=pl.BlockSpec((tm, tn), lambda i, j: (i, j)),
        out_shape=jax.ShapeDtypeStruct((m, D_MODEL), F32),
        compiler_params=_cparams(("parallel", "arbitrary")),
        name=name,
    )(x, mg, w_out)


def _rope_tables(pos):
    half = ROPE_DIM // 2
    inv = jnp.power(jnp.float32(ROPE_THETA), -jnp.arange(half, dtype=F32) / half)
    ang = pos.astype(F32)[:, None] * inv[None, :]
    n = pos.shape[0]
    cos_t = jnp.concatenate([jnp.cos(ang), jnp.cos(ang), jnp.ones((n, HEAD_DIM - ROPE_DIM), F32)], axis=-1)
    sin_t = jnp.concatenate([jnp.sin(ang), jnp.sin(ang), jnp.zeros((n, HEAD_DIM - ROPE_DIM), F32)], axis=-1)
    return cos_t, sin_t


def _rope_perm():
    half = ROPE_DIM // 2
    r = jnp.arange(HEAD_DIM)[:, None]
    c = jnp.arange(HEAD_DIM)[None, :]
    p = jnp.where((c < half) & (r == c + half), -1.0, 0.0) + jnp.where(
        (c >= half) & (c < ROPE_DIM) & (r == c - half), 1.0, 0.0)
    return p.astype(BF16)


ROPE_ROWS = 512
ATT_BLOCKS_PER_TRIP = 8


def _attn_prompt_kernel(cos_ref, sin_ref, perm_ref, *refs, seq):
    q_refs, k_refs, v_refs = refs[0:3], refs[3:6], refs[6:9]
    att_ref = refs[9]
    kc_refs, vc_refs = refs[10:13], refs[13:16]
    qs, ks, vs = refs[16:19]
    o_s, lse_s = refs[19:22], refs[22:25]
    perm = perm_ref[...]
    nk = ATT_GROUPS[0][0] // ATT_GROUPS[0][1]
    row = lax.broadcasted_iota(jnp.int32, (nk, nk), 0)
    col = lax.broadcasted_iota(jnp.int32, (nk, nk), 1)
    cur_ok = col <= row
    prev_ok = col >= row
    neg = jnp.float32(-jnp.inf)

    for g, (win, dil) in enumerate(ATT_GROUPS):
        assert win // dil == nk
        span = nk * dil
        shift = dil.bit_length() - 1

        def rope_body(c, carry, g=g):
            r0 = pl.multiple_of(c * ROPE_ROWS, ROPE_ROWS)
            cs = cos_ref[pl.ds(r0, ROPE_ROWS), :]
            sn = sin_ref[pl.ds(r0, ROPE_ROWS), :]
            qb = q_refs[g][0, pl.ds(r0, ROPE_ROWS), :]
            kb = k_refs[g][0, pl.ds(r0, ROPE_ROWS), :]
            qr = qb.astype(F32) * cs + jnp.dot(qb, perm, preferred_element_type=F32) * sn
            kr = kb.astype(F32) * cs + jnp.dot(kb, perm, preferred_element_type=F32) * sn
            qs[pl.ds(r0, ROPE_ROWS), :] = qr
            ks[pl.ds(r0, ROPE_ROWS), :] = kr
            vs[pl.ds(r0, ROPE_ROWS), :] = v_refs[g][0, pl.ds(r0, ROPE_ROWS), :].astype(F32)
            return carry

        lax.fori_loop(0, seq // ROPE_ROWS, rope_body, 0)
        keep = min(win, seq)
        kc_refs[g][0] = ks[seq - keep:seq, :]
        vc_refs[g][0] = vs[seq - keep:seq, :]

        def rows(start, dil=dil):
            if dil == 1:
                return pl.ds(pl.multiple_of(start, nk), nk)
            return pl.ds(start, nk, stride=dil)

        def blk_body(it, carry, g=g, dil=dil, span=span, shift=shift, rows=rows):
            blks = [it * ATT_BLOCKS_PER_TRIP + u for u in range(ATT_BLOCKS_PER_TRIP)]
            n = [blk >> shift for blk in blks]
            start = [n_u * span + (blk & (dil - 1)) for n_u, blk in zip(n, blks)]
            has_prev = [n_u > 0 for n_u in n]
            start_p = [jnp.where(hp, st - span, st) for hp, st in zip(has_prev, start)]
            q = [qs[rows(st), :] for st in start]
            s_c = [jnp.where(cur_ok, _mm_nt(q_u, ks[rows(st), :]) * ATT_SCALE, neg) for q_u, st in zip(q, start)]
            no_prev = [jnp.where(hp, jnp.float32(0.0), neg) for hp in has_prev]
            s_p = [jnp.where(prev_ok, _mm_nt(q_u, ks[rows(sp), :]) * ATT_SCALE + np_u, neg)
                   for q_u, sp, np_u in zip(q, start_p, no_prev)]
            m = [jnp.max(jnp.maximum(c_u, p_u), axis=-1, keepdims=True) for c_u, p_u in zip(s_c, s_p)]
            p_c = [jnp.exp(c_u - m_u) for c_u, m_u in zip(s_c, m)]
            p_p = [jnp.exp(p_u - m_u) for p_u, m_u in zip(s_p, m)]
            l = [jnp.sum(c_u + p_u, axis=-1, keepdims=True) for c_u, p_u in zip(p_c, p_p)]
            acc = [_mm(c_u, vs[rows(st), :]) + _mm(p_u, vs[rows(sp), :])
                   for c_u, p_u, st, sp in zip(p_c, p_p, start, start_p)]
            for u, st in enumerate(start):
                o_s[g][rows(st), :] = acc[u] * (1.0 / l[u])
                lse_s[g][rows(st), :] = jnp.broadcast_to(m[u] + jnp.log(l[u]), (nk, HEAD_DIM))
            return carry

        lax.fori_loop(0, seq // nk // ATT_BLOCKS_PER_TRIP, blk_body, 0)

    def out_body(c, carry):
        sl = pl.ds(pl.multiple_of(c * nk, nk), nk)
        lse = [lse_s[g][sl, :] for g in range(N_ATT_GROUPS)]
        top = functools.reduce(jnp.maximum, lse)
        w = [jnp.exp(x - top) for x in lse]
        num = sum(w[g] * o_s[g][sl, :] for g in range(N_ATT_GROUPS))
        att_ref[0, sl, :] = (num / sum(w)).astype(att_ref.dtype)
        return carry

    lax.fori_loop(0, seq // nk, out_body, 0)


def _attn_prompt(u3, cos_t, sin_t, *, name):
    bsz, seq, _ = u3.shape
    perm = _rope_perm()
    in_specs = [
        pl.BlockSpec((seq, HEAD_DIM), lambda b, h: (0, 0)),
        pl.BlockSpec((seq, HEAD_DIM), lambda b, h: (0, 0)),
        pl.BlockSpec((HEAD_DIM, HEAD_DIM), lambda b, h: (0, 0)),
    ]
    for off in (O_AQ, O_AK, O_AV):
        for g in range(N_ATT_GROUPS):
            blk0 = off // HEAD_DIM + g * ATT_HEADS
            in_specs.append(pl.BlockSpec((1, seq, HEAD_DIM), lambda b, h, blk0=blk0: (b, 0, blk0 + h)))
    keeps = [min(w, seq) for w, _ in ATT_GROUPS]
    out_specs = [pl.BlockSpec((1, seq, HEAD_DIM), lambda b, h: (b, 0, h))]
    out_shape = [jax.ShapeDtypeStruct((bsz, seq, ATT_WIDTH), BF16)]
    for _ in range(2):
        for keep in keeps:
            out_specs.append(pl.BlockSpec((1, keep, HEAD_DIM), lambda b, h: (b, 0, h)))
            out_shape.append(jax.ShapeDtypeStruct((bsz, keep, ATT_WIDTH), F32))
    outs = pl.pallas_call(
        functools.partial(_attn_prompt_kernel, seq=seq),
        grid=(bsz, ATT_HEADS),
        in_specs=in_specs,
        out_specs=out_specs,
        out_shape=out_shape,
        scratch_shapes=[pltpu.VMEM((seq, HEAD_DIM), F32) for _ in range(3 + 2 * N_ATT_GROUPS)],
        compiler_params=_cparams(("parallel", "arbitrary")),
        name=name,
    )(cos_t, sin_t, perm, *([u3] * 9))
    return outs[0], outs[1:4], outs[4:7]


def _attn_sample_kernel(cos_ref, sin_ref, q_ref, k_ref, v_ref, c0_ref, c1_ref, c2_ref,
                        att_ref, kn_ref, *, n_new):
    cache_refs = (c0_ref, c1_ref, c2_ref)
    cs = cos_ref[...]
    sn = sin_ref[...]
    rows_pad = q_ref.shape[1]
    half = ROPE_DIM // 2
    lane = lax.broadcasted_iota(jnp.int32, (rows_pad, HEAD_DIM), 1)

    def rot_half(x):
        return jnp.where(lane < half, -pltpu.roll(x, HEAD_DIM - half, axis=1),
                         jnp.where(lane < ROPE_DIM, pltpu.roll(x, half, axis=1), 0.0))

    nk = ATT_GROUPS[0][0] // ATT_GROUPS[0][1]
    kv_rows = 2 * ATT_HEADS
    w_iota = lax.broadcasted_iota(jnp.int32, (nk, 1), 0)
    u_iota = lax.broadcasted_iota(jnp.int32, (rows_pad, 1), 0)
    neg = jnp.float32(-jnp.inf)
    att_ref[...] = jnp.zeros(att_ref.shape, att_ref.dtype)

    for h in range(ATT_HEADS):
        parts = [[None] * N_ATT_GROUPS for _ in range(n_new)]
        for g, (win, dil) in enumerate(ATT_GROUPS):
            hs = (g * ATT_HEADS + h) * HEAD_DIM
            qh = q_ref[0, :, hs:hs + HEAD_DIM]
            kh = k_ref[0, :, hs:hs + HEAD_DIM]
            vh = v_ref[0, :, hs:hs + HEAD_DIM]
            qr = _rb(qh * cs + rot_half(qh) * sn)
            kr = kh * cs + rot_half(kh) * sn
            kn_ref[0, :, hs:hs + HEAD_DIM] = kr
            kr = _rb(kr)
            vh = _rb(vh)
            cref = cache_refs[g]
            for t in range(n_new):
                q_t = qr[t:t + 1, :]
                if dil == 1:
                    p0 = 0
                else:
                    p0 = t
                k_c = _rb(cref[pl.ds(p0 * kv_rows + h, nk, stride=kv_rows * dil), :])
                v_c = _rb(cref[pl.ds(p0 * kv_rows + ATT_HEADS + h, nk, stride=kv_rows * dil), :])
                s_c = jnp.sum(k_c * q_t, axis=-1, keepdims=True) * ATT_SCALE
                s_n = jnp.sum(kr * q_t, axis=-1, keepdims=True) * ATT_SCALE
                if dil == 1:
                    s_c = jnp.where(w_iota >= t, s_c, neg)
                    s_n = jnp.where(u_iota <= t, s_n, neg)
                else:
                    s_n = jnp.where(u_iota == t, s_n, neg)
                m = jnp.maximum(jnp.max(s_c, axis=0, keepdims=True), jnp.max(s_n, axis=0, keepdims=True))
                p_c = jnp.exp(s_c - m)
                p_n = jnp.exp(s_n - m)
                l = jnp.sum(p_c, axis=0, keepdims=True) + jnp.sum(p_n, axis=0, keepdims=True)
                acc = jnp.sum(_rb(p_c) * v_c, axis=0, keepdims=True) + jnp.sum(_rb(p_n) * vh, axis=0, keepdims=True)
                parts[t][g] = (m, l, acc)
        for t in range(n_new):
            m_all = functools.reduce(jnp.maximum, [p[0] for p in parts[t]])
            num = sum(jnp.exp(p[0] - m_all) * p[2] for p in parts[t])
            den = sum(jnp.exp(p[0] - m_all) * p[1] for p in parts[t])
            att_ref[0, t:t + 1, h * HEAD_DIM:(h + 1) * HEAD_DIM] = num / den


def _attn_sample(u3, caches2d, cos_t, sin_t, *, li, n_new, name):
    bsz, rows_pad, _ = u3.shape
    qkv_w = N_ATT_GROUPS * ATT_WIDTH
    in_specs = [
        pl.BlockSpec((rows_pad, HEAD_DIM), lambda b: (0, 0)),
        pl.BlockSpec((rows_pad, HEAD_DIM), lambda b: (0, 0)),
        pl.BlockSpec((1, rows_pad, qkv_w), lambda b: (b, 0, O_AQ // qkv_w)),
        pl.BlockSpec((1, rows_pad, qkv_w), lambda b: (b, 0, O_AK // qkv_w)),
        pl.BlockSpec((1, rows_pad, qkv_w), lambda b: (b, 0, O_AV // qkv_w)),
    ]
    for (win, _), c in zip(ATT_GROUPS, caches2d):
        rows = c.shape[0] // (DEPTH * bsz)
        in_specs.append(pl.BlockSpec((rows, HEAD_DIM), lambda b, bsz=bsz: (li * bsz + b, 0)))
    return pl.pallas_call(
        functools.partial(_attn_sample_kernel, n_new=n_new),
        grid=(bsz,),
        in_specs=in_specs,
        out_specs=[
            pl.BlockSpec((1, rows_pad, ATT_WIDTH), lambda b: (b, 0, 0)),
            pl.BlockSpec((1, rows_pad, qkv_w), lambda b: (b, 0, 0)),
        ],
        out_shape=[
            jax.ShapeDtypeStruct((bsz, rows_pad, ATT_WIDTH), F32),
            jax.ShapeDtypeStruct((bsz, rows_pad, qkv_w), F32),
        ],
        compiler_params=_cparams(("parallel",)),
        name=name,
    )(cos_t, sin_t, u3, u3, u3, *caches2d)


ML_HPB = 4
ML_VPB = 2


def _mlstm_kernel(q_ref, k_ref, *refs, chunk, precise):
    nvb = ML_HPB // ML_VPB
    v_refs, mo_refs = refs[:nvb], refs[nvb:2 * nvb]
    gc_ref, gr_ref, mlg_ref, c0_ref, n0_ref, m0_ref, h_ref, c_out, n_out, m_out, ct_s, n_s, m_s = refs[2 * nvb:]
    s_idx = pl.program_id(2)
    n_s_blocks = pl.num_programs(2)
    sb = q_ref.shape[1]
    L = chunk

    @pl.when(s_idx == 0)
    def _():
        for hh in range(ML_HPB):
            ct_s[hh] = c0_ref[0, hh].T
            n_s[hh] = n0_ref[0, 0, hh:hh + 1, :]
            m_s[hh] = m0_ref[0, 0, hh:hh + 1, :]

    row = lax.broadcasted_iota(jnp.int32, (L, L), 0)
    col = lax.broadcasted_iota(jnp.int32, (L, L), 1)
    causal = row >= col
    neg = jnp.float32(-jnp.inf)

    def chunk_body(c, carry):
        r0 = pl.multiple_of(c * L, L)
        gc = gc_ref[0, 0, pl.ds(r0, L), :]
        gr = gr_ref[0, 0, :, pl.ds(r0, L)]
        heads = range(ML_HPB)
        ig_col = [gc[:, hh:hh + 1] for hh in heads]
        lf_col = [gc[:, ML_HPB + hh:ML_HPB + hh + 1] for hh in heads]
        ig_row = [gr[hh:hh + 1, :] for hh in heads]
        lf_row = [gr[ML_HPB + hh:ML_HPB + hh + 1, :] for hh in heads]
        vcols = [slice((hh % ML_VPB) * ML_V, (hh % ML_VPB + 1) * ML_V) for hh in heads]
        qf = [q_ref[0, pl.ds(r0, L), hh * ML_QK:(hh + 1) * ML_QK] for hh in heads]
        ks = [k_ref[0, pl.ds(r0, L), hh * ML_QK:(hh + 1) * ML_QK].astype(F32) * ML_K_SCALE for hh in heads]
        vf = [v_refs[hh // ML_VPB][0, pl.ds(r0, L), vcols[hh]] for hh in heads]
        ct = [ct_s[hh] for hh in heads]
        n_row = [n_s[hh] for hh in heads]
        m_prev = [m_s[hh][:, 0:1] for hh in heads]
        qk = [_mm_nt(qf[hh], ks[hh], precise) for hh in heads]
        q_ct = [_mm(qf[hh], ct[hh], precise) for hh in heads]
        b_col = [jnp.sum(jnp.where(causal, lf_row[hh], 0.0), axis=1, keepdims=True) for hh in heads]
        b_row = [jnp.sum(jnp.where(row <= col, lf_col[hh], 0.0), axis=0, keepdims=True) for hh in heads]
        inter = [b_col[hh] + m_prev[hh] for hh in heads]
        dm = [jnp.where(causal, b_col[hh] - b_row[hh] + ig_row[hh], neg) for hh in heads]
        m_t = [jnp.maximum(inter[hh], jnp.max(dm[hh], axis=1, keepdims=True)) for hh in heads]
        w_inter = [jnp.exp(inter[hh] - m_t[hh]) for hh in heads]
        sw = [qk[hh] * jnp.exp(dm[hh] - m_t[hh]) for hh in heads]
        num = [w_inter[hh] * q_ct[hh] + _mm(sw[hh], vf[hh], precise) for hh in heads]
        den = [w_inter[hh] * jnp.sum(_rb(qf[hh], precise) * _rb(n_row[hh], precise), axis=1, keepdims=True)
               + jnp.sum(sw[hh], axis=1, keepdims=True) for hh in heads]
        hv = [num[hh] / jnp.maximum(jnp.abs(den[hh]), jnp.exp(-m_t[hh])) for hh in heads]
        hn = [hv[hh] * lax.rsqrt(jnp.mean(hv[hh] * hv[hh], axis=-1, keepdims=True) + EPS) for hh in heads]
        for hh in heads:
            mo = mo_refs[hh // ML_VPB][0, pl.ds(r0, L), vcols[hh]].astype(F32)
            gain = mlg_ref[:, hh * ML_V:(hh + 1) * ML_V]
            h_ref[0, pl.ds(r0, L), hh * ML_V:(hh + 1) * ML_V] = (hn[hh] * gain * _sigmoid(mo)).astype(h_ref.dtype)
        b_last = [b_col[hh][L - 1:L, :] for hh in heads]
        m_new = [m_t[hh][L - 1:L, :] for hh in heads]
        decay = [jnp.exp(b_last[hh] + m_prev[hh] - m_new[hh]) for hh in heads]
        wk_col = [jnp.exp(b_last[hh] - b_col[hh] + ig_col[hh] - m_new[hh]) for hh in heads]
        upd = [_mm_tn(ks[hh] * wk_col[hh], vf[hh], precise) for hh in heads]
        for hh in heads:
            ct_s[hh] = decay[hh] * ct[hh] + upd[hh]
            n_s[hh] = decay[hh] * n_row[hh] + jnp.sum(_rb(ks[hh], precise) * _rb(wk_col[hh], precise), axis=0,
                                                      keepdims=True)
            m_s[hh] = jnp.broadcast_to(m_new[hh], (1, LANES))
        return carry

    lax.fori_loop(0, sb // L, chunk_body, 0)

    @pl.when(s_idx == n_s_blocks - 1)
    def _():
        for hh in range(ML_HPB):
            c_out[0, hh] = ct_s[hh].T
            n_out[0, 0, hh:hh + 1, :] = n_s[hh]
            m_out[0, 0, hh:hh + 1, :] = m_s[hh]


def _mlstm(u3, gc, gr, mlg, c0, n0, m0, *, chunk, sb, out_dtype, precise, name):
    bsz, seq, _ = u3.shape
    hg_n = ML_HEADS // ML_HPB
    qk_w, v_w, vb_w = ML_HPB * ML_QK, ML_HPB * ML_V, ML_VPB * ML_V
    nvb = ML_HPB // ML_VPB
    assert O_MQ % qk_w == 0 and O_MK % qk_w == 0 and O_MV % vb_w == 0 and O_MO % vb_w == 0
    v_specs = [pl.BlockSpec((1, sb, vb_w), lambda b, hg, s, j=j, o=off // vb_w: (b, s, o + hg * nvb + j))
               for off in (O_MV, O_MO) for j in range(nvb)]
    n0r = n0.reshape(bsz, hg_n, ML_HPB, ML_QK)
    m0r = jnp.broadcast_to(m0.reshape(bsz, hg_n, ML_HPB, 1), (bsz, hg_n, ML_HPB, LANES))
    state_spec = pl.BlockSpec((1, 1, ML_HPB, LANES), lambda b, hg, s: (b, hg, 0, 0))
    c_spec = pl.BlockSpec((1, ML_HPB, ML_V, ML_QK), lambda b, hg, s: (b, hg, 0, 0))
    h, c_new, n_new, m_new = pl.pallas_call(
        functools.partial(_mlstm_kernel, chunk=chunk, precise=precise),
        grid=(bsz, hg_n, seq // sb),
        in_specs=[
            pl.BlockSpec((1, sb, qk_w), lambda b, hg, s: (b, s, O_MQ // qk_w + hg)),
            pl.BlockSpec((1, sb, qk_w), lambda b, hg, s: (b, s, O_MK // qk_w + hg)),
            *v_specs,
            pl.BlockSpec((1, 1, sb, 2 * ML_HPB), lambda b, hg, s: (b, hg, s, 0)),
            pl.BlockSpec((1, 1, 2 * ML_HPB, sb), lambda b, hg, s: (b, hg, 0, s)),
            pl.BlockSpec((1, v_w), lambda b, hg, s: (0, hg)),
            c_spec, state_spec, state_spec,
        ],
        out_specs=[
            pl.BlockSpec((1, sb, v_w), lambda b, hg, s: (b, s, hg)),
            c_spec, state_spec, state_spec,
        ],
        out_shape=[
            jax.ShapeDtypeStruct((bsz, seq, ML_HEADS * ML_V), out_dtype),
            jax.ShapeDtypeStruct((bsz, ML_HEADS, ML_V, ML_QK), F32),
            jax.ShapeDtypeStruct((bsz, hg_n, ML_HPB, LANES), F32),
            jax.ShapeDtypeStruct((bsz, hg_n, ML_HPB, LANES), F32),
        ],
        scratch_shapes=[
            pltpu.VMEM((ML_HPB, ML_QK, ML_V), F32),
            pltpu.VMEM((ML_HPB, 1, ML_QK), F32),
            pltpu.VMEM((ML_HPB, 1, LANES), F32),
        ],
        compiler_params=_cparams(("parallel", "parallel", "arbitrary")),
        name=name,
    )(u3, u3, *([u3] * (2 * nvb)), gc, gr, mlg, c0, n0r, m0r)
    return h, c_new, n_new.reshape(bsz, ML_HEADS, ML_QK), m_new[..., 0].reshape(bsz, ML_HEADS)


def _gate_layouts(gates, bsz, seq, seq_pad):
    hg_n = ML_HEADS // ML_HPB
    ig = gates[:, :ML_HEADS].reshape(bsz, seq, ML_HEADS)
    lf = gates[:, ML_HEADS:2 * ML_HEADS].reshape(bsz, seq, ML_HEADS)
    if seq_pad > seq:
        ig = jnp.pad(ig, ((0, 0), (0, seq_pad - seq), (0, 0)), constant_values=-jnp.inf)
        lf = jnp.pad(lf, ((0, 0), (0, seq_pad - seq), (0, 0)))
    ig = ig.reshape(bsz, seq_pad, hg_n, ML_HPB).transpose(0, 2, 1, 3)
    lf = lf.reshape(bsz, seq_pad, hg_n, ML_HPB).transpose(0, 2, 1, 3)
    gc = jnp.concatenate([ig, lf], axis=-1)
    return gc, gc.transpose(0, 1, 3, 2)


def _router_kernel(x_ref, g_ref, w_ref, b_ref, xn_ref, route_ref, *, precise):
    xf = x_ref[...]
    r = lax.rsqrt(jnp.mean(xf * xf, axis=-1, keepdims=True) + EPS)
    xn = (xf * r) * g_ref[...]
    xn_ref[...] = xn.astype(xn_ref.dtype)
    logits = _mm(xn, w_ref[...], precise) + b_ref[...]
    lane = lax.broadcasted_iota(jnp.int32, logits.shape, 1).astype(F32)
    neg = jnp.float32(-jnp.inf)
    big = jnp.float32(LANES)
    gl = jnp.where(lane < N_GROUPS, logits, neg)
    g_max = jnp.max(gl, axis=-1, keepdims=True)
    g_val = 1.0 / jnp.sum(jnp.exp(gl - g_max), axis=-1, keepdims=True)
    g_idx = jnp.min(jnp.where(gl == g_max, lane, big), axis=-1, keepdims=True)
    lo = N_GROUPS + EXP_PER_GROUP * g_idx
    es = jnp.where((lane >= lo) & (lane < lo + EXP_PER_GROUP), logits, neg)
    t0 = jnp.max(es, axis=-1, keepdims=True)
    i0 = jnp.min(jnp.where(es == t0, lane, big), axis=-1, keepdims=True)
    es1 = jnp.where(lane == i0, neg, es)
    t1 = jnp.max(es1, axis=-1, keepdims=True)
    i1 = jnp.min(jnp.where(es1 == t1, lane, big), axis=-1, keepdims=True)
    e1 = jnp.exp(t1 - t0)
    w0 = g_val / (1.0 + e1)
    w1 = g_val * e1 / (1.0 + e1)
    route = jnp.where(lane == 0, i0 - N_GROUPS,
                      jnp.where(lane == 1, i1 - N_GROUPS,
                                jnp.where(lane == 2, w0, jnp.where(lane == 3, w1, 0.0))))
    route_ref[...] = route


def _router(x, g, w_r, b_r, *, tm, precise, name):
    m, d = x.shape
    return pl.pallas_call(
        functools.partial(_router_kernel, precise=precise),
        grid=(m // tm,),
        in_specs=[
            pl.BlockSpec((tm, d), lambda i: (i, 0)),
            pl.BlockSpec((1, d), lambda i: (0, 0)),
            pl.BlockSpec((d, LANES), lambda i: (0, 0)),
            pl.BlockSpec((1, LANES), lambda i: (0, 0)),
        ],
        out_specs=[pl.BlockSpec((tm, d), lambda i: (i, 0)), pl.BlockSpec((tm, LANES), lambda i: (i, 0))],
        out_shape=[jax.ShapeDtypeStruct((m, d), F32), jax.ShapeDtypeStruct((m, LANES), F32)],
        compiler_params=_cparams(("parallel",)),
        name=name,
    )(x, g, w_r, b_r)


def _ffn_kernel(te_ref, nv_ref, xs_ref, wr_ref, wg_ref, wu_ref, wd_ref, y_ref, wg_s, wu_s, wd_s, *, precise):
    i = pl.program_id(0)
    valid = i < nv_ref[0]
    prev = te_ref[jnp.maximum(i - 1, 0)]

    @pl.when(valid & ((i == 0) | (te_ref[i] != prev)))
    def _():
        wg_s[...] = wg_ref[...].astype(wg_s.dtype)
        wu_s[...] = wu_ref[...].astype(wu_s.dtype)
        wd_s[...] = wd_ref[...].astype(wd_s.dtype)

    @pl.when(valid)
    def _():
        xs = xs_ref[...]
        gt = _mm(xs, wg_s[...], precise)
        up = _mm(xs, wu_s[...], precise)
        y_ref[...] = _mm((gt * _sigmoid(gt) * up) * wr_ref[...], wd_s[...], precise)


def _ffn(xs, wrow, te, nv, w_gate, w_up, w_down, *, li, tm, precise, name):
    rows, d = xs.shape
    n_tiles = rows // tm
    f = EXPERT_FF
    wdt = F32 if precise else BF16

    def row_blk(i, te, nv):
        return (jnp.minimum(i, nv[0] - 1), 0)

    return pl.pallas_call(
        functools.partial(_ffn_kernel, precise=precise),
        grid_spec=pltpu.PrefetchScalarGridSpec(
            num_scalar_prefetch=2,
            grid=(n_tiles,),
            in_specs=[
                pl.BlockSpec((tm, d), row_blk),
                pl.BlockSpec((tm, 1), row_blk),
                pl.BlockSpec((None, d, f), lambda i, te, nv: (li, 0, te[i])),
                pl.BlockSpec((None, d, f), lambda i, te, nv: (li, 0, te[i])),
                pl.BlockSpec((None, f, d), lambda i, te, nv: (li, te[i], 0)),
            ],
            out_specs=pl.BlockSpec((tm, d), row_blk),
            scratch_shapes=[pltpu.VMEM((d, f), wdt), pltpu.VMEM((d, f), wdt), pltpu.VMEM((f, d), wdt)],
        ),
        out_shape=jax.ShapeDtypeStruct((rows, d), F32),
        compiler_params=_cparams(("arbitrary",)),
        name=name,
    )(te, nv, xs, wrow, w_gate, w_up, w_down)


def _rank_kernel(route_ref, offs_ref, dest_ref, run_s):
    @pl.when(pl.program_id(0) == 0)
    def _():
        run_s[...] = jnp.zeros(run_s.shape, run_s.dtype)

    route = route_ref[...]
    tm = route.shape[0]
    lane = lax.broadcasted_iota(jnp.int32, route.shape, 1).astype(F32)
    is0 = lane == route[:, 0:1]
    is1 = lane == route[:, 1:2]
    onehot = jnp.where(is0, 1.0, 0.0) + jnp.where(is1, 1.0, 0.0)
    r = lax.broadcasted_iota(jnp.int32, (tm, tm), 0)
    c = lax.broadcasted_iota(jnp.int32, (tm, tm), 1)
    earlier = jnp.where(c < r, 1.0, 0.0)
    before = _mm(earlier, onehot)
    base = offs_ref[...] + run_s[...] + before
    d0 = jnp.sum(jnp.where(is0, base, 0.0), axis=-1, keepdims=True)
    d1 = jnp.sum(jnp.where(is1, base, 0.0), axis=-1, keepdims=True)
    run_s[...] = run_s[...] + jnp.sum(onehot, axis=0, keepdims=True)
    dest_ref[...] = jnp.where(lane == 0.0, d0, jnp.where(lane == 1.0, d1, 0.0)).astype(jnp.int32)


def _rank(route, offs_row, *, tm, name):
    m = route.shape[0]
    return pl.pallas_call(
        _rank_kernel,
        grid=(m // tm,),
        in_specs=[pl.BlockSpec((tm, LANES), lambda i: (i, 0)), pl.BlockSpec((1, LANES), lambda i: (0, 0))],
        out_specs=pl.BlockSpec((tm, LANES), lambda i: (i, 0)),
        out_shape=jax.ShapeDtypeStruct((m, LANES), jnp.int32),
        scratch_shapes=[pltpu.VMEM((1, LANES), F32)],
        compiler_params=_cparams(("arbitrary",)),
        name=name,
    )(route, offs_row)


ROW_DMA_UNROLL = 8


def _row_copy(src, src_row, dst, dst_row, sem):
    return pltpu.make_async_copy(src.at[pl.ds(src_row, 1)], dst.at[pl.ds(dst_row, 1)], sem)


def _scatter_kernel(dest_ref, last_ref, xn_ref, xs_hbm, zero_s, sem, zsem, *, tile):
    i = pl.program_id(0)
    tm = xn_ref.shape[0]

    def zero_copy(e):
        return pltpu.make_async_copy(zero_s, xs_hbm.at[pl.ds(pl.multiple_of(last_ref[e], tile), tile)], zsem)

    @pl.when(i == 0)
    def _():
        zero_s[...] = jnp.zeros(zero_s.shape, zero_s.dtype)

        def z_start(e, carry):
            @pl.when(last_ref[e] >= 0)
            def _():
                zero_copy(e).start()
            return carry

        def z_wait(e, carry):
            @pl.when(last_ref[e] >= 0)
            def _():
                zero_copy(e).wait()
            return carry

        lax.fori_loop(0, N_EXPERTS, z_start, 0)
        lax.fori_loop(0, N_EXPERTS, z_wait, 0)

    base = i * (2 * tm)

    def start(r, carry):
        _row_copy(xn_ref, r, xs_hbm, dest_ref[base + 2 * r], sem).start()
        _row_copy(xn_ref, r, xs_hbm, dest_ref[base + 2 * r + 1], sem).start()
        return carry

    def wait(r, carry):
        _row_copy(xn_ref, r, xs_hbm, dest_ref[base + 2 * r], sem).wait()
        _row_copy(xn_ref, r, xs_hbm, dest_ref[base + 2 * r + 1], sem).wait()
        return carry

    lax.fori_loop(0, tm, start, 0, unroll=ROW_DMA_UNROLL)
    lax.fori_loop(0, tm, wait, 0, unroll=ROW_DMA_UNROLL)


def _scatter(dest, last_tile, xn, *, rows, tm, tile, name):
    m, d = xn.shape
    return pl.pallas_call(
        functools.partial(_scatter_kernel, tile=tile),
        grid_spec=pltpu.PrefetchScalarGridSpec(
            num_scalar_prefetch=2,
            grid=(m // tm,),
            in_specs=[pl.BlockSpec((tm, d), lambda i, dest, last: (i, 0))],
            out_specs=pl.BlockSpec(memory_space=pl.ANY),
            scratch_shapes=[pltpu.VMEM((tile, d), F32), pltpu.SemaphoreType.DMA(()), pltpu.SemaphoreType.DMA(())],
        ),
        out_shape=jax.ShapeDtypeStruct((rows, d), F32),
        compiler_params=_cparams(("arbitrary",)),
        name=name,
    )(dest, last_tile, xn)


def _combine_kernel(dest_ref, x_ref, wt_ref, y_hbm, o_ref, ya, yb, sem):
    i = pl.program_id(0)
    tm = x_ref.shape[0]
    base = i * (2 * tm)

    def start(r, carry):
        _row_copy(y_hbm, dest_ref[base + 2 * r], ya, r, sem).start()
        _row_copy(y_hbm, dest_ref[base + 2 * r + 1], yb, r, sem).start()
        return carry

    def wait(r, carry):
        _row_copy(y_hbm, dest_ref[base + 2 * r], ya, r, sem).wait()
        _row_copy(y_hbm, dest_ref[base + 2 * r + 1], yb, r, sem).wait()
        return carry

    lax.fori_loop(0, tm, start, 0, unroll=ROW_DMA_UNROLL)
    lax.fori_loop(0, tm, wait, 0, unroll=ROW_DMA_UNROLL)
    wt = wt_ref[...]
    o_ref[...] = x_ref[...] + (wt[:, 2:3] * ya[...] + wt[:, 3:4] * yb[...])


def _combine(dest, x, wts, y, *, tm, name):
    m, d = x.shape
    return pl.pallas_call(
        _combine_kernel,
        grid_spec=pltpu.PrefetchScalarGridSpec(
            num_scalar_prefetch=1,
            grid=(m // tm,),
            in_specs=[
                pl.BlockSpec((tm, d), lambda i, dest: (i, 0)),
                pl.BlockSpec((tm, LANES), lambda i, dest: (i, 0)),
                pl.BlockSpec(memory_space=pl.ANY),
            ],
            out_specs=pl.BlockSpec((tm, d), lambda i, dest: (i, 0)),
            scratch_shapes=[pltpu.VMEM((tm, d), F32), pltpu.VMEM((tm, d), F32), pltpu.SemaphoreType.DMA(())],
        ),
        out_shape=jax.ShapeDtypeStruct((m, d), F32),
        compiler_params=_cparams(("arbitrary",)),
        name=name,
    )(dest, x, wts, y)


def _moe(x, g, w_r, b_r, w_gate, w_up, w_down, *, li, tm_tok, tm_ffn, gate_before_down, precise, name):
    m = x.shape[0]
    xn, route = _router(x, g, w_r, b_r, tm=tm_tok, precise=precise, name=name + "_router")
    eid = route[:, 0:2].astype(jnp.int32)
    counts = jnp.sum((eid[:, :, None] == jnp.arange(N_EXPERTS, dtype=jnp.int32)).astype(jnp.int32), axis=(0, 1))
    padded = ((counts + tm_ffn - 1) // tm_ffn) * tm_ffn
    ends = jnp.cumsum(padded)
    offs = ends - padded
    rows = 2 * m + N_EXPERTS * tm_ffn
    tile_start = jnp.arange(rows // tm_ffn, dtype=jnp.int32) * tm_ffn
    te = jnp.minimum(jnp.sum((ends[None, :] <= tile_start[:, None]).astype(jnp.int32), axis=1), N_EXPERTS - 1)
    nv = (ends[-1] // tm_ffn).reshape(1)
    last_tile = jnp.where(padded > 0, ends - tm_ffn, -1)
    offs_row = jnp.pad(offs.astype(F32), (0, LANES - N_EXPERTS)).reshape(1, LANES)
    dest = _rank(route, offs_row, tm=tm_tok, name=name + "_rank")[:, 0:2].reshape(-1)
    xs = _scatter(dest, last_tile, xn, rows=rows, tm=tm_tok, tile=tm_ffn, name=name + "_scatter")
    if gate_before_down:
        wrow = jnp.zeros((rows, 1), F32).at[dest, 0].set(route[:, 2:4].reshape(-1))
        wts = jnp.ones_like(route)
    else:
        wrow = jnp.ones((rows, 1), F32)
        wts = route
    y = _ffn(xs, wrow, te, nv, w_gate, w_up, w_down, li=li, tm=tm_ffn, precise=precise, name=name + "_ffn")
    return _combine(dest, x, wts, y, tm=tm_tok, name=name + "_combine")


def _rmsnorm_kernel(x_ref, g_ref, o_ref):
    xf = x_ref[...]
    r = lax.rsqrt(jnp.mean(xf * xf, axis=-1, keepdims=True) + EPS)
    o_ref[...] = (xf * r) * g_ref[...]


def _rmsnorm(x, g, *, tm, name):
    m, d = x.shape
    return pl.pallas_call(
        _rmsnorm_kernel,
        grid=(m // tm,),
        in_specs=[pl.BlockSpec((tm, d), lambda i: (i, 0)), pl.BlockSpec((1, d), lambda i: (0, 0))],
        out_specs=pl.BlockSpec((tm, d), lambda i: (i, 0)),
        out_shape=jax.ShapeDtypeStruct((m, d), F32),
        compiler_params=_cparams(("parallel",)),
        name=name,
    )(x, g)


def _layer_weights(p, li):
    b_if = jnp.pad(p["b_if"][li], (0, LANES - 2 * ML_HEADS)).reshape(1, LANES)
    w_r = jnp.pad(jnp.concatenate([p["w_rg"][li], p["w_re"][li]], axis=1),
                  ((0, 0), (0, LANES - N_GROUPS - N_EXPERTS)))
    b_r = jnp.pad(jnp.concatenate([p["b_rg"][li], p["b_re"][li]]), (0, LANES - N_GROUPS - N_EXPERTS)).reshape(1, LANES)
    return b_if, w_r, b_r


def _run_trunk(x3, p, layer_w, *, caches, c0s, n0s, m0s, pos0):
    bsz, seq, d = x3.shape
    m = bsz * seq
    x = x3.reshape(m, d)
    precise = False
    if caches is None:
        act_dtype, tag = BF16, "p"
        tm, tn, tm_tok, tm_ffn, seq_pad = 1024, 512, 256, 256, seq
    else:
        act_dtype, tag = F32, "s"
        tm, tn, tm_tok, tm_ffn, seq_pad = m, 512, m, 16, 16
    cos_t, sin_t = _rope_tables(pos0 + jnp.arange(seq_pad, dtype=jnp.int32))
    zeros_bias = jnp.zeros((1, N_MAIN), F32)
    new_kv = [[] for _ in ATT_GROUPS]
    c_all, n_all, m_all = [], [], []
    for li in range(DEPTH):
        b_if, w_r, b_r = layer_w[li]
        g_attn = p["attn_norm_g"][li].reshape(1, d)
        common = dict(li=li, tm=tm, precise=precise)
        u = _norm_matmul(x, g_attn, p["w_in"], zeros_bias, row0=0, n_out=N_MAIN, tn=TN_MAIN,
                         out_dtype=act_dtype, act="none", name=f"{tag}{li}_inproj", **common)
        sg = _norm_matmul(x, g_attn, p["w_in"], zeros_bias, row0=O_GA, n_out=2 * D_MODEL, tn=tn,
                          out_dtype=act_dtype, act="sigmoid", name=f"{tag}{li}_gateproj", **common)
        gates = _norm_matmul(x, g_attn, p["w_in"], b_if, row0=O_MI, n_out=LANES, tn=LANES,
                             out_dtype=F32, act="gates", name=f"{tag}{li}_ifproj", **common)
        u3 = u.reshape(bsz, seq, N_MAIN)
        if seq_pad > seq:
            u3 = jnp.pad(u3, ((0, 0), (0, seq_pad - seq), (0, 0)))
        if caches is None:
            att3, kcs, vcs = _attn_prompt(u3, cos_t, sin_t, name=f"{tag}{li}_attn")
            for gi in range(N_ATT_GROUPS):
                keep = kcs[gi].shape[1]
                new_kv[gi].append(jnp.stack([kcs[gi].reshape(bsz, keep, ATT_HEADS, HEAD_DIM),
                                             vcs[gi].reshape(bsz, keep, ATT_HEADS, HEAD_DIM)], axis=2))
            att = att3.reshape(m, ATT_WIDTH)
        else:
            caches2d = [c.reshape(-1, HEAD_DIM) for c in caches]
            att3, kn3 = _attn_sample(u3, caches2d, cos_t, sin_t, li=li, n_new=seq, name=f"{tag}{li}_attn")
            att = att3[:, :seq].reshape(m, ATT_WIDTH)
            nh = N_ATT_GROUPS * ATT_HEADS
            k_new = kn3[:, :seq].reshape(bsz, seq, nh, HEAD_DIM)
            v_new = u3[:, :seq, O_AV:O_AV + nh * HEAD_DIM].reshape(bsz, seq, nh, HEAD_DIM)
            for gi in range(N_ATT_GROUPS):
                sl = slice(gi * ATT_HEADS, (gi + 1) * ATT_HEADS)
                new_kv[gi].append(jnp.stack([k_new[:, :, sl], v_new[:, :, sl]], axis=2))
        gc, gr = _gate_layouts(gates, bsz, seq, seq_pad)
        mlg = p["ml_norm_g"][li].reshape(1, ML_HEADS * ML_V)
        hg3, c_new, n_new, m_new = _mlstm(
            u3, gc, gr, mlg, c0s[li], n0s[li], m0s[li], chunk=min(128, seq_pad), sb=min(512, seq_pad),
            out_dtype=act_dtype, precise=precise, name=f"{tag}{li}_mlstm")
        hg = hg3[:, :seq].reshape(m, ML_HEADS * ML_V)
        c_all.append(c_new)
        n_all.append(n_new)
        m_all.append(m_new)
        merged = _merge(att, hg, sg, p["w_pa"], p["w_pm"], li=li, tm=tm, tn=tn, out_dtype=act_dtype,
                        precise=precise, name=f"{tag}{li}_merge")
        x = _outproj(x, merged, p["w_out"], li=li, tm=tm, tn=tn, precise=precise, name=f"{tag}{li}_outproj")
        x = _moe(x, p["ffn_norm_g"][li].reshape(1, d), w_r, b_r, p["w_gate"], p["w_up"], p["w_down"],
                 li=li, tm_tok=tm_tok, tm_ffn=tm_ffn, gate_before_down=caches is not None, precise=precise,
                 name=f"{tag}{li}_moe")
    y = _rmsnorm(x, p["final_norm_g"].reshape(1, d), tm=min(m, 512), name=f"{tag}_final_norm")
    new_kv = [jnp.stack(a) for a in new_kv]
    if caches is not None:
        shift = [(0, 0, 0)] * 2 + [(-seq, seq, 0)] + [(0, 0, 0)] * 3
        new_kv = [lax.dynamic_update_slice_in_dim(lax.pad(c, jnp.float32(0), shift), fresh, c.shape[2] - seq, axis=2)
                  for c, fresh in zip(caches, new_kv)]
    return (y.reshape(bsz, seq, d), new_kv, jnp.stack(c_all), jnp.stack(n_all), jnp.stack(m_all))


def kernel(x_prompt, x_sample, cache_kv_w128, cache_kv_w512, cache_kv_w2048, state_C, state_n, state_m,
           attn_norm_g, w_in, b_if, ml_norm_g, w_pa, w_pm, w_out, ffn_norm_g, w_rg, b_rg, w_re, b_re,
           w_gate, w_up, w_down, final_norm_g):
    p = dict(attn_norm_g=attn_norm_g, w_in=jnp.swapaxes(w_in, 1, 2), b_if=b_if, ml_norm_g=ml_norm_g, w_pa=w_pa, w_pm=w_pm,
             w_out=w_out, ffn_norm_g=ffn_norm_g, w_rg=w_rg, b_rg=b_rg, w_re=w_re, b_re=b_re,
             w_gate=w_gate, w_up=w_up, w_down=w_down, final_norm_g=final_norm_g)
    layer_w = [_layer_weights(p, li) for li in range(DEPTH)]
    bp = x_prompt.shape[0]
    c0 = jnp.zeros((DEPTH, bp, ML_HEADS, ML_V, ML_QK), F32)
    n0 = jnp.zeros((DEPTH, bp, ML_HEADS, ML_QK), F32)
    m0 = jnp.zeros((DEPTH, bp, ML_HEADS), F32)
    y_p, p_kv, p_c, p_n, p_m = _run_trunk(x_prompt, p, layer_w, caches=None,
                                          c0s=c0, n0s=n0, m0s=m0, pos0=0)
    y_s, s_kv, s_c, s_n, s_m = _run_trunk(x_sample, p, layer_w,
                                          caches=[cache_kv_w128, cache_kv_w512, cache_kv_w2048],
                                          c0s=state_C, n0s=state_n, m0s=state_m, pos0=PAST_LEN)
    return (y_p, y_s, p_kv[0], p_kv[1], p_kv[2], p_c, p_n, p_m,
            s_kv[0], s_kv[1], s_kv[2], s_c, s_n, s_m)
```

```python
import functools

import jax
import jax.numpy as jnp
from jax import lax
from jax.experimental import pallas as pl
from jax.experimental.pallas import tpu as pltpu

F32 = jnp.float32
BF16 = jnp.bfloat16

D_MODEL = 2048
DEPTH = 2
PAST_LEN = 16384
ATT_GROUPS = ((128, 1), (512, 4), (2048, 16))
N_ATT_GROUPS = 3
ATT_HEADS = 4
HEAD_DIM = 128
ATT_WIDTH = ATT_HEADS * HEAD_DIM
ATT_SCALE = HEAD_DIM ** -0.5
ROPE_DIM = HEAD_DIM // 4
ROPE_THETA = 500000.0
ML_HEADS = 8
ML_QK = 128
ML_V = 256
ML_K_SCALE = ML_QK ** -0.5
N_GROUPS = 4
EXP_PER_GROUP = 8
N_EXPERTS = N_GROUPS * EXP_PER_GROUP
EXPERT_FF = 256
EPS = 1e-6

O_AQ, O_AK, O_AV = 0, 1536, 3072
O_MQ, O_MK, O_MV, O_MO = 4608, 5632, 6656, 8704
O_MI, O_MF, O_GA, O_GB = 10752, 10760, 10768, 12816
N_MAIN = O_MI
LANES = 128
SUBLANES = 8
TN_MAIN = 768
VMEM_LIMIT_MB = 56


def _cparams(sem):
    return pltpu.CompilerParams(dimension_semantics=sem, vmem_limit_bytes=VMEM_LIMIT_MB * 1024 * 1024)


def _dot(a, b, dims, precise):
    dn = (dims, ((), ()))
    if precise:
        return lax.dot_general(a.astype(F32), b.astype(F32), dn, preferred_element_type=F32,
                               precision=lax.Precision.HIGHEST)
    return lax.dot_general(a.astype(BF16), b.astype(BF16), dn, preferred_element_type=F32)


def _mm(a, b, precise=False):
    return _dot(a, b, ((1,), (0,)), precise)


def _mm_nt(a, b, precise=False):
    return _dot(a, b, ((1,), (1,)), precise)


def _mm_tn(a, b, precise=False):
    return _dot(a, b, ((0,), (0,)), precise)


def _rb(x, precise=False):
    return x.astype(F32) if precise else x.astype(BF16).astype(F32)


def _sigmoid(z):
    return 1.0 / (1.0 + jnp.exp(-z))


def _log_sigmoid(z):
    return jnp.minimum(z, 0.0) - jnp.log1p(jnp.exp(-jnp.abs(z)))


def _norm_matmul_kernel(x_ref, g_ref, w_ref, b_ref, o_ref, xn_ref, *, act, precise):
    @pl.when(pl.program_id(1) == 0)
    def _():
        xf = x_ref[...]
        r = lax.rsqrt(jnp.mean(xf * xf, axis=-1, keepdims=True) + EPS)
        xn_ref[...] = ((xf * r) * g_ref[...]).astype(xn_ref.dtype)

    acc = _mm_nt(xn_ref[...], w_ref[0], precise)
    if act == "sigmoid":
        acc = _sigmoid(acc)
    elif act == "gates":
        z = acc + b_ref[...]
        lane = lax.broadcasted_iota(jnp.int32, z.shape, 1)
        acc = jnp.where(lane < ML_HEADS, z, _log_sigmoid(z))
    o_ref[...] = acc.astype(o_ref.dtype)


def _norm_matmul(x, g, wt, bias, *, li, row0, n_out, tm, tn, out_dtype, act, precise, name):
    m, d = x.shape
    assert row0 % SUBLANES == 0 and tn % SUBLANES == 0
    w_spec = pl.BlockSpec((pl.Element(1), pl.Element(tn), pl.Element(d)),
                          lambda i, j: (li, pl.multiple_of(row0 + j * tn, SUBLANES), 0))
    return pl.pallas_call(
        functools.partial(_norm_matmul_kernel, act=act, precise=precise),
        grid=(m // tm, n_out // tn),
        in_specs=[
            pl.BlockSpec((tm, d), lambda i, j: (i, 0)),
            pl.BlockSpec((1, d), lambda i, j: (0, 0)),
            w_spec,
            pl.BlockSpec((1, tn), lambda i, j: (0, j)),
        ],
        out_specs=pl.BlockSpec((tm, tn), lambda i, j: (i, j)),
        out_shape=jax.ShapeDtypeStruct((m, n_out), out_dtype),
        scratch_shapes=[pltpu.VMEM((tm, d), F32 if precise else BF16)],
        compiler_params=_cparams(("parallel", "arbitrary")),
        name=name,
    )(x, g, wt, bias)


def _merge_kernel(att_ref, hg_ref, sa_ref, sb_ref, wpa_ref, wpm_ref, o_ref, *, precise):
    a = _mm(att_ref[...], wpa_ref[...], precise)
    m = _mm(hg_ref[...], wpm_ref[...], precise)
    o_ref[...] = (sa_ref[...].astype(F32) * a + sb_ref[...].astype(F32) * m).astype(o_ref.dtype)


def _merge(att, hg, sg, w_pa, w_pm, *, li, tm, tn, out_dtype, precise, name):
    m = att.shape[0]
    nb = D_MODEL // tn
    return pl.pallas_call(
        functools.partial(_merge_kernel, precise=precise),
        grid=(m // tm, nb),
        in_specs=[
            pl.BlockSpec((tm, ATT_WIDTH), lambda i, j: (i, 0)),
            pl.BlockSpec((tm, ML_HEADS * ML_V), lambda i, j: (i, 0)),
            pl.BlockSpec((tm, tn), lambda i, j: (i, j)),
            pl.BlockSpec((tm, tn), lambda i, j: (i, j + nb)),
            pl.BlockSpec((None, ATT_WIDTH, tn), lambda i, j: (li, 0, j)),
            pl.BlockSpec((None, ML_HEADS * ML_V, tn), lambda i, j: (li, 0, j)),
        ],
        out_specs=pl.BlockSpec((tm, tn), lambda i, j: (i, j)),
        out_shape=jax.ShapeDtypeStruct((m, D_MODEL), out_dtype),
        compiler_params=_cparams(("parallel", "arbitrary")),
        name=name,
    )(att, hg, sg, sg, w_pa, w_pm)


def _outproj_kernel(x_ref, mg_ref, w_ref, o_ref, *, precise):
    o_ref[...] = x_ref[...] + _mm(mg_ref[...], w_ref[...], precise)


def _outproj(x, mg, w_out, *, li, tm, tn, precise, name):
    m = x.shape[0]
    return pl.pallas_call(
        functools.partial(_outproj_kernel, precise=precise),
        grid=(m // tm, D_MODEL // tn),
        in_specs=[
            pl.BlockSpec((tm, tn), lambda i, j: (i, j)),
            pl.BlockSpec((tm, D_MODEL), lambda i, j: (i, 0)),
            pl.BlockSpec((None, D_MODEL, tn), lambda i, j: (li, 0, j)),
        ],
        out_specs=pl.BlockSpec((tm, tn), lambda i, j: (i, j)),
        out_shape=jax.ShapeDtypeStruct((m, D_MODEL), F32),
        compiler_params=_cparams(("parallel", "arbitrary")),
        name=name,
    )(x, mg, w_out)


def _rope_tables(pos):
    half = ROPE_DIM // 2
    inv = jnp.power(jnp.float32(ROPE_THETA), -jnp.arange(half, dtype=F32) / half)
    ang = pos.astype(F32)[:, None] * inv[None, :]
    n = pos.shape[0]
    cos_t = jnp.concatenate([jnp.cos(ang), jnp.cos(ang), jnp.ones((n, HEAD_DIM - ROPE_DIM), F32)], axis=-1)
    sin_t = jnp.concatenate([jnp.sin(ang), jnp.sin(ang), jnp.zeros((n, HEAD_DIM - ROPE_DIM), F32)], axis=-1)
    return cos_t, sin_t


def _rope_perm():
    half = ROPE_DIM // 2
    r = jnp.arange(HEAD_DIM)[:, None]
    c = jnp.arange(HEAD_DIM)[None, :]
    p = jnp.where((c < half) & (r == c + half), -1.0, 0.0) + jnp.where(
        (c >= half) & (c < ROPE_DIM) & (r == c - half), 1.0, 0.0)
    return p.astype(BF16)


ROPE_ROWS = 512
ATT_BLOCKS_PER_TRIP = 8


def _attn_prompt_kernel(cos_ref, sin_ref, perm_ref, *refs, seq):
    q_refs, k_refs, v_refs = refs[0:3], refs[3:6], refs[6:9]
    att_ref = refs[9]
    kc_refs, vc_refs = refs[10:13], refs[13:16]
    qs, ks, vs = refs[16:19]
    o_s, lse_s = refs[19:22], refs[22:25]
    perm = perm_ref[...]
    nk = ATT_GROUPS[0][0] // ATT_GROUPS[0][1]
    row = lax.broadcasted_iota(jnp.int32, (nk, nk), 0)
    col = lax.broadcasted_iota(jnp.int32, (nk, nk), 1)
    cur_ok = col <= row
    prev_ok = col >= row
    neg = jnp.float32(-jnp.inf)

    for g, (win, dil) in enumerate(ATT_GROUPS):
        assert win // dil == nk
        span = nk * dil
        shift = dil.bit_length() - 1

        def rope_body(c, carry, g=g):
            r0 = pl.multiple_of(c * ROPE_ROWS, ROPE_ROWS)
            cs = cos_ref[pl.ds(r0, ROPE_ROWS), :]
            sn = sin_ref[pl.ds(r0, ROPE_ROWS), :]
            qb = q_refs[g][0, pl.ds(r0, ROPE_ROWS), :]
            kb = k_refs[g][0, pl.ds(r0, ROPE_ROWS), :]
            qr = qb.astype(F32) * cs + jnp.dot(qb, perm, preferred_element_type=F32) * sn
            kr = kb.astype(F32) * cs + jnp.dot(kb, perm, preferred_element_type=F32) * sn
            qs[pl.ds(r0, ROPE_ROWS), :] = qr
            ks[pl.ds(r0, ROPE_ROWS), :] = kr
            vs[pl.ds(r0, ROPE_ROWS), :] = v_refs[g][0, pl.ds(r0, ROPE_ROWS), :].astype(F32)
            return carry

        lax.fori_loop(0, seq // ROPE_ROWS, rope_body, 0)
        keep = min(win, seq)
        kc_refs[g][0] = ks[seq - keep:seq, :]
        vc_refs[g][0] = vs[seq - keep:seq, :]

        def rows(start, dil=dil):
            if dil == 1:
                return pl.ds(pl.multiple_of(start, nk), nk)
            return pl.ds(start, nk, stride=dil)

        def blk_body(it, carry, g=g, dil=dil, span=span, shift=shift, rows=rows):
            blks = [it * ATT_BLOCKS_PER_TRIP + u for u in range(ATT_BLOCKS_PER_TRIP)]
            n = [blk >> shift for blk in blks]
            start = [n_u * span + (blk & (dil - 1)) for n_u, blk in zip(n, blks)]
            has_prev = [n_u > 0 for n_u in n]
            start_p = [jnp.where(hp, st - span, st) for hp, st in zip(has_prev, start)]
            q = [qs[rows(st), :] for st in start]
            s_c = [jnp.where(cur_ok, _mm_nt(q_u, ks[rows(st), :]) * ATT_SCALE, neg) for q_u, st in zip(q, start)]
            no_prev = [jnp.where(hp, jnp.float32(0.0), neg) for hp in has_prev]
            s_p = [jnp.where(prev_ok, _mm_nt(q_u, ks[rows(sp), :]) * ATT_SCALE + np_u, neg)
                   for q_u, sp, np_u in zip(q, start_p, no_prev)]
            m = [jnp.max(jnp.maximum(c_u, p_u), axis=-1, keepdims=True) for c_u, p_u in zip(s_c, s_p)]
            p_c = [jnp.exp(c_u - m_u) for c_u, m_u in zip(s_c, m)]
            p_p = [jnp.exp(p_u - m_u) for p_u, m_u in zip(s_p, m)]
            l = [jnp.sum(c_u + p_u, axis=-1, keepdims=True) for c_u, p_u in zip(p_c, p_p)]
            acc = [_mm(c_u, vs[rows(st), :]) + _mm(p_u, vs[rows(sp), :])
                   for c_u, p_u, st, sp in zip(p_c, p_p, start, start_p)]
            for u, st in enumerate(start):
                o_s[g][rows(st), :] = acc[u] * (1.0 / l[u])
                lse_s[g][rows(st), :] = jnp.broadcast_to(m[u] + jnp.log(l[u]), (nk, HEAD_DIM))
            return carry

        lax.fori_loop(0, seq // nk // ATT_BLOCKS_PER_TRIP, blk_body, 0)

    def out_body(c, carry):
        sl = pl.ds(pl.multiple_of(c * nk, nk), nk)
        lse = [lse_s[g][sl, :] for g in range(N_ATT_GROUPS)]
        top = functools.reduce(jnp.maximum, lse)
        w = [jnp.exp(x - top) for x in lse]
        num = sum(w[g] * o_s[g][sl, :] for g in range(N_ATT_GROUPS))
        att_ref[0, sl, :] = (num / sum(w)).astype(att_ref.dtype)
        return carry

    lax.fori_loop(0, seq // nk, out_body, 0)


def _attn_prompt(u3, cos_t, sin_t, *, name):
    bsz, seq, _ = u3.shape
    perm = _rope_perm()
    in_specs = [
        pl.BlockSpec((seq, HEAD_DIM), lambda b, h: (0, 0)),
        pl.BlockSpec((seq, HEAD_DIM), lambda b, h: (0, 0)),
        pl.BlockSpec((HEAD_DIM, HEAD_DIM), lambda b, h: (0, 0)),
    ]
    for off in (O_AQ, O_AK, O_AV):
        for g in range(N_ATT_GROUPS):
            blk0 = off // HEAD_DIM + g * ATT_HEADS
            in_specs.append(pl.BlockSpec((1, seq, HEAD_DIM), lambda b, h, blk0=blk0: (b, 0, blk0 + h)))
    keeps = [min(w, seq) for w, _ in ATT_GROUPS]
    out_specs = [pl.BlockSpec((1, seq, HEAD_DIM), lambda b, h: (b, 0, h))]
    out_shape = [jax.ShapeDtypeStruct((bsz, seq, ATT_WIDTH), BF16)]
    for _ in range(2):
        for keep in keeps:
            out_specs.append(pl.BlockSpec((1, keep, HEAD_DIM), lambda b, h: (b, 0, h)))
            out_shape.append(jax.ShapeDtypeStruct((bsz, keep, ATT_WIDTH), F32))
    outs = pl.pallas_call(
        functools.partial(_attn_prompt_kernel, seq=seq),
        grid=(bsz, ATT_HEADS),
        in_specs=in_specs,
        out_specs=out_specs,
        out_shape=out_shape,
        scratch_shapes=[pltpu.VMEM((seq, HEAD_DIM), F32) for _ in range(3 + 2 * N_ATT_GROUPS)],
        compiler_params=_cparams(("parallel", "arbitrary")),
        name=name,
    )(cos_t, sin_t, perm, *([u3] * 9))
    return outs[0], outs[1:4], outs[4:7]


def _attn_sample_kernel(cos_ref, sin_ref, q_ref, k_ref, v_ref, c0_ref, c1_ref, c2_ref,
                        att_ref, kn_ref, *, n_new):
    cache_refs = (c0_ref, c1_ref, c2_ref)
    cs = cos_ref[...]
    sn = sin_ref[...]
    rows_pad = q_ref.shape[1]
    half = ROPE_DIM // 2
    lane = lax.broadcasted_iota(jnp.int32, (rows_pad, HEAD_DIM), 1)

    def rot_half(x):
        return jnp.where(lane < half, -pltpu.roll(x, HEAD_DIM - half, axis=1),
                         jnp.where(lane < ROPE_DIM, pltpu.roll(x, half, axis=1), 0.0))

    nk = ATT_GROUPS[0][0] // ATT_GROUPS[0][1]
    kv_rows = 2 * ATT_HEADS
    w_iota = lax.broadcasted_iota(jnp.int32, (nk, 1), 0)
    u_iota = lax.broadcasted_iota(jnp.int32, (rows_pad, 1), 0)
    neg = jnp.float32(-jnp.inf)
    att_ref[...] = jnp.zeros(att_ref.shape, att_ref.dtype)

    for h in range(ATT_HEADS):
        parts = [[None] * N_ATT_GROUPS for _ in range(n_new)]
        for g, (win, dil) in enumerate(ATT_GROUPS):
            hs = (g * ATT_HEADS + h) * HEAD_DIM
            qh = q_ref[0, :, hs:hs + HEAD_DIM]
            kh = k_ref[0, :, hs:hs + HEAD_DIM]
            vh = v_ref[0, :, hs:hs + HEAD_DIM]
            qr = _rb(qh * cs + rot_half(qh) * sn)
            kr = kh * cs + rot_half(kh) * sn
            kn_ref[0, :, hs:hs + HEAD_DIM] = kr
            kr = _rb(kr)
            vh = _rb(vh)
            cref = cache_refs[g]
            for t in range(n_new):
                q_t = qr[t:t + 1, :]
                if dil == 1:
                    p0 = 0
                else:
                    p0 = t
                k_c = _rb(cref[pl.ds(p0 * kv_rows + h, nk, stride=kv_rows * dil), :])
                v_c = _rb(cref[pl.ds(p0 * kv_rows + ATT_HEADS + h, nk, stride=kv_rows * dil), :])
                s_c = jnp.sum(k_c * q_t, axis=-1, keepdims=True) * ATT_SCALE
                s_n = jnp.sum(kr * q_t, axis=-1, keepdims=True) * ATT_SCALE
                if dil == 1:
                    s_c = jnp.where(w_iota >= t, s_c, neg)
                    s_n = jnp.where(u_iota <= t, s_n, neg)
                else:
                    s_n = jnp.where(u_iota == t, s_n, neg)
                m = jnp.maximum(jnp.max(s_c, axis=0, keepdims=True), jnp.max(s_n, axis=0, keepdims=True))
                p_c = jnp.exp(s_c - m)
                p_n = jnp.exp(s_n - m)
                l = jnp.sum(p_c, axis=0, keepdims=True) + jnp.sum(p_n, axis=0, keepdims=True)
                acc = jnp.sum(_rb(p_c) * v_c, axis=0, keepdims=True) + jnp.sum(_rb(p_n) * vh, axis=0, keepdims=True)
                parts[t][g] = (m, l, acc)
        for t in range(n_new):
            m_all = functools.reduce(jnp.maximum, [p[0] for p in parts[t]])
            num = sum(jnp.exp(p[0] - m_all) * p[2] for p in parts[t])
            den = sum(jnp.exp(p[0] - m_all) * p[1] for p in parts[t])
            att_ref[0, t:t + 1, h * HEAD_DIM:(h + 1) * HEAD_DIM] = num / den


def _attn_sample(u3, caches2d, cos_t, sin_t, *, li, n_new, name):
    bsz, rows_pad, _ = u3.shape
    qkv_w = N_ATT_GROUPS * ATT_WIDTH
    in_specs = [
        pl.BlockSpec((rows_pad, HEAD_DIM), lambda b: (0, 0)),
        pl.BlockSpec((rows_pad, HEAD_DIM), lambda b: (0, 0)),
        pl.BlockSpec((1, rows_pad, qkv_w), lambda b: (b, 0, O_AQ // qkv_w)),
        pl.BlockSpec((1, rows_pad, qkv_w), lambda b: (b, 0, O_AK // qkv_w)),
        pl.BlockSpec((1, rows_pad, qkv_w), lambda b: (b, 0, O_AV // qkv_w)),
    ]
    for (win, _), c in zip(ATT_GROUPS, caches2d):
        rows = c.shape[0] // (DEPTH * bsz)
        in_specs.append(pl.BlockSpec((rows, HEAD_DIM), lambda b, bsz=bsz: (li * bsz + b, 0)))
    return pl.pallas_call(
        functools.partial(_attn_sample_kernel, n_new=n_new),
        grid=(bsz,),
        in_specs=in_specs,
        out_specs=[
            pl.BlockSpec((1, rows_pad, ATT_WIDTH), lambda b: (b, 0, 0)),
            pl.BlockSpec((1, rows_pad, qkv_w), lambda b: (b, 0, 0)),
        ],
        out_shape=[
            jax.ShapeDtypeStruct((bsz, rows_pad, ATT_WIDTH), F32),
            jax.ShapeDtypeStruct((bsz, rows_pad, qkv_w), F32),
        ],
        compiler_params=_cparams(("parallel",)),
        name=name,
    )(cos_t, sin_t, u3, u3, u3, *caches2d)


ML_HPB = 4
ML_VPB = 2


def _mlstm_kernel(q_ref, k_ref, *refs, chunk, precise):
    nvb = ML_HPB // ML_VPB
    v_refs, mo_refs = refs[:nvb], refs[nvb:2 * nvb]
    gc_ref, gr_ref, mlg_ref, c0_ref, n0_ref, m0_ref, h_ref, c_out, n_out, m_out, ct_s, n_s, m_s = refs[2 * nvb:]
    s_idx = pl.program_id(2)
    n_s_blocks = pl.num_programs(2)
    sb = q_ref.shape[1]
    L = chunk

    @pl.when(s_idx == 0)
    def _():
        for hh in range(ML_HPB):
            ct_s[hh] = c0_ref[0, hh].T
            n_s[hh] = n0_ref[0, 0, hh:hh + 1, :]
            m_s[hh] = m0_ref[0, 0, hh:hh + 1, :]

    row = lax.broadcasted_iota(jnp.int32, (L, L), 0)
    col = lax.broadcasted_iota(jnp.int32, (L, L), 1)
    causal = row >= col
    neg = jnp.float32(-jnp.inf)

    def chunk_body(c, carry):
        r0 = pl.multiple_of(c * L, L)
        gc = gc_ref[0, 0, pl.ds(r0, L), :]
        gr = gr_ref[0, 0, :, pl.ds(r0, L)]
        heads = range(ML_HPB)
        ig_col = [gc[:, hh:hh + 1] for hh in heads]
        lf_col = [gc[:, ML_HPB + hh:ML_HPB + hh + 1] for hh in heads]
        ig_row = [gr[hh:hh + 1, :] for hh in heads]
        lf_row = [gr[ML_HPB + hh:ML_HPB + hh + 1, :] for hh in heads]
        vcols = [slice((hh % ML_VPB) * ML_V, (hh % ML_VPB + 1) * ML_V) for hh in heads]
        qf = [q_ref[0, pl.ds(r0, L), hh * ML_QK:(hh + 1) * ML_QK] for hh in heads]
        ks = [k_ref[0, pl.ds(r0, L), hh * ML_QK:(hh + 1) * ML_QK].astype(F32) * ML_K_SCALE for hh in heads]
        vf = [v_refs[hh // ML_VPB][0, pl.ds(r0, L), vcols[hh]] for hh in heads]
        ct = [ct_s[hh] for hh in heads]
        n_row = [n_s[hh] for hh in heads]
        m_prev = [m_s[hh][:, 0:1] for hh in heads]
        qk = [_mm_nt(qf[hh], ks[hh], precise) for hh in heads]
        q_ct = [_mm(qf[hh], ct[hh], precise) for hh in heads]
        b_col = [jnp.sum(jnp.where(causal, lf_row[hh], 0.0), axis=1, keepdims=True) for hh in heads]
        b_row = [jnp.sum(jnp.where(row <= col, lf_col[hh], 0.0), axis=0, keepdims=True) for hh in heads]
        inter = [b_col[hh] + m_prev[hh] for hh in heads]
        dm = [jnp.where(causal, b_col[hh] - b_row[hh] + ig_row[hh], neg) for hh in heads]
        m_t = [jnp.maximum(inter[hh], jnp.max(dm[hh], axis=1, keepdims=True)) for hh in heads]
        w_inter = [jnp.exp(inter[hh] - m_t[hh]) for hh in heads]
        sw = [qk[hh] * jnp.exp(dm[hh] - m_t[hh]) for hh in heads]
        num = [w_inter[hh] * q_ct[hh] + _mm(sw[hh], vf[hh], precise) for hh in heads]
        den = [w_inter[hh] * jnp.sum(_rb(qf[hh], precise) * _rb(n_row[hh], precise), axis=1, keepdims=True)
               + jnp.sum(sw[hh], axis=1, keepdims=True) for hh in heads]
        hv = [num[hh] / jnp.maximum(jnp.abs(den[hh]), jnp.exp(-m_t[hh])) for hh in heads]
        hn = [hv[hh] * lax.rsqrt(jnp.mean(hv[hh] * hv[hh], axis=-1, keepdims=True) + EPS) for hh in heads]
        for hh in heads:
            mo = mo_refs[hh // ML_VPB][0, pl.ds(r0, L), vcols[hh]].astype(F32)
            gain = mlg_ref[:, hh * ML_V:(hh + 1) * ML_V]
            h_ref[0, pl.ds(r0, L), hh * ML_V:(hh + 1) * ML_V] = (hn[hh] * gain * _sigmoid(mo)).astype(h_ref.dtype)
        b_last = [b_col[hh][L - 1:L, :] for hh in heads]
        m_new = [m_t[hh][L - 1:L, :] for hh in heads]
        decay = [jnp.exp(b_last[hh] + m_prev[hh] - m_new[hh]) for hh in heads]
        wk_col = [jnp.exp(b_last[hh] - b_col[hh] + ig_col[hh] - m_new[hh]) for hh in heads]
        upd = [_mm_tn(ks[hh] * wk_col[hh], vf[hh], precise) for hh in heads]
        for hh in heads:
            ct_s[hh] = decay[hh] * ct[hh] + upd[hh]
            n_s[hh] = decay[hh] * n_row[hh] + jnp.sum(_rb(ks[hh], precise) * _rb(wk_col[hh], precise), axis=0,
                                                      keepdims=True)
            m_s[hh] = jnp.broadcast_to(m_new[hh], (1, LANES))
        return carry

    lax.fori_loop(0, sb // L, chunk_body, 0)

    @pl.when(s_idx == n_s_blocks - 1)
    def _():
        for hh in range(ML_HPB):
            c_out[0, hh] = ct_s[hh].T
            n_out[0, 0, hh:hh + 1, :] = n_s[hh]
            m_out[0, 0, hh:hh + 1, :] = m_s[hh]


def _mlstm(u3, gc, gr, mlg, c0, n0, m0, *, chunk, sb, out_dtype, precise, name):
    bsz, seq, _ = u3.shape
    hg_n = ML_HEADS // ML_HPB
    qk_w, v_w, vb_w = ML_HPB * ML_QK, ML_HPB * ML_V, ML_VPB * ML_V
    nvb = ML_HPB // ML_VPB
    assert O_MQ % qk_w == 0 and O_MK % qk_w == 0 and O_MV % vb_w == 0 and O_MO % vb_w == 0
    v_specs = [pl.BlockSpec((1, sb, vb_w), lambda b, hg, s, j=j, o=off // vb_w: (b, s, o + hg * nvb + j))
               for off in (O_MV, O_MO) for j in range(nvb)]
    n0r = n0.reshape(bsz, hg_n, ML_HPB, ML_QK)
    m0r = jnp.broadcast_to(m0.reshape(bsz, hg_n, ML_HPB, 1), (bsz, hg_n, ML_HPB, LANES))
    state_spec = pl.BlockSpec((1, 1, ML_HPB, LANES), lambda b, hg, s: (b, hg, 0, 0))
    c_spec = pl.BlockSpec((1, ML_HPB, ML_V, ML_QK), lambda b, hg, s: (b, hg, 0, 0))
    h, c_new, n_new, m_new = pl.pallas_call(
        functools.partial(_mlstm_kernel, chunk=chunk, precise=precise),
        grid=(bsz, hg_n, seq // sb),
        in_specs=[
            pl.BlockSpec((1, sb, qk_w), lambda b, hg, s: (b, s, O_MQ // qk_w + hg)),
            pl.BlockSpec((1, sb, qk_w), lambda b, hg, s: (b, s, O_MK // qk_w + hg)),
            *v_specs,
            pl.BlockSpec((1, 1, sb, 2 * ML_HPB), lambda b, hg, s: (b, hg, s, 0)),
            pl.BlockSpec((1, 1, 2 * ML_HPB, sb), lambda b, hg, s: (b, hg, 0, s)),
            pl.BlockSpec((1, v_w), lambda b, hg, s: (0, hg)),
            c_spec, state_spec, state_spec,
        ],
        out_specs=[
            pl.BlockSpec((1, sb, v_w), lambda b, hg, s: (b, s, hg)),
            c_spec, state_spec, state_spec,
        ],
        out_shape=[
            jax.ShapeDtypeStruct((bsz, seq, ML_HEADS * ML_V), out_dtype),
            jax.ShapeDtypeStruct((bsz, ML_HEADS, ML_V, ML_QK), F32),
            jax.ShapeDtypeStruct((bsz, hg_n, ML_HPB, LANES), F32),
            jax.ShapeDtypeStruct((bsz, hg_n, ML_HPB, LANES), F32),
        ],
        scratch_shapes=[
            pltpu.VMEM((ML_HPB, ML_QK, ML_V), F32),
            pltpu.VMEM((ML_HPB, 1, ML_QK), F32),
            pltpu.VMEM((ML_HPB, 1, LANES), F32),
        ],
        compiler_params=_cparams(("parallel", "parallel", "arbitrary")),
        name=name,
    )(u3, u3, *([u3] * (2 * nvb)), gc, gr, mlg, c0, n0r, m0r)
    return h, c_new, n_new.reshape(bsz, ML_HEADS, ML_QK), m_new[..., 0].reshape(bsz, ML_HEADS)


def _gate_layouts(gates, bsz, seq, seq_pad):
    hg_n = ML_HEADS // ML_HPB
    ig = gates[:, :ML_HEADS].reshape(bsz, seq, ML_HEADS)
    lf = gates[:, ML_HEADS:2 * ML_HEADS].reshape(bsz, seq, ML_HEADS)
    if seq_pad > seq:
        ig = jnp.pad(ig, ((0, 0), (0, seq_pad - seq), (0, 0)), constant_values=-jnp.inf)
        lf = jnp.pad(lf, ((0, 0), (0, seq_pad - seq), (0, 0)))
    ig = ig.reshape(bsz, seq_pad, hg_n, ML_HPB).transpose(0, 2, 1, 3)
    lf = lf.reshape(bsz, seq_pad, hg_n, ML_HPB).transpose(0, 2, 1, 3)
    gc = jnp.concatenate([ig, lf], axis=-1)
    return gc, gc.transpose(0, 1, 3, 2)


def _router_kernel(x_ref, g_ref, w_ref, b_ref, xn_ref, route_ref, *, precise):
    xf = x_ref[...]
    r = lax.rsqrt(jnp.mean(xf * xf, axis=-1, keepdims=True) + EPS)
    xn = (xf * r) * g_ref[...]
    xn_ref[...] = xn.astype(xn_ref.dtype)
    logits = _mm(xn, w_ref[...], precise) + b_ref[...]
    lane = lax.broadcasted_iota(jnp.int32, logits.shape, 1).astype(F32)
    neg = jnp.float32(-jnp.inf)
    big = jnp.float32(LANES)
    gl = jnp.where(lane < N_GROUPS, logits, neg)
    g_max = jnp.max(gl, axis=-1, keepdims=True)
    g_val = 1.0 / jnp.sum(jnp.exp(gl - g_max), axis=-1, keepdims=True)
    g_idx = jnp.min(jnp.where(gl == g_max, lane, big), axis=-1, keepdims=True)
    lo = N_GROUPS + EXP_PER_GROUP * g_idx
    es = jnp.where((lane >= lo) & (lane < lo + EXP_PER_GROUP), logits, neg)
    t0 = jnp.max(es, axis=-1, keepdims=True)
    i0 = jnp.min(jnp.where(es == t0, lane, big), axis=-1, keepdims=True)
    es1 = jnp.where(lane == i0, neg, es)
    t1 = jnp.max(es1, axis=-1, keepdims=True)
    i1 = jnp.min(jnp.where(es1 == t1, lane, big), axis=-1, keepdims=True)
    e1 = jnp.exp(t1 - t0)
    w0 = g_val / (1.0 + e1)
    w1 = g_val * e1 / (1.0 + e1)
    route = jnp.where(lane == 0, i0 - N_GROUPS,
                      jnp.where(lane == 1, i1 - N_GROUPS,
                                jnp.where(lane == 2, w0, jnp.where(lane == 3, w1, 0.0))))
    route_ref[...] = route


def _router(x, g, w_r, b_r, *, tm, precise, name):
    m, d = x.shape
    return pl.pallas_call(
        functools.partial(_router_kernel, precise=precise),
        grid=(m // tm,),
        in_specs=[
            pl.BlockSpec((tm, d), lambda i: (i, 0)),
            pl.BlockSpec((1, d), lambda i: (0, 0)),
            pl.BlockSpec((d, LANES), lambda i: (0, 0)),
            pl.BlockSpec((1, LANES), lambda i: (0, 0)),
        ],
        out_specs=[pl.BlockSpec((tm, d), lambda i: (i, 0)), pl.BlockSpec((tm, LANES), lambda i: (i, 0))],
        out_shape=[jax.ShapeDtypeStruct((m, d), F32), jax.ShapeDtypeStruct((m, LANES), F32)],
        compiler_params=_cparams(("parallel",)),
        name=name,
    )(x, g, w_r, b_r)


def _ffn_kernel(te_ref, nv_ref, xs_ref, wr_ref, wg_ref, wu_ref, wd_ref, y_ref, wg_s, wu_s, wd_s, *, precise):
    i = pl.program_id(0)
    valid = i < nv_ref[0]
    prev = te_ref[jnp.maximum(i - 1, 0)]

    @pl.when(valid & ((i == 0) | (te_ref[i] != prev)))
    def _():
        wg_s[...] = wg_ref[...].astype(wg_s.dtype)
        wu_s[...] = wu_ref[...].astype(wu_s.dtype)
        wd_s[...] = wd_ref[...].astype(wd_s.dtype)

    @pl.when(valid)
    def _():
        xs = xs_ref[...]
        gt = _mm(xs, wg_s[...], precise)
        up = _mm(xs, wu_s[...], precise)
        y_ref[...] = _mm((gt * _sigmoid(gt) * up) * wr_ref[...], wd_s[...], precise)


def _ffn(xs, wrow, te, nv, w_gate, w_up, w_down, *, li, tm, precise, name):
    rows, d = xs.shape
    n_tiles = rows // tm
    f = EXPERT_FF
    wdt = F32 if precise else BF16

    def row_blk(i, te, nv):
        return (jnp.minimum(i, nv[0] - 1), 0)

    return pl.pallas_call(
        functools.partial(_ffn_kernel, precise=precise),
        grid_spec=pltpu.PrefetchScalarGridSpec(
            num_scalar_prefetch=2,
            grid=(n_tiles,),
            in_specs=[
                pl.BlockSpec((tm, d), row_blk),
                pl.BlockSpec((tm, 1), row_blk),
                pl.BlockSpec((None, d, f), lambda i, te, nv: (li, 0, te[i])),
                pl.BlockSpec((None, d, f), lambda i, te, nv: (li, 0, te[i])),
                pl.BlockSpec((None, f, d), lambda i, te, nv: (li, te[i], 0)),
            ],
            out_specs=pl.BlockSpec((tm, d), row_blk),
            scratch_shapes=[pltpu.VMEM((d, f), wdt), pltpu.VMEM((d, f), wdt), pltpu.VMEM((f, d), wdt)],
        ),
        out_shape=jax.ShapeDtypeStruct((rows, d), F32),
        compiler_params=_cparams(("arbitrary",)),
        name=name,
    )(te, nv, xs, wrow, w_gate, w_up, w_down)


def _rank_kernel(route_ref, offs_ref, dest_ref, run_s):
    @pl.when(pl.program_id(0) == 0)
    def _():
        run_s[...] = jnp.zeros(run_s.shape, run_s.dtype)

    route = route_ref[...]
    tm = route.shape[0]
    lane = lax.broadcasted_iota(jnp.int32, route.shape, 1).astype(F32)
    is0 = lane == route[:, 0:1]
    is1 = lane == route[:, 1:2]
    onehot = jnp.where(is0, 1.0, 0.0) + jnp.where(is1, 1.0, 0.0)
    r = lax.broadcasted_iota(jnp.int32, (tm, tm), 0)
    c = lax.broadcasted_iota(jnp.int32, (tm, tm), 1)
    earlier = jnp.where(c < r, 1.0, 0.0)
    before = _mm(earlier, onehot)
    base = offs_ref[...] + run_s[...] + before
    d0 = jnp.sum(jnp.where(is0, base, 0.0), axis=-1, keepdims=True)
    d1 = jnp.sum(jnp.where(is1, base, 0.0), axis=-1, keepdims=True)
    run_s[...] = run_s[...] + jnp.sum(onehot, axis=0, keepdims=True)
    dest_ref[...] = jnp.where(lane == 0.0, d0, jnp.where(lane == 1.0, d1, 0.0)).astype(jnp.int32)


def _rank(route, offs_row, *, tm, name):
    m = route.shape[0]
    return pl.pallas_call(
        _rank_kernel,
        grid=(m // tm,),
        in_specs=[pl.BlockSpec((tm, LANES), lambda i: (i, 0)), pl.BlockSpec((1, LANES), lambda i: (0, 0))],
        out_specs=pl.BlockSpec((tm, LANES), lambda i: (i, 0)),
        out_shape=jax.ShapeDtypeStruct((m, LANES), jnp.int32),
        scratch_shapes=[pltpu.VMEM((1, LANES), F32)],
        compiler_params=_cparams(("arbitrary",)),
        name=name,
    )(route, offs_row)


ROW_DMA_UNROLL = 8


def _row_copy(src, src_row, dst, dst_row, sem):
    return pltpu.make_async_copy(src.at[pl.ds(src_row, 1)], dst.at[pl.ds(dst_row, 1)], sem)


def _scatter_kernel(dest_ref, last_ref, xn_ref, xs_hbm, zero_s, sem, zsem, *, tile):
    i = pl.program_id(0)
    tm = xn_ref.shape[0]

    def zero_copy(e):
        return pltpu.make_async_copy(zero_s, xs_hbm.at[pl.ds(pl.multiple_of(last_ref[e], tile), tile)], zsem)

    @pl.when(i == 0)
    def _():
        zero_s[...] = jnp.zeros(zero_s.shape, zero_s.dtype)

        def z_start(e, carry):
            @pl.when(last_ref[e] >= 0)
            def _():
                zero_copy(e).start()
            return carry

        def z_wait(e, carry):
            @pl.when(last_ref[e] >= 0)
            def _():
                zero_copy(e).wait()
            return carry

        lax.fori_loop(0, N_EXPERTS, z_start, 0)
        lax.fori_loop(0, N_EXPERTS, z_wait, 0)

    base = i * (2 * tm)

    def start(r, carry):
        _row_copy(xn_ref, r, xs_hbm, dest_ref[base + 2 * r], sem).start()
        _row_copy(xn_ref, r, xs_hbm, dest_ref[base + 2 * r + 1], sem).start()
        return carry

    def wait(r, carry):
        _row_copy(xn_ref, r, xs_hbm, dest_ref[base + 2 * r], sem).wait()
        _row_copy(xn_ref, r, xs_hbm, dest_ref[base + 2 * r + 1], sem).wait()
        return carry

    lax.fori_loop(0, tm, start, 0, unroll=ROW_DMA_UNROLL)
    lax.fori_loop(0, tm, wait, 0, unroll=ROW_DMA_UNROLL)


def _scatter(dest, last_tile, xn, *, rows, tm, tile, name):
    m, d = xn.shape
    return pl.pallas_call(
        functools.partial(_scatter_kernel, tile=tile),
        grid_spec=pltpu.PrefetchScalarGridSpec(
            num_scalar_prefetch=2,
            grid=(m // tm,),
            in_specs=[pl.BlockSpec((tm, d), lambda i, dest, last: (i, 0))],
            out_specs=pl.BlockSpec(memory_space=pl.ANY),
            scratch_shapes=[pltpu.VMEM((tile, d), F32), pltpu.SemaphoreType.DMA(()), pltpu.SemaphoreType.DMA(())],
        ),
        out_shape=jax.ShapeDtypeStruct((rows, d), F32),
        compiler_params=_cparams(("arbitrary",)),
        name=name,
    )(dest, last_tile, xn)


def _combine_kernel(dest_ref, x_ref, wt_ref, y_hbm, o_ref, ya, yb, sem):
    i = pl.program_id(0)
    tm = x_ref.shape[0]
    base = i * (2 * tm)

    def start(r, carry):
        _row_copy(y_hbm, dest_ref[base + 2 * r], ya, r, sem).start()
        _row_copy(y_hbm, dest_ref[base + 2 * r + 1], yb, r, sem).start()
        return carry

    def wait(r, carry):
        _row_copy(y_hbm, dest_ref[base + 2 * r], ya, r, sem).wait()
        _row_copy(y_hbm, dest_ref[base + 2 * r + 1], yb, r, sem).wait()
        return carry

    lax.fori_loop(0, tm, start, 0, unroll=ROW_DMA_UNROLL)
    lax.fori_loop(0, tm, wait, 0, unroll=ROW_DMA_UNROLL)
    wt = wt_ref[...]
    o_ref[...] = x_ref[...] + (wt[:, 2:3] * ya[...] + wt[:, 3:4] * yb[...])


def _combine(dest, x, wts, y, *, tm, name):
    m, d = x.shape
    return pl.pallas_call(
        _combine_kernel,
        grid_spec=pltpu.PrefetchScalarGridSpec(
            num_scalar_prefetch=1,
            grid=(m // tm,),
            in_specs=[
                pl.BlockSpec((tm, d), lambda i, dest: (i, 0)),
                pl.BlockSpec((tm, LANES), lambda i, dest: (i, 0)),
                pl.BlockSpec(memory_space=pl.ANY),
            ],
            out_specs=pl.BlockSpec((tm, d), lambda i, dest: (i, 0)),
            scratch_shapes=[pltpu.VMEM((tm, d), F32), pltpu.VMEM((tm, d), F32), pltpu.SemaphoreType.DMA(())],
        ),
        out_shape=jax.ShapeDtypeStruct((m, d), F32),
        compiler_params=_cparams(("arbitrary",)),
        name=name,
    )(dest, x, wts, y)


def _moe(x, g, w_r, b_r, w_gate, w_up, w_down, *, li, tm_tok, tm_ffn, gate_before_down, precise, name):
    m = x.shape[0]
    xn, route = _router(x, g, w_r, b_r, tm=tm_tok, precise=precise, name=name + "_router")
    eid = route[:, 0:2].astype(jnp.int32)
    counts = jnp.sum((eid[:, :, None] == jnp.arange(N_EXPERTS, dtype=jnp.int32)).astype(jnp.int32), axis=(0, 1))
    padded = ((counts + tm_ffn - 1) // tm_ffn) * tm_ffn
    ends = jnp.cumsum(padded)
    offs = ends - padded
    rows = 2 * m + N_EXPERTS * tm_ffn
    tile_start = jnp.arange(rows // tm_ffn, dtype=jnp.int32) * tm_ffn
    te = jnp.minimum(jnp.sum((ends[None, :] <= tile_start[:, None]).astype(jnp.int32), axis=1), N_EXPERTS - 1)
    nv = (ends[-1] // tm_ffn).reshape(1)
    last_tile = jnp.where(padded > 0, ends - tm_ffn, -1)
    offs_row = jnp.pad(offs.astype(F32), (0, LANES - N_EXPERTS)).reshape(1, LANES)
    dest = _rank(route, offs_row, tm=tm_tok, name=name + "_rank")[:, 0:2].reshape(-1)
    xs = _scatter(dest, last_tile, xn, rows=rows, tm=tm_tok, tile=tm_ffn, name=name + "_scatter")
    if gate_before_down:
        wrow = jnp.zeros((rows, 1), F32).at[dest, 0].set(route[:, 2:4].reshape(-1))
        wts = jnp.ones_like(route)
    else:
        wrow = jnp.ones((rows, 1), F32)
        wts = route
    y = _ffn(xs, wrow, te, nv, w_gate, w_up, w_down, li=li, tm=tm_ffn, precise=precise, name=name + "_ffn")
    return _combine(dest, x, wts, y, tm=tm_tok, name=name + "_combine")


def _rmsnorm_kernel(x_ref, g_ref, o_ref):
    xf = x_ref[...]
    r = lax.rsqrt(jnp.mean(xf * xf, axis=-1, keepdims=True) + EPS)
    o_ref[...] = (xf * r) * g_ref[...]


def _rmsnorm(x, g, *, tm, name):
    m, d = x.shape
    return pl.pallas_call(
        _rmsnorm_kernel,
        grid=(m // tm,),
        in_specs=[pl.BlockSpec((tm, d), lambda i: (i, 0)), pl.BlockSpec((1, d), lambda i: (0, 0))],
        out_specs=pl.BlockSpec((tm, d), lambda i: (i, 0)),
        out_shape=jax.ShapeDtypeStruct((m, d), F32),
        compiler_params=_cparams(("parallel",)),
        name=name,
    )(x, g)


def _layer_weights(p, li):
    b_if = jnp.pad(p["b_if"][li], (0, LANES - 2 * ML_HEADS)).reshape(1, LANES)
    w_r = jnp.pad(jnp.concatenate([p["w_rg"][li], p["w_re"][li]], axis=1),
                  ((0, 0), (0, LANES - N_GROUPS - N_EXPERTS)))
    b_r = jnp.pad(jnp.concatenate([p["b_rg"][li], p["b_re"][li]]), (0, LANES - N_GROUPS - N_EXPERTS)).reshape(1, LANES)
    return b_if, w_r, b_r


def _run_trunk(x3, p, layer_w, *, caches, c0s, n0s, m0s, pos0):
    bsz, seq, d = x3.shape
    m = bsz * seq
    x = x3.reshape(m, d)
    precise = False
    if caches is None:
        act_dtype, tag = BF16, "p"
        tm, tn, tm_tok, tm_ffn, seq_pad = 1024, 1024, 256, 256, seq
    else:
        act_dtype, tag = F32, "s"
        tm, tn, tm_tok, tm_ffn, seq_pad = m, 512, m, 16, 16
    cos_t, sin_t = _rope_tables(pos0 + jnp.arange(seq_pad, dtype=jnp.int32))
    zeros_bias = jnp.zeros((1, N_MAIN), F32)
    new_kv = [[] for _ in ATT_GROUPS]
    c_all, n_all, m_all = [], [], []
    for li in range(DEPTH):
        b_if, w_r, b_r = layer_w[li]
        g_attn = p["attn_norm_g"][li].reshape(1, d)
        common = dict(li=li, tm=tm, precise=precise)
        u = _norm_matmul(x, g_attn, p["w_in"], zeros_bias, row0=0, n_out=N_MAIN, tn=TN_MAIN,
                         out_dtype=act_dtype, act="none", name=f"{tag}{li}_inproj", **common)
        sg = _norm_matmul(x, g_attn, p["w_in"], zeros_bias, row0=O_GA, n_out=2 * D_MODEL, tn=tn,
                          out_dtype=act_dtype, act="sigmoid", name=f"{tag}{li}_gateproj", **common)
        gates = _norm_matmul(x, g_attn, p["w_in"], b_if, row0=O_MI, n_out=LANES, tn=LANES,
                             out_dtype=F32, act="gates", name=f"{tag}{li}_ifproj", **common)
        u3 = u.reshape(bsz, seq, N_MAIN)
        if seq_pad > seq:
            u3 = jnp.pad(u3, ((0, 0), (0, seq_pad - seq), (0, 0)))
        if caches is None:
            att3, kcs, vcs = _attn_prompt(u3, cos_t, sin_t, name=f"{tag}{li}_attn")
            for gi in range(N_ATT_GROUPS):
                keep = kcs[gi].shape[1]
                new_kv[gi].append(jnp.stack([kcs[gi].reshape(bsz, keep, ATT_HEADS, HEAD_DIM),
                                             vcs[gi].reshape(bsz, keep, ATT_HEADS, HEAD_DIM)], axis=2))
            att = att3.reshape(m, ATT_WIDTH)
        else:
            caches2d = [c.reshape(-1, HEAD_DIM) for c in caches]
            att3, kn3 = _attn_sample(u3, caches2d, cos_t, sin_t, li=li, n_new=seq, name=f"{tag}{li}_attn")
            att = att3[:, :seq].reshape(m, ATT_WIDTH)
            nh = N_ATT_GROUPS * ATT_HEADS
            k_new = kn3[:, :seq].reshape(bsz, seq, nh, HEAD_DIM)
            v_new = u3[:, :seq, O_AV:O_AV + nh * HEAD_DIM].reshape(bsz, seq, nh, HEAD_DIM)
            for gi in range(N_ATT_GROUPS):
                sl = slice(gi * ATT_HEADS, (gi + 1) * ATT_HEADS)
                new_kv[gi].append(jnp.stack([k_new[:, :, sl], v_new[:, :, sl]], axis=2))
        gc, gr = _gate_layouts(gates, bsz, seq, seq_pad)
        mlg = p["ml_norm_g"][li].reshape(1, ML_HEADS * ML_V)
        hg3, c_new, n_new, m_new = _mlstm(
            u3, gc, gr, mlg, c0s[li], n0s[li], m0s[li], chunk=min(128, seq_pad), sb=min(512, seq_pad),
            out_dtype=act_dtype, precise=precise, name=f"{tag}{li}_mlstm")
        hg = hg3[:, :seq].reshape(m, ML_HEADS * ML_V)
        c_all.append(c_new)
        n_all.append(n_new)
        m_all.append(m_new)
        merged = _merge(att, hg, sg, p["w_pa"], p["w_pm"], li=li, tm=tm, tn=tn, out_dtype=act_dtype,
                        precise=precise, name=f"{tag}{li}_merge")
        x = _outproj(x, merged, p["w_out"], li=li, tm=tm, tn=tn, precise=precise, name=f"{tag}{li}_outproj")
        x = _moe(x, p["ffn_norm_g"][li].reshape(1, d), w_r, b_r, p["w_gate"], p["w_up"], p["w_down"],
                 li=li, tm_tok=tm_tok, tm_ffn=tm_ffn, gate_before_down=caches is not None, precise=precise,
                 name=f"{tag}{li}_moe")
    y = _rmsnorm(x, p["final_norm_g"].reshape(1, d), tm=min(m, 512), name=f"{tag}_final_norm")
    new_kv = [jnp.stack(a) for a in new_kv]
    if caches is not None:
        shift = [(0, 0, 0)] * 2 + [(-seq, seq, 0)] + [(0, 0, 0)] * 3
        new_kv = [lax.dynamic_update_slice_in_dim(lax.pad(c, jnp.float32(0), shift), fresh, c.shape[2] - seq, axis=2)
                  for c, fresh in zip(caches, new_kv)]
    return (y.reshape(bsz, seq, d), new_kv, jnp.stack(c_all), jnp.stack(n_all), jnp.stack(m_all))


def kernel(x_prompt, x_sample, cache_kv_w128, cache_kv_w512, cache_kv_w2048, state_C, state_n, state_m,
           attn_norm_g, w_in, b_if, ml_norm_g, w_pa, w_pm, w_out, ffn_norm_g, w_rg, b_rg, w_re, b_re,
           w_gate, w_up, w_down, final_norm_g):
    p = dict(attn_norm_g=attn_norm_g, w_in=jnp.swapaxes(w_in, 1, 2), b_if=b_if, ml_norm_g=ml_norm_g, w_pa=w_pa, w_pm=w_pm,
             w_out=w_out, ffn_norm_g=ffn_norm_g, w_rg=w_rg, b_rg=b_rg, w_re=w_re, b_re=b_re,
             w_gate=w_gate, w_up=w_up, w_down=w_down, final_norm_g=final_norm_g)
    layer_w = [_layer_weights(p, li) for li in range(DEPTH)]
    bp = x_prompt.shape[0]
    c0 = jnp.zeros((DEPTH, bp, ML_HEADS, ML_V, ML_QK), F32)
    n0 = jnp.zeros((DEPTH, bp, ML_HEADS, ML_QK), F32)
    m0 = jnp.zeros((DEPTH, bp, ML_HEADS), F32)
    y_p, p_kv, p_c, p_n, p_m = _run_trunk(x_prompt, p, layer_w, caches=None,
                                          c0s=c0, n0s=n0, m0s=m0, pos0=0)
    y_s, s_kv, s_c, s_n, s_m = _run_trunk(x_sample, p, layer_w,
                                          caches=[cache_kv_w128, cache_kv_w512, cache_kv_w2048],
                                          c0s=state_C, n0s=state_n, m0s=state_m, pos0=PAST_LEN)
    return (y_p, y_s, p_kv[0], p_kv[1], p_kv[2], p_c, p_n, p_m,
            s_kv[0], s_kv[1], s_kv[2], s_c, s_n, s_m)
```
